```python
import math
import jax, jax.numpy as jnp
from jax import lax
import numpy as np

D_MODEL = 2048
BATCH = 1
SEQ = 8192
DEPTH = 1

CHUNK = 64
Q_BLOCK = 128
ROPE_THETA = 500000.0
EPS = 1e-6

MLA_HEADS = 8
MLA_Q_RANK = 768
MLA_KV_RANK = 512
MLA_NOPE = 128
MLA_ROPE = 64
MLA_V = 128

DIFF_HEADS = 8
DIFF_HEAD_DIM = 64
DIFF_ROT = DIFF_HEAD_DIM // 4
DIFF_QK = DIFF_HEADS * 2 * DIFF_HEAD_DIM
DIFF_VW = DIFF_HEADS * 2 * DIFF_HEAD_DIM

D_MIX = MLA_HEADS * MLA_V + DIFF_VW
IN_SPLITS = (MLA_Q_RANK, MLA_KV_RANK, MLA_ROPE, DIFF_QK, DIFF_QK, DIFF_VW)
D_IN = MLA_Q_RANK + MLA_KV_RANK + MLA_ROPE + 2 * DIFF_QK + DIFF_VW

N_EXPERTS = 64
TOP_K = 8
N_GROUPS = 8
TOPK_GROUPS = 4
D_EXPERT = 512
D_SHARED = 512
ROUTED_SCALE = 2.5
MOE_BLOCK = 128

kernel_name = "hybrid_mla_diffattn_moe_block"


def rmsnorm(x, g):
    xf = x.astype(jnp.float32)
    y = xf * lax.rsqrt(jnp.mean(xf * xf, axis=-1, keepdims=True) + EPS)
    return y.astype(x.dtype) * g


def rope_tables(seq, dim, dtype):
    pos = jnp.arange(seq, dtype=jnp.float32)
    inv = ROPE_THETA ** (-jnp.arange(0, dim, 2, dtype=jnp.float32) / dim)
    ang = pos[:, None] * inv[None, :]
    return jnp.cos(ang).astype(dtype), jnp.sin(ang).astype(dtype)


def rotate(x, cos, sin):
    half = x.shape[-1] // 2
    x1, x2 = x[..., :half], x[..., half:]
    return jnp.concatenate([x1 * cos - x2 * sin, x2 * cos + x1 * sin], axis=-1)


def to_blocks(a):
    b, s = a.shape[:2]
    nb = s // Q_BLOCK
    return jnp.moveaxis(a.reshape((b, nb, Q_BLOCK) + a.shape[2:]), 1, 0)


def from_blocks(a):
    a = jnp.moveaxis(a, 0, 1)
    return a.reshape((a.shape[0], a.shape[1] * a.shape[2]) + a.shape[3:])


def chunk_mask(start, seq):
    qpos = start + jnp.arange(Q_BLOCK, dtype=jnp.int32)
    kpos = jnp.arange(seq, dtype=jnp.int32)
    return (kpos[None, :] // CHUNK) <= (qpos[:, None] // CHUNK)


def sweep_query_blocks(fn, q_arrays):
    seq = q_arrays[0].shape[1]
    nb = seq // Q_BLOCK
    blocks = tuple(to_blocks(a) for a in q_arrays)
    starts = jnp.arange(nb, dtype=jnp.int32) * Q_BLOCK
    return from_blocks(lax.map(fn, blocks + (starts,)))


def mla_mixer(q_lat, kv_lat, k_pe, g_q, w_uq, g_kv, w_ukv):
    b, s, _ = q_lat.shape
    cos, sin = rope_tables(s, MLA_ROPE, q_lat.dtype)
    q = (rmsnorm(q_lat, g_q) @ w_uq).reshape(b, s, MLA_HEADS, MLA_NOPE + MLA_ROPE)
    q_nope = q[..., :MLA_NOPE]
    q_pe = rotate(q[..., MLA_NOPE:], cos[None, :, None, :], sin[None, :, None, :])
    kv = (rmsnorm(kv_lat, g_kv) @ w_ukv).reshape(b, s, MLA_HEADS, MLA_NOPE + MLA_V)
    k_nope, v = kv[..., :MLA_NOPE], kv[..., MLA_NOPE:]
    k_pe = rotate(k_pe, cos[None], sin[None])
    scale = (MLA_NOPE + MLA_ROPE) ** -0.5

    def block(args):
        qn, qp, start = args
        sc = (jnp.einsum('bqhd,bkhd->bhqk', qn, k_nope)
              + jnp.einsum('bqhd,bkd->bhqk', qp, k_pe)).astype(jnp.float32) * scale
        sc = jnp.where(chunk_mask(start, s)[None, None], sc, -jnp.inf)
        p = jax.nn.softmax(sc, axis=-1).astype(v.dtype)
        return jnp.einsum('bhqk,bkhd->bqhd', p, v)

    out = sweep_query_blocks(block, (q_nope, q_pe))
    return out.reshape(b, s, MLA_HEADS * MLA_V)


def diff_mixer(q, k, v, lq1, lk1, lq2, lk2, g_sub, lambda_init):
    b, s, _ = q.shape
    cos, sin = rope_tables(s, DIFF_ROT, q.dtype)
    cb, sb = cos[None, :, None, None, :], sin[None, :, None, None, :]
    q = q.reshape(b, s, DIFF_HEADS, 2, DIFF_HEAD_DIM)
    k = k.reshape(b, s, DIFF_HEADS, 2, DIFF_HEAD_DIM)
    v = v.reshape(b, s, DIFF_HEADS, 2 * DIFF_HEAD_DIM)
    q = jnp.concatenate([rotate(q[..., :DIFF_ROT], cb, sb), q[..., DIFF_ROT:]], axis=-1)
    k = jnp.concatenate([rotate(k[..., :DIFF_ROT], cb, sb), k[..., DIFF_ROT:]], axis=-1)
    lam = (jnp.exp(jnp.sum(lq1.astype(jnp.float32) * lk1.astype(jnp.float32)))
           - jnp.exp(jnp.sum(lq2.astype(jnp.float32) * lk2.astype(jnp.float32)))
           + lambda_init)
    scale = DIFF_HEAD_DIM ** -0.5

    def block(args):
        qb, start = args
        sc = jnp.einsum('bqhcd,bkhcd->cbhqk', qb, k).astype(jnp.float32) * scale
        sc = jnp.where(chunk_mask(start, s)[None, None, None], sc, -jnp.inf)
        p = jax.nn.softmax(sc, axis=-1)
        a = (p[0] - lam * p[1]).astype(v.dtype)
        return jnp.einsum('bhqk,bkhe->bqhe', a, v)

    out = sweep_query_blocks(block, (q,))
    out = rmsnorm(out, g_sub) * (1.0 - lambda_init)
    return out.reshape(b, s, DIFF_VW)


def swiglu(h, w_gate, w_up, w_down):
    return (jax.nn.silu(h @ w_gate) * (h @ w_up)) @ w_down


def moe(h, w_router, b_router, w_gate, w_up, w_down, ws_gate, ws_up, ws_down):
    b, s, d = h.shape
    t = b * s
    hf = h.reshape(t, d)
    scores = jax.nn.sigmoid(hf.astype(jnp.float32) @ w_router.astype(jnp.float32))
    biased = scores + b_router.astype(jnp.float32)
    grouped = biased.reshape(t, N_GROUPS, N_EXPERTS // N_GROUPS)
    gscore = jnp.sum(lax.top_k(grouped, 2)[0], axis=-1)
    _, gidx = lax.top_k(gscore, TOPK_GROUPS)
    gmask = jnp.zeros((t, N_GROUPS), dtype=bool).at[jnp.arange(t)[:, None], gidx].set(True)
    emask = jnp.repeat(gmask, N_EXPERTS // N_GROUPS, axis=1)
    _, eidx = lax.top_k(jnp.where(emask, biased, -jnp.inf), TOP_K)
    wts = jnp.take_along_axis(scores, eidx, axis=1)
    wts = wts / jnp.sum(wts, axis=-1, keepdims=True) * ROUTED_SCALE

    tk = t * TOP_K
    e_flat = eidx.reshape(tk).astype(jnp.int32)
    tok_flat = jnp.repeat(jnp.arange(t, dtype=jnp.int32), TOP_K)
    w_flat = wts.reshape(tk)
    order = jnp.argsort(e_flat, stable=True)
    se = e_flat[order]
    counts = jnp.bincount(e_flat, length=N_EXPERTS).astype(jnp.int32)
    padded = (counts + MOE_BLOCK - 1) // MOE_BLOCK * MOE_BLOCK
    pend = jnp.cumsum(padded)
    pstart = pend - padded
    gstart = jnp.cumsum(counts) - counts
    dest = pstart[se] + jnp.arange(tk, dtype=jnp.int32) - gstart[se]
    m_pad = tk + N_EXPERTS * MOE_BLOCK
    n_blk = m_pad // MOE_BLOCK
    row_tok = jnp.full((m_pad,), t, dtype=jnp.int32).at[dest].set(tok_flat[order])
    row_w = jnp.zeros((m_pad,), dtype=jnp.float32).at[dest].set(w_flat[order])
    blk_start = jnp.arange(n_blk, dtype=jnp.int32) * MOE_BLOCK
    blk_e = jnp.minimum(jnp.searchsorted(pend, blk_start, side='right'), N_EXPERTS - 1)
    h_pad = jnp.concatenate([hf, jnp.zeros((1, d), dtype=hf.dtype)], axis=0)

    def expert_block(args):
        rows, rw, e = args
        y = swiglu(h_pad[rows], w_gate[e], w_up[e], w_down[e])
        return y * rw[:, None].astype(y.dtype)

    ys = lax.map(expert_block, (row_tok.reshape(n_blk, MOE_BLOCK),
                                row_w.reshape(n_blk, MOE_BLOCK), blk_e))
    routed = jax.ops.segment_sum(ys.reshape(m_pad, d), row_tok, num_segments=t + 1)[:t]
    shared = swiglu(hf, ws_gate, ws_up, ws_down)
    return (shared + routed).reshape(b, s, d)


def setup_inputs(seed: int = 0) -> dict:
    key = jax.random.key(seed)
    ks = jax.random.split(key, 32)
    f32 = jnp.float32
    L, D = DEPTH, D_MODEL

    def nrm(k, shape, scale):
        return jax.random.normal(k, shape, dtype=f32) * scale

    def gain(k, shape):
        return 1.0 + 0.02 * jax.random.normal(k, shape, dtype=f32)

    return {
        "x": nrm(ks[0], (BATCH, SEQ, D), 1.0),
        "c": nrm(ks[1], (BATCH, D), 1.0),
        "w_ada": nrm(ks[2], (L, D, 6 * D), 0.5 * D ** -0.5),
        "b_ada": nrm(ks[3], (L, 6 * D), 0.02),
        "g_pre_mix": gain(ks[4], (L, D)),
        "w_in": nrm(ks[5], (L, D, D_IN), D ** -0.5),
        "g_q_lat": gain(ks[6], (L, MLA_Q_RANK)),
        "w_uq": nrm(ks[7], (L, MLA_Q_RANK, MLA_HEADS * (MLA_NOPE + MLA_ROPE)), MLA_Q_RANK ** -0.5),
        "g_kv_lat": gain(ks[8], (L, MLA_KV_RANK)),
        "w_ukv": nrm(ks[9], (L, MLA_KV_RANK, MLA_HEADS * (MLA_NOPE + MLA_V)), MLA_KV_RANK ** -0.5),
        "lambda_q1": nrm(ks[10], (L, DIFF_HEAD_DIM), 0.1),
        "lambda_k1": nrm(ks[11], (L, DIFF_HEAD_DIM), 0.1),
        "lambda_q2": nrm(ks[12], (L, DIFF_HEAD_DIM), 0.1),
        "lambda_k2": nrm(ks[13], (L, DIFF_HEAD_DIM), 0.1),
        "g_diff_sub": gain(ks[14], (L, 2 * DIFF_HEAD_DIM)),
        "w_out": nrm(ks[15], (L, D_MIX, D), D_MIX ** -0.5),
        "g_post_mix": gain(ks[16], (L, D)),
        "g_pre_ffn": gain(ks[17], (L, D)),
        "w_router": nrm(ks[18], (L, D, N_EXPERTS), D ** -0.5),
        "b_router": nrm(ks[19], (L, N_EXPERTS), 0.01),
        "w_gate": nrm(ks[20], (L, N_EXPERTS, D, D_EXPERT), D ** -0.5),
        "w_up": nrm(ks[21], (L, N_EXPERTS, D, D_EXPERT), D ** -0.5),
        "w_down": nrm(ks[22], (L, N_EXPERTS, D_EXPERT, D), D_EXPERT ** -0.5),
        "ws_gate": nrm(ks[23], (L, D, D_SHARED), D ** -0.5),
        "ws_up": nrm(ks[24], (L, D, D_SHARED), D ** -0.5),
        "ws_down": nrm(ks[25], (L, D_SHARED, D), D_SHARED ** -0.5),
        "g_post_ffn": gain(ks[26], (L, D)),
    }


def reference(x, c, w_ada, b_ada, g_pre_mix, w_in, g_q_lat, w_uq, g_kv_lat, w_ukv,
              lambda_q1, lambda_k1, lambda_q2, lambda_k2, g_diff_sub, w_out, g_post_mix,
              g_pre_ffn, w_router, b_router, w_gate, w_up, w_down,
              ws_gate, ws_up, ws_down, g_post_ffn):
    offs = []
    acc = 0
    for wdt in IN_SPLITS[:-1]:
        acc += wdt
        offs.append(acc)
    for l in range(DEPTH):
        lambda_init = 0.8 - 0.6 * math.exp(-0.3 * l)
        mod = (jax.nn.silu(c) @ w_ada[l] + b_ada[l])[:, None, :]
        sh_a, sc_a, gt_a, sh_f, sc_f, gt_f = jnp.split(mod, 6, axis=-1)

        h = rmsnorm(x, g_pre_mix[l]) * (1.0 + sc_a) + sh_a
        proj = h @ w_in[l]
        q_lat, kv_lat, k_pe, dq, dk, dv = jnp.split(proj, offs, axis=-1)
        o_mla = mla_mixer(q_lat, kv_lat, k_pe, g_q_lat[l], w_uq[l], g_kv_lat[l], w_ukv[l])
        o_diff = diff_mixer(dq, dk, dv, lambda_q1[l], lambda_k1[l], lambda_q2[l], lambda_k2[l],
                            g_diff_sub[l], lambda_init)
        y = jnp.concatenate([o_mla, o_diff], axis=-1) @ w_out[l]
        x = x + gt_a * rmsnorm(y, g_post_mix[l])

        h = rmsnorm(x, g_pre_ffn[l]) * (1.0 + sc_f) + sh_f
        y = moe(h, w_router[l], b_router[l], w_gate[l], w_up[l], w_down[l],
                ws_gate[l], ws_up[l], ws_down[l])
        x = x + gt_f * rmsnorm(y, g_post_ffn[l])
    return x
```

```python
import functools
import math

import jax
import jax.numpy as jnp
import numpy as np
from jax import lax
from jax.experimental import pallas as pl
from jax.experimental.pallas import tpu as pltpu

F32 = jnp.float32
BF16 = jnp.bfloat16
U32 = jnp.uint32
I32 = jnp.int32

D_MODEL = 2048
CHUNK = 64
ROPE_THETA = 500000.0
EPS = 1e-6
LOG2E = 1.4426950408889634

MLA_HEADS = 8
MLA_Q_RANK = 768
MLA_KV_RANK = 512
MLA_NOPE = 128
MLA_ROPE = 64
MLA_V = 128
MLA_QK_PAD = 256

DIFF_HEADS = 8
DIFF_HEAD_DIM = 64
DIFF_ROT = DIFF_HEAD_DIM // 4
DIFF_W = DIFF_HEADS * 2 * DIFF_HEAD_DIM

N_EXPERTS = 64
TOP_K = 8
N_GROUPS = 8
GROUP_SIZE = N_EXPERTS // N_GROUPS
TOPK_GROUPS = 4
D_EXPERT = 512
ROUTED_SCALE = 2.5
LAMBDA_INIT = 0.8 - 0.6 * math.exp(-0.3 * 0)

LANES = 128
SUBLANES = 8
HALF = D_MODEL // 2
ROW_TILES = HALF // LANES

TM_PROJ = 256
BQ = 512
EXP_BLK = 256
TD = 128
TR = 512
TN_ADA = 1024

VMEM_LIMIT = 56 * 1024 * 1024

assert ROW_TILES == SUBLANES


def _cparams(sem, vmem=VMEM_LIMIT):
    return pltpu.CompilerParams(dimension_semantics=sem, vmem_limit_bytes=vmem)


def _dot(a, b):
    return jnp.dot(a, b, preferred_element_type=F32)


def _dot_nt(a, b):
    return lax.dot_general(a, b, (((1,), (1,)), ((), ())), preferred_element_type=F32)


def _rms(x):
    return x * lax.rsqrt(jnp.mean(x * x, axis=-1, keepdims=True) + EPS)


def _split_bf16(x):
    hi = x.astype(BF16)
    lo = (x - hi.astype(F32)).astype(BF16)
    return hi, lo


def _const_spec(shape):
    nd = len(shape)
    return pl.BlockSpec(shape, lambda *a: (0,) * nd, pipeline_mode=pl.Buffered(1))


def _ada_kernel(c_ref, w_ref, b_ref, o_ref):
    c = c_ref[...]
    a = c * jax.nn.sigmoid(c)
    a8 = jnp.broadcast_to(a, (SUBLANES, a.shape[1]))
    a_hi, a_lo = _split_bf16(a8)
    w_hi, w_lo = _split_bf16(w_ref[...])
    r = _dot(a_hi, w_hi) + _dot(a_lo, w_hi) + _dot(a_hi, w_lo)
    o_ref[...] = r[0:1] + b_ref[...]


def _ada(c, w, b):
    d, n = w.shape
    return pl.pallas_call(
        _ada_kernel,
        grid=(n // TN_ADA,),
        in_specs=[pl.BlockSpec((1, d), lambda j: (0, 0)),
                  pl.BlockSpec((d, TN_ADA), lambda j: (0, j)),
                  pl.BlockSpec((1, TN_ADA), lambda j: (0, j))],
        out_specs=pl.BlockSpec((1, TN_ADA), lambda j: (0, j)),
        out_shape=jax.ShapeDtypeStruct((1, n), F32),
        compiler_params=_cparams(("arbitrary",)),
        name="ada_mod",
    )(c, w, b)


def _tile_lanes(t, reps):
    return jnp.concatenate([t] * reps, axis=1)


def _rope_lanes(x, c, sa, sb, half):
    n = x.shape[1]
    return x * c + pltpu.roll(x, n - half, 1) * sa + pltpu.roll(x, half, 1) * sb


OFF_LAT = MLA_Q_RANK + MLA_KV_RANK
OFF_DQ = OFF_LAT
OFF_DK = OFF_DQ + DIFF_W
OFF_DV = OFF_DK + DIFF_W
OFF_KPE = OFF_DV + DIFF_W
D_IN_PAD = OFF_KPE + LANES


def _inproj_kernel(x_ref, g_ref, sc_ref, sh_ref, w_ref, c_ref, sa_ref, sb_ref,
                   lat_ref, kpe_ref, dq_ref, dk_ref, dv_ref):
    x = x_ref[...]
    h = _rms(x) * g_ref[...] * (1.0 + sc_ref[...]) + sh_ref[...]
    hb = h.astype(BF16)
    reps = DIFF_W // LANES
    c = _tile_lanes(c_ref[...], reps)
    sa = _tile_lanes(sa_ref[...], reps)
    sb = _tile_lanes(sb_ref[...], reps)
    half = DIFF_ROT // 2
    lat_ref[...] = _dot(hb, w_ref[:, 0:OFF_LAT])
    q = _dot(hb, w_ref[:, OFF_DQ:OFF_DK])
    dq_ref[...] = (_rope_lanes(q, c, sa, sb, half) * (DIFF_HEAD_DIM ** -0.5 * LOG2E)).astype(BF16)
    k = _dot(hb, w_ref[:, OFF_DK:OFF_DV])
    dk_ref[...] = _rope_lanes(k, c, sa, sb, half).astype(BF16)
    dv_ref[...] = _dot(hb, w_ref[:, OFF_DV:OFF_KPE]).astype(BF16)
    kpe_ref[...] = _dot(hb, w_ref[:, OFF_KPE:D_IN_PAD])


def _inproj(x, g, sc, sh, w_r, tabs):
    s, d = x.shape
    tm = min(TM_PROJ, s)
    row = lambda n: pl.BlockSpec((tm, n), lambda i: (i, 0))
    return pl.pallas_call(
        _inproj_kernel,
        grid=(s // tm,),
        in_specs=[row(d), _const_spec((1, d)), _const_spec((1, d)), _const_spec((1, d)),
                  _const_spec(w_r.shape), row(LANES), row(LANES), row(LANES)],
        out_specs=[row(OFF_LAT), row(LANES), row(DIFF_W), row(DIFF_W), row(DIFF_W)],
        out_shape=[jax.ShapeDtypeStruct((s, OFF_LAT), F32),
                   jax.ShapeDtypeStruct((s, LANES), F32),
                   jax.ShapeDtypeStruct((s, DIFF_W), BF16),
                   jax.ShapeDtypeStruct((s, DIFF_W), BF16),
                   jax.ShapeDtypeStruct((s, DIFF_W), BF16)],
        compiler_params=_cparams(("arbitrary",)),
        name="in_proj",
    )(x, g, sc, sh, w_r, *tabs)


def _mla_prep_kernel(lat_ref, kpe_ref, gq_ref, gkv_ref, wuq_ref, wukv_ref,
                     c_ref, sa_ref, sb_ref, q_ref, k_ref, v_ref):
    lat = lat_ref[...]
    qn = (_rms(lat[:, :MLA_Q_RANK]) * gq_ref[...]).astype(BF16)
    kvn = (_rms(lat[:, MLA_Q_RANK:]) * gkv_ref[...]).astype(BF16)
    q = _dot(qn, wuq_ref[...])
    kv = _dot(kvn, wukv_ref[...])
    c, sa, sb = c_ref[...], sa_ref[...], sb_ref[...]
    half = MLA_ROPE // 2
    qs = (MLA_NOPE + MLA_ROPE) ** -0.5 * LOG2E
    kpe = _rope_lanes(kpe_ref[...], c, sa, sb, half).astype(BF16)
    for h in range(MLA_HEADS):
        o = h * MLA_QK_PAD
        q_ref[:, o:o + LANES] = (q[:, o:o + LANES] * qs).astype(BF16)
        q_ref[:, o + LANES:o + 2 * LANES] = (
            _rope_lanes(q[:, o + LANES:o + 2 * LANES], c, sa, sb, half) * qs).astype(BF16)
        k_ref[:, o:o + LANES] = kv[:, h * MLA_NOPE:(h + 1) * MLA_NOPE].astype(BF16)
        k_ref[:, o + LANES:o + 2 * LANES] = kpe
    v_ref[...] = kv[:, MLA_HEADS * MLA_NOPE:].astype(BF16)


def _mla_prep(lat, kpe, gq, gkv, wuq_r, wukv_r, tabs):
    s = lat.shape[0]
    tm = min(TM_PROJ, s)
    row = lambda n: pl.BlockSpec((tm, n), lambda i: (i, 0))
    hq = MLA_HEADS * MLA_QK_PAD
    hv = MLA_HEADS * MLA_V
    return pl.pallas_call(
        _mla_prep_kernel,
        grid=(s // tm,),
        in_specs=[row(OFF_LAT), row(LANES), _const_spec(gq.shape), _const_spec(gkv.shape),
                  _const_spec(wuq_r.shape), _const_spec(wukv_r.shape),
                  row(LANES), row(LANES), row(LANES)],
        out_specs=[row(hq), row(hq), row(hv)],
        out_shape=[jax.ShapeDtypeStruct((s, hq), BF16),
                   jax.ShapeDtypeStruct((s, hq), BF16),
                   jax.ShapeDtypeStruct((s, hv), BF16)],
        compiler_params=_cparams(("arbitrary",)),
        name="mla_prep",
    )(lat, kpe, gq, gkv, wuq_r, wukv_r, *tabs)


def _pair_tables(nq):
    qi = np.concatenate([np.full(i + 1, i) for i in range(nq)]).astype(np.int32)
    kj = np.concatenate([np.arange(i + 1) for i in range(nq)]).astype(np.int32)
    return jnp.asarray(qi), jnp.asarray(kj)


def _diag_mask(bq):
    r = lax.broadcasted_iota(I32, (bq, bq), 0) // CHUNK
    c = lax.broadcasted_iota(I32, (bq, bq), 1) // CHUNK
    return c <= r


def _online_update(s, v, m_scr, l_scr, acc_scr):
    m_prev = m_scr[...]
    m_new = jnp.maximum(m_prev, jnp.max(s, axis=1, keepdims=True))
    alpha = jnp.exp2(m_prev - m_new)
    p = jnp.exp2(s - m_new)
    l_scr[...] = alpha * l_scr[...] + jnp.sum(p, axis=1, keepdims=True)
    acc_scr[...] = alpha * acc_scr[...] + _dot(p.astype(BF16), v)
    m_scr[...] = m_new


def _mla_attn_kernel(qi_ref, kj_ref, q_ref, k_ref, v_ref, o_ref, m_scr, l_scr, acc_scr):
    t = pl.program_id(1)
    i = qi_ref[t]
    j = kj_ref[t]

    @pl.when(j == 0)
    def _():
        m_scr[...] = jnp.full(m_scr.shape, -jnp.inf, F32)
        l_scr[...] = jnp.zeros(l_scr.shape, F32)
        acc_scr[...] = jnp.zeros(acc_scr.shape, F32)

    @pl.when(j < i)
    def _():
        s = _dot_nt(q_ref[...], k_ref[...])
        _online_update(s, v_ref[...], m_scr, l_scr, acc_scr)

    @pl.when(j == i)
    def _():
        s = _dot_nt(q_ref[...], k_ref[...])
        s = jnp.where(_diag_mask(s.shape[0]), s, -jnp.inf)
        _online_update(s, v_ref[...], m_scr, l_scr, acc_scr)
        o_ref[...] = (acc_scr[...] / l_scr[...]).astype(o_ref.dtype)


def _mla_attn(qc, kc, v):
    s = qc.shape[0]
    bq = min(BQ, s)
    nq = s // bq
    qi, kj = _pair_tables(nq)
    gs = pltpu.PrefetchScalarGridSpec(
        num_scalar_prefetch=2,
        grid=(MLA_HEADS, qi.shape[0]),
        in_specs=[pl.BlockSpec((bq, MLA_QK_PAD), lambda h, t, qi, kj: (qi[t], h)),
                  pl.BlockSpec((bq, MLA_QK_PAD), lambda h, t, qi, kj: (kj[t], h)),
                  pl.BlockSpec((bq, MLA_V), lambda h, t, qi, kj: (kj[t], h))],
        out_specs=pl.BlockSpec((bq, MLA_V), lambda h, t, qi, kj: (qi[t], h)),
        scratch_shapes=[pltpu.VMEM((bq, 1), F32), pltpu.VMEM((bq, 1), F32),
                        pltpu.VMEM((bq, MLA_V), F32)],
    )
    return pl.pallas_call(
        _mla_attn_kernel,
        grid_spec=gs,
        out_shape=jax.ShapeDtypeStruct((s, MLA_HEADS * MLA_V), BF16),
        compiler_params=_cparams(("arbitrary", "arbitrary")),
        name="mla_attn",
    )(qi, kj, qc, kc, v)


def _diff_attn_kernel(qi_ref, kj_ref, q_ref, k_ref, v_ref, lq1_ref, lk1_ref, lq2_ref, lk2_ref,
                      g_ref, o_ref, q0_scr, q1_scr, m0, l0, a0, m1, l1, a1):
    t = pl.program_id(1)
    i = qi_ref[t]
    j = kj_ref[t]

    @pl.when(j == 0)
    def _():
        q = q_ref[...]
        lane = lax.broadcasted_iota(I32, q.shape, 1)
        zero = jnp.zeros_like(q)
        q0_scr[...] = jnp.where(lane < DIFF_HEAD_DIM, q, zero)
        q1_scr[...] = jnp.where(lane >= DIFF_HEAD_DIM, q, zero)
        for m, l, a in ((m0, l0, a0), (m1, l1, a1)):
            m[...] = jnp.full(m.shape, -jnp.inf, F32)
            l[...] = jnp.zeros(l.shape, F32)
            a[...] = jnp.zeros(a.shape, F32)

    def step(masked):
        k = k_ref[...]
        v = v_ref[...]
        for qs, m, l, a in ((q0_scr, m0, l0, a0), (q1_scr, m1, l1, a1)):
            s = _dot_nt(qs[...], k)
            if masked:
                s = jnp.where(_diag_mask(s.shape[0]), s, -jnp.inf)
            _online_update(s, v, m, l, a)

    @pl.when(j < i)
    def _():
        step(False)

    @pl.when(j == i)
    def _():
        step(True)
        lam = (jnp.exp(jnp.sum(lq1_ref[...] * lk1_ref[...], axis=1, keepdims=True))
               - jnp.exp(jnp.sum(lq2_ref[...] * lk2_ref[...], axis=1, keepdims=True))
               + LAMBDA_INIT)
        o = a0[...] / l0[...] - lam * (a1[...] / l1[...])
        o_ref[...] = (_rms(o) * g_ref[...] * (1.0 - LAMBDA_INIT)).astype(o_ref.dtype)


def _diff_attn(dq, dk, dv, lq1, lk1, lq2, lk2, g_sub):
    s = dq.shape[0]
    bq = min(BQ, s)
    nq = s // bq
    qi, kj = _pair_tables(nq)
    hw = 2 * DIFF_HEAD_DIM
    small = lambda a: pl.BlockSpec(a.shape, lambda h, t, qi, kj: (0, 0))
    gs = pltpu.PrefetchScalarGridSpec(
        num_scalar_prefetch=2,
        grid=(DIFF_HEADS, qi.shape[0]),
        in_specs=[pl.BlockSpec((bq, hw), lambda h, t, qi, kj: (qi[t], h)),
                  pl.BlockSpec((bq, hw), lambda h, t, qi, kj: (kj[t], h)),
                  pl.BlockSpec((bq, hw), lambda h, t, qi, kj: (kj[t], h)),
                  small(lq1), small(lk1), small(lq2), small(lk2), small(g_sub)],
        out_specs=pl.BlockSpec((bq, hw), lambda h, t, qi, kj: (qi[t], h)),
        scratch_shapes=[pltpu.VMEM((bq, hw), BF16), pltpu.VMEM((bq, hw), BF16),
                        pltpu.VMEM((bq, 1), F32), pltpu.VMEM((bq, 1), F32), pltpu.VMEM((bq, hw), F32),
                        pltpu.VMEM((bq, 1), F32), pltpu.VMEM((bq, 1), F32), pltpu.VMEM((bq, hw), F32)],
    )
    return pl.pallas_call(
        _diff_attn_kernel,
        grid_spec=gs,
        out_shape=jax.ShapeDtypeStruct((s, DIFF_W), BF16),
        compiler_params=_cparams(("arbitrary", "arbitrary")),
        name="diff_attn",
    )(qi, kj, dq, dk, dv, lq1, lk1, lq2, lk2, g_sub)


def _pack_words(y):
    lo = pltpu.bitcast(y[:, :HALF].astype(BF16).astype(F32), U32) >> 16
    hi = pltpu.bitcast(y[:, HALF:].astype(BF16).astype(F32), U32) & jnp.uint32(0xFFFF0000)
    return lo | hi


def _store_packed(ref, words, rows):
    for s in range(ROW_TILES):
        ref[pl.ds(s, rows, stride=ROW_TILES), :] = words[:, s * LANES:(s + 1) * LANES]


def _load_packed(ref, rows):
    return jnp.concatenate(
        [ref[pl.ds(s, rows, stride=ROW_TILES), :] for s in range(ROW_TILES)], axis=1)


def _unpack_words(w):
    lo = pltpu.bitcast(w << 16, F32)
    hi = pltpu.bitcast(w & jnp.uint32(0xFFFF0000), F32)
    return lo, hi


def _outproj_kernel(om_ref, od_ref, x_ref, w_ref, gt_ref, gpost_ref, gpre_ref, sc_ref, sh_ref,
                    wr_ref, x1_ref, h2_ref, h2p_ref, lg_ref):
    nm = om_ref.shape[1]
    y = _dot(om_ref[...], w_ref[0:nm, :]) + _dot(od_ref[...], w_ref[nm:, :])
    x1 = x_ref[...] + gt_ref[...] * (_rms(y) * gpost_ref[...])
    x1_ref[...] = x1
    h2 = _rms(x1) * gpre_ref[...] * (1.0 + sc_ref[...]) + sh_ref[...]
    h2_ref[...] = h2.astype(BF16)
    _store_packed(h2p_ref, _pack_words(h2), h2.shape[0])
    h_hi, h_lo = _split_bf16(h2)
    w_hi, w_lo = _split_bf16(wr_ref[...])
    lg_ref[...] = _dot_nt(w_hi, h_hi) + _dot_nt(w_hi, h_lo) + _dot_nt(w_lo, h_hi)


def _outproj(om, od, x, w_out, gt, gpost, gpre, sc, sh, wr_t):
    s, d = x.shape
    tm = min(TM_PROJ, s)
    row = lambda n: pl.BlockSpec((tm, n), lambda i: (i, 0))
    vec = _const_spec((1, d))
    return pl.pallas_call(
        _outproj_kernel,
        grid=(s // tm,),
        in_specs=[row(om.shape[1]), row(od.shape[1]), row(d), _const_spec(w_out.shape),
                  vec, vec, vec, vec, vec, _const_spec(wr_t.shape)],
        out_specs=[row(d), row(d),
                   pl.BlockSpec((tm * ROW_TILES, LANES), lambda i: (i, 0)),
                   pl.BlockSpec((N_EXPERTS, tm), lambda i: (0, i))],
        out_shape=[jax.ShapeDtypeStruct((s, d), F32),
                   jax.ShapeDtypeStruct((s, d), BF16),
                   jax.ShapeDtypeStruct((s * ROW_TILES, LANES), U32),
                   jax.ShapeDtypeStruct((N_EXPERTS, s), F32)],
        compiler_params=_cparams(("arbitrary",)),
        name="out_proj",
    )(om, od, x, w_out, gt, gpost, gpre, sc, sh, wr_t)


def _route_kernel(lg_ref, b_ref, eidx_ref, w_ref, rank_ref, cnt_ref, carry):
    step = pl.program_id(0)

    @pl.when(step == 0)
    def _():
        carry[...] = jnp.zeros(carry.shape, F32)

    tr = lg_ref.shape[1]
    ninf = jnp.float32(-jnp.inf)
    ie = lax.broadcasted_iota(I32, (GROUP_SIZE, tr), 0)
    rmax = lambda a: jnp.max(a, axis=0, keepdims=True)
    rmin = lambda a: jnp.min(a, axis=0, keepdims=True)
    rsum = lambda a: jnp.sum(a, axis=0, keepdims=True)

    sc, bi, gscore = [], [], []
    for g in range(N_GROUPS):
        lg = lg_ref[g * GROUP_SIZE:(g + 1) * GROUP_SIZE, :]
        s = jax.nn.sigmoid(lg)
        b = s + b_ref[g * GROUP_SIZE:(g + 1) * GROUP_SIZE, :]
        m1 = rmax(b)
        i1 = rmin(jnp.where(b == m1, ie, GROUP_SIZE))
        m2 = rmax(jnp.where(ie == i1, ninf, b))
        sc.append(s)
        bi.append(b)
        gscore.append(m1 + m2)

    gsel = [jnp.zeros((1, tr), I32) for _ in range(N_GROUPS)]
    for _ in range(TOPK_GROUPS):
        gm = functools.reduce(jnp.maximum, gscore)
        gi = functools.reduce(
            jnp.minimum, [jnp.where(gscore[g] == gm, g, N_GROUPS) for g in range(N_GROUPS)])
        for g in range(N_GROUPS):
            hit = gi == g
            gsel[g] = jnp.where(hit, 1, gsel[g])
            gscore[g] = jnp.where(hit, ninf, gscore[g])

    masked = [jnp.where(jnp.broadcast_to(gsel[g], (GROUP_SIZE, tr)) > 0, bi[g], ninf)
              for g in range(N_GROUPS)]
    sel = [jnp.zeros((GROUP_SIZE, tr), F32) for _ in range(N_GROUPS)]
    idxs, svals = [], []
    for _ in range(TOP_K):
        m = functools.reduce(jnp.maximum, [rmax(x) for x in masked])
        idx = functools.reduce(
            jnp.minimum,
            [rmin(jnp.where(masked[g] == m, ie + g * GROUP_SIZE, N_EXPERTS)) for g in range(N_GROUPS)])
        sv = jnp.zeros((1, tr), F32)
        for g in range(N_GROUPS):
            hit = (ie + g * GROUP_SIZE) == idx
            sv = sv + rsum(jnp.where(hit, sc[g], 0.0))
            masked[g] = jnp.where(hit, ninf, masked[g])
            sel[g] = jnp.where(hit, 1.0, sel[g])
        idxs.append(idx)
        svals.append(sv)
    tot = functools.reduce(lambda a, b: a + b, svals)

    sel2 = jnp.concatenate(sel, axis=0)
    before = (lax.broadcasted_iota(I32, (tr, tr), 0) < lax.broadcasted_iota(I32, (tr, tr), 1))
    rank2 = _dot(sel2.astype(BF16), before.astype(BF16)) + carry[:, 0:1]
    for k in range(TOP_K):
        rk = jnp.zeros((1, tr), F32)
        for g in range(N_GROUPS):
            hit = (ie + g * GROUP_SIZE) == idxs[k]
            rk = rk + rsum(jnp.where(hit, rank2[g * GROUP_SIZE:(g + 1) * GROUP_SIZE, :], 0.0))
        eidx_ref[k:k + 1, :] = idxs[k]
        w_ref[k:k + 1, :] = svals[k] / tot * ROUTED_SCALE
        rank_ref[k:k + 1, :] = rk.astype(I32)
    carry[...] = carry[...] + jnp.sum(sel2, axis=1, keepdims=True)
    cnt_ref[...] = carry[...]


def _route(lg_t, b_col):
    e, t = lg_t.shape
    tr = min(TR, t)
    tok = lambda: pl.BlockSpec((TOP_K, tr), lambda i: (0, i))
    return pl.pallas_call(
        _route_kernel,
        grid=(t // tr,),
        in_specs=[pl.BlockSpec((e, tr), lambda i: (0, i)), pl.BlockSpec((e, 1), lambda i: (0, 0))],
        out_specs=[tok(), tok(), tok(), pl.BlockSpec((e, LANES), lambda i: (0, 0))],
        out_shape=[jax.ShapeDtypeStruct((TOP_K, t), I32),
                   jax.ShapeDtypeStruct((TOP_K, t), F32),
                   jax.ShapeDtypeStruct((TOP_K, t), I32),
                   jax.ShapeDtypeStruct((e, LANES), F32)],
        scratch_shapes=[pltpu.VMEM((e, LANES), F32)],
        compiler_params=_cparams(("arbitrary",)),
        name="route",
    )(lg_t, b_col)


def _pos_kernel(pstart_ref, eidx_ref, rank_ref, pos_ref):
    e = eidx_ref[...]
    pos = rank_ref[...]
    for x in range(N_EXPERTS):
        pos = pos + jnp.where(e == x, pstart_ref[x], 0)
    pos_ref[...] = pos


def _positions(pstart, eidx, rank):
    k, t = eidx.shape
    tr = min(TR, t)
    gs = pltpu.PrefetchScalarGridSpec(
        num_scalar_prefetch=1,
        grid=(t // tr,),
        in_specs=[pl.BlockSpec((k, tr), lambda i, p: (0, i)), pl.BlockSpec((k, tr), lambda i, p: (0, i))],
        out_specs=pl.BlockSpec((k, tr), lambda i, p: (0, i)),
    )
    return pl.pallas_call(
        _pos_kernel, grid_spec=gs,
        out_shape=jax.ShapeDtypeStruct((k, t), I32),
        compiler_params=_cparams(("arbitrary",)),
        name="positions",
    )(pstart, eidx, rank)


def _row_copy(src, src_row, dst, dst_row, sem):
    return pltpu.make_async_copy(src.at[pl.ds(src_row * ROW_TILES, ROW_TILES)],
                                 dst.at[pl.ds(dst_row * ROW_TILES, ROW_TILES)], sem)


def _dispatch_kernel(pos_ref, h_ref, xs_ref, sem):
    td = pos_ref.shape[1]

    def issue(t, c):
        for k in range(TOP_K):
            _row_copy(h_ref, t, xs_ref, pos_ref[k, t], sem).start()
        return c

    lax.fori_loop(0, td, issue, 0)
    for k in range(TOP_K):
        pltpu.make_async_copy(h_ref, xs_ref.at[pl.ds(0, td * ROW_TILES)], sem).wait()


def _dispatch(pos, h2p, m_pad):
    k, t = pos.shape
    td = min(TD, t)
    return pl.pallas_call(
        _dispatch_kernel,
        grid=(t // td,),
        in_specs=[pl.BlockSpec((k, td), lambda i: (0, i), memory_space=pltpu.SMEM),
                  pl.BlockSpec((td * ROW_TILES, LANES), lambda i: (i, 0))],
        out_specs=pl.BlockSpec(memory_space=pl.ANY),
        out_shape=jax.ShapeDtypeStruct((m_pad * ROW_TILES, LANES), U32),
        scratch_shapes=[pltpu.SemaphoreType.DMA],
        compiler_params=_cparams(("arbitrary",)),
        name="dispatch",
    )(pos, h2p)


def _experts_kernel(be_ref, nu_ref, x_ref, wg_ref, wu_ref, wd_ref, y_ref, wg_s, wu_s, wd_s):
    i = pl.program_id(0)
    prev = be_ref[jnp.maximum(i - 1, 0)]
    fresh = jnp.logical_or(i == 0, be_ref[i] != prev)

    @pl.when(jnp.logical_and(fresh, i < nu_ref[0]))
    def _():
        wg_s[...] = wg_ref[0].astype(BF16)
        wu_s[...] = wu_ref[0].astype(BF16)
        wd_s[...] = wd_ref[0].astype(BF16)

    @pl.when(i < nu_ref[0])
    def _():
        rows = x_ref.shape[0] // ROW_TILES
        xa, xb = _unpack_words(_load_packed(x_ref, rows))
        xa = xa.astype(BF16)
        xb = xb.astype(BF16)
        g = _dot(xa, wg_s[0:HALF, :]) + _dot(xb, wg_s[HALF:, :])
        u = _dot(xa, wu_s[0:HALF, :]) + _dot(xb, wu_s[HALF:, :])
        a = (g * jax.nn.sigmoid(g) * u).astype(BF16)
        y = _dot(a, wd_s[...])
        _store_packed(y_ref, _pack_words(y), rows)


def _experts(blk_e, nused, xs, w_gate, w_up, w_down, blk):
    n_blk = blk_e.shape[0]
    ne, d, de = w_gate.shape
    xmap = lambda i, be, nu: (jnp.minimum(i, nu[0] - 1), 0)
    wmap = lambda i, be, nu: (be[i], 0, 0)
    gs = pltpu.PrefetchScalarGridSpec(
        num_scalar_prefetch=2,
        grid=(n_blk,),
        in_specs=[pl.BlockSpec((blk * ROW_TILES, LANES), xmap),
                  pl.BlockSpec((1, d, de), wmap), pl.BlockSpec((1, d, de), wmap),
                  pl.BlockSpec((1, de, d), wmap)],
        out_specs=pl.BlockSpec((blk * ROW_TILES, LANES), xmap),
        scratch_shapes=[pltpu.VMEM((d, de), BF16), pltpu.VMEM((d, de), BF16), pltpu.VMEM((de, d), BF16)],
    )
    return pl.pallas_call(
        _experts_kernel, grid_spec=gs,
        out_shape=jax.ShapeDtypeStruct(xs.shape, U32),
        compiler_params=_cparams(("arbitrary",)),
        name="experts",
    )(blk_e, nused, xs, w_gate, w_up, w_down)


def _combine_kernel(pos_ref, wt_ref, ys_ref, h_ref, x1_ref, wsg_ref, wsu_ref, wsd_ref,
                    gt_ref, gpost_ref, o_ref, gbuf, sem):
    td = pos_ref.shape[1]

    def issue(t, c):
        for k in range(TOP_K):
            _row_copy(ys_ref, pos_ref[k, t], gbuf.at[k], t, sem).start()
        return c

    lax.fori_loop(0, td, issue, 0)

    hb = h_ref[...]
    g = _dot(hb, wsg_ref[...])
    u = _dot(hb, wsu_ref[...])
    y = _dot((g * jax.nn.sigmoid(g) * u).astype(BF16), wsd_ref[...])

    wpad = jnp.concatenate([wt_ref[...], jnp.zeros((LANES - TOP_K, td), F32)], axis=0)
    wcol = wpad.T

    for k in range(TOP_K):
        pltpu.make_async_copy(ys_ref.at[pl.ds(0, td * ROW_TILES)], gbuf.at[k], sem).wait()
    lo = jnp.zeros((td, HALF), F32)
    hi = jnp.zeros((td, HALF), F32)
    for k in range(TOP_K):
        a, b = _unpack_words(_load_packed(gbuf.at[k], td))
        wk = wcol[:, k:k + 1]
        lo = lo + wk * a
        hi = hi + wk * b
    y = y + jnp.concatenate([lo, hi], axis=1)
    o_ref[...] = x1_ref[...] + gt_ref[...] * (_rms(y) * gpost_ref[...])


def _combine(pos, wts, ys, h2, x1, wsg, wsu, wsd, gt, gpost):
    k, t = pos.shape
    d = x1.shape[1]
    td = min(TD, t)
    assert td == LANES
    row = lambda n: pl.BlockSpec((td, n), lambda i: (i, 0))
    vec = _const_spec((1, d))
    return pl.pallas_call(
        _combine_kernel,
        grid=(t // td,),
        in_specs=[pl.BlockSpec((k, td), lambda i: (0, i), memory_space=pltpu.SMEM),
                  pl.BlockSpec((k, td), lambda i: (0, i)),
                  pl.BlockSpec(memory_space=pl.ANY),
                  row(d), row(d), _const_spec(wsg.shape), _const_spec(wsu.shape), _const_spec(wsd.shape),
                  vec, vec],
        out_specs=row(d),
        out_shape=jax.ShapeDtypeStruct((t, d), F32),
        scratch_shapes=[pltpu.VMEM((k, td * ROW_TILES, LANES), U32), pltpu.SemaphoreType.DMA],
        compiler_params=_cparams(("arbitrary",)),
        name="combine",
    )(pos, wts, ys, h2, x1, wsg, wsu, wsd, gt, gpost)


def _rope_tables(seq, dim):
    pos = jnp.arange(seq, dtype=F32)
    inv = ROPE_THETA ** (-jnp.arange(0, dim, 2, dtype=F32) / dim)
    ang = pos[:, None] * inv[None, :]
    return jnp.cos(ang), jnp.sin(ang)


def _lane_tables(seq, dim, period):
    cos, sin = _rope_tables(seq, dim)
    half = dim // 2
    r = jnp.arange(LANES) % period
    idx = r % half
    c = jnp.where(r < dim, cos[:, idx], 1.0)
    sa = jnp.where(r < half, -sin[:, idx], 0.0)
    sb = jnp.where((r >= half) & (r < dim), sin[:, idx], 0.0)
    return c, sa, sb


def _moe(h2, h2p, x1, lg_t, b_router, w_gate, w_up, w_down, wsg, wsu, wsd, gt_f, g_post_ffn):
    t = h2.shape[0]
    blk = EXP_BLK
    m_pad = t * TOP_K + N_EXPERTS * blk
    eidx, wts, rank, cnt = _route(lg_t, b_router.reshape(N_EXPERTS, 1))
    counts = cnt[:, 0].astype(I32)
    padded = (counts + blk - 1) // blk * blk
    pend = jnp.cumsum(padded)
    pstart = pend - padded
    n_blk = m_pad // blk
    blk_e = jnp.minimum(jnp.searchsorted(pend, jnp.arange(n_blk, dtype=I32) * blk, side="right"),
                        N_EXPERTS - 1).astype(I32)
    nused = (pend[-1:] // blk).astype(I32)
    pos = _positions(pstart.astype(I32), eidx, rank)
    xs = _dispatch(pos, h2p, m_pad)
    ys = _experts(blk_e, nused, xs, w_gate, w_up, w_down, blk)
    return _combine(pos, wts, ys, h2, x1, wsg, wsu, wsd, gt_f, g_post_ffn)


def _layer(x, c, w_ada, b_ada, g_pre_mix, w_in, g_q_lat, w_uq, g_kv_lat, w_ukv,
           lq1, lk1, lq2, lk2, g_diff_sub, w_out, g_post_mix, g_pre_ffn, w_router, b_router,
           w_gate, w_up, w_down, ws_gate, ws_up, ws_down, g_post_ffn):
    s, d = x.shape
    row = lambda a: a.reshape(1, -1)

    mod = _ada(c, w_ada, row(b_ada))
    sh_a, sc_a, gt_a, sh_f, sc_f, gt_f = [mod[:, i * d:(i + 1) * d] for i in range(6)]

    o1 = MLA_Q_RANK + MLA_KV_RANK
    o2 = o1 + MLA_ROPE
    w_in_r = jnp.concatenate(
        [w_in[:, :o1], w_in[:, o2:], w_in[:, o1:o2], jnp.zeros((d, LANES - MLA_ROPE), w_in.dtype)],
        axis=1).astype(BF16)
    lat, kpe, dq, dk, dv = _inproj(x, row(g_pre_mix), sc_a, sh_a, w_in_r,
                                   _lane_tables(s, DIFF_ROT, DIFF_HEAD_DIM))

    wq = w_uq.reshape(MLA_Q_RANK, MLA_HEADS, MLA_NOPE + MLA_ROPE)
    wq = jnp.pad(wq, ((0, 0), (0, 0), (0, MLA_QK_PAD - MLA_NOPE - MLA_ROPE)))
    wuq_r = wq.reshape(MLA_Q_RANK, MLA_HEADS * MLA_QK_PAD).astype(BF16)
    wkv = w_ukv.reshape(MLA_KV_RANK, MLA_HEADS, MLA_NOPE + MLA_V)
    wukv_r = jnp.concatenate(
        [wkv[:, :, :MLA_NOPE].reshape(MLA_KV_RANK, -1), wkv[:, :, MLA_NOPE:].reshape(MLA_KV_RANK, -1)],
        axis=1).astype(BF16)
    qc, kc, v = _mla_prep(lat, kpe, row(g_q_lat), row(g_kv_lat), wuq_r, wukv_r,
                          _lane_tables(s, MLA_ROPE, MLA_ROPE))

    o_mla = _mla_attn(qc, kc, v)
    o_diff = _diff_attn(dq, dk, dv, row(lq1), row(lk1), row(lq2), row(lk2), row(g_diff_sub))

    x1, h2, h2p, lg_t = _outproj(o_mla, o_diff, x, w_out.astype(BF16), gt_a, row(g_post_mix),
                                 row(g_pre_ffn), sc_f, sh_f, w_router.T)
    return _moe(h2, h2p, x1, lg_t, b_router, w_gate, w_up, w_down,
                ws_gate.astype(BF16), ws_up.astype(BF16), ws_down.astype(BF16), gt_f, row(g_post_ffn))


def kernel(x, c, w_ada, b_ada, g_pre_mix, w_in, g_q_lat, w_uq, g_kv_lat, w_ukv, lambda_q1, lambda_k1, lambda_q2, lambda_k2, g_diff_sub, w_out, g_post_mix, g_pre_ffn, w_router, b_router, w_gate, w_up, w_down, ws_gate, ws_up, ws_down, g_post_ffn):
    assert x.shape[0] == 1 and w_ada.shape[0] == 1
    out = _layer(x[0], c, w_ada[0], b_ada[0], g_pre_mix[0], w_in[0], g_q_lat[0], w_uq[0],
                 g_kv_lat[0], w_ukv[0], lambda_q1[0], lambda_k1[0], lambda_q2[0], lambda_k2[0],
                 g_diff_sub[0], w_out[0], g_post_mix[0], g_pre_ffn[0], w_router[0], b_router[0],
                 w_gate[0], w_up[0], w_down[0], ws_gate[0], ws_up[0], ws_down[0], g_post_ffn[0])
    return out[None]
```

```python
import functools
import math

import jax
import jax.numpy as jnp
import numpy as np
from jax import lax
from jax.experimental import pallas as pl
from jax.experimental.pallas import tpu as pltpu

F32 = jnp.float32
BF16 = jnp.bfloat16
U32 = jnp.uint32
I32 = jnp.int32

D_MODEL = 2048
CHUNK = 64
ROPE_THETA = 500000.0
EPS = 1e-6
LOG2E = 1.4426950408889634

MLA_HEADS = 8
MLA_Q_RANK = 768
MLA_KV_RANK = 512
MLA_NOPE = 128
MLA_ROPE = 64
MLA_V = 128
MLA_QK_PAD = 256

DIFF_HEADS = 8
DIFF_HEAD_DIM = 64
DIFF_ROT = DIFF_HEAD_DIM // 4
DIFF_W = DIFF_HEADS * 2 * DIFF_HEAD_DIM

N_EXPERTS = 64
TOP_K = 8
N_GROUPS = 8
GROUP_SIZE = N_EXPERTS // N_GROUPS
TOPK_GROUPS = 4
D_EXPERT = 512
ROUTED_SCALE = 2.5
LAMBDA_INIT = 0.8 - 0.6 * math.exp(-0.3 * 0)

LANES = 128
SUBLANES = 8
HALF = D_MODEL // 2
ROW_TILES = HALF // LANES

TM_PROJ = 256
ATT_BLK = 1024
CHAIN_W = 256
EXP_BLK = 256
TD = 128
TR = 512
TN_ADA = 1024

VMEM_LIMIT = 56 * 1024 * 1024

assert ROW_TILES == SUBLANES


def _cparams(sem, vmem=VMEM_LIMIT):
    return pltpu.CompilerParams(dimension_semantics=sem, vmem_limit_bytes=vmem)


def _dot(a, b):
    return jnp.dot(a, b, preferred_element_type=F32)


def _dot_nt(a, b):
    return lax.dot_general(a, b, (((1,), (1,)), ((), ())), preferred_element_type=F32)


def _rms(x):
    return x * lax.rsqrt(jnp.mean(x * x, axis=-1, keepdims=True) + EPS)


def _split_bf16(x):
    hi = x.astype(BF16)
    lo = (x - hi.astype(F32)).astype(BF16)
    return hi, lo


def _const_spec(shape):
    nd = len(shape)
    return pl.BlockSpec(shape, lambda *a: (0,) * nd, pipeline_mode=pl.Buffered(1))


def _ada_kernel(c_ref, w_ref, b_ref, o_ref):
    c = c_ref[...]
    a = c * jax.nn.sigmoid(c)
    a8 = jnp.broadcast_to(a, (SUBLANES, a.shape[1]))
    a_hi, a_lo = _split_bf16(a8)
    w_hi, w_lo = _split_bf16(w_ref[...])
    r = _dot(a_hi, w_hi) + _dot(a_lo, w_hi) + _dot(a_hi, w_lo)
    o_ref[...] = r[0:1] + b_ref[...]


def _ada(c, w, b):
    d, n = w.shape
    return pl.pallas_call(
        _ada_kernel,
        grid=(n // TN_ADA,),
        in_specs=[pl.BlockSpec((1, d), lambda j: (0, 0)),
                  pl.BlockSpec((d, TN_ADA), lambda j: (0, j)),
                  pl.BlockSpec((1, TN_ADA), lambda j: (0, j))],
        out_specs=pl.BlockSpec((1, TN_ADA), lambda j: (0, j)),
        out_shape=jax.ShapeDtypeStruct((1, n), F32),
        compiler_params=_cparams(("arbitrary",)),
        name="ada_mod",
    )(c, w, b)


def _tile_lanes(t, reps):
    return jnp.concatenate([t] * reps, axis=1)


def _rope_lanes(x, c, sa, sb, half):
    n = x.shape[1]
    return x * c + pltpu.roll(x, n - half, 1) * sa + pltpu.roll(x, half, 1) * sb


OFF_LAT = MLA_Q_RANK + MLA_KV_RANK
OFF_DQ = OFF_LAT
OFF_DK = OFF_DQ + DIFF_W
OFF_KPE = OFF_DK + DIFF_W
D_IN_PAD = OFF_KPE + LANES


def _inproj_kernel(x_ref, g_ref, sc_ref, sh_ref, w_ref, wvt_ref, c_ref, sa_ref, sb_ref,
                   lat_ref, kpe_ref, dq_ref, dk_ref, dvt_ref):
    x = x_ref[...]
    h = _rms(x) * g_ref[...] * (1.0 + sc_ref[...]) + sh_ref[...]
    hb = h.astype(BF16)
    reps = DIFF_W // LANES
    c = _tile_lanes(c_ref[...], reps)
    sa = _tile_lanes(sa_ref[...], reps)
    sb = _tile_lanes(sb_ref[...], reps)
    half = DIFF_ROT // 2
    lat_ref[...] = _dot(hb, w_ref[:, 0:OFF_LAT])
    q = _dot(hb, w_ref[:, OFF_DQ:OFF_DK])
    dq_ref[...] = (_rope_lanes(q, c, sa, sb, half) * (DIFF_HEAD_DIM ** -0.5 * LOG2E)).astype(BF16)
    k = _dot(hb, w_ref[:, OFF_DK:OFF_KPE])
    dk_ref[...] = _rope_lanes(k, c, sa, sb, half).astype(BF16)
    kpe_ref[...] = _dot(hb, w_ref[:, OFF_KPE:D_IN_PAD])
    dvt_ref[...] = _dot_nt(wvt_ref[...], hb).astype(BF16)


def _inproj(x, g, sc, sh, w_r, wv_t, tabs):
    s, d = x.shape
    tm = min(TM_PROJ, s)
    row = lambda n: pl.BlockSpec((tm, n), lambda i: (i, 0))
    return pl.pallas_call(
        _inproj_kernel,
        grid=(s // tm,),
        in_specs=[row(d), _const_spec((1, d)), _const_spec((1, d)), _const_spec((1, d)),
                  _const_spec(w_r.shape), _const_spec(wv_t.shape), row(LANES), row(LANES), row(LANES)],
        out_specs=[row(OFF_LAT), row(LANES), row(DIFF_W), row(DIFF_W),
                   pl.BlockSpec((DIFF_W, tm), lambda i: (0, i))],
        out_shape=[jax.ShapeDtypeStruct((s, OFF_LAT), F32),
                   jax.ShapeDtypeStruct((s, LANES), F32),
                   jax.ShapeDtypeStruct((s, DIFF_W), BF16),
                   jax.ShapeDtypeStruct((s, DIFF_W), BF16),
                   jax.ShapeDtypeStruct((DIFF_W, s), BF16)],
        compiler_params=_cparams(("arbitrary",)),
        name="in_proj",
    )(x, g, sc, sh, w_r, wv_t, *tabs)


def _mla_prep_kernel(lat_ref, kpe_ref, gq_ref, gkv_ref, wuq_ref, wuk_ref, wvt_ref,
                     c_ref, sa_ref, sb_ref, q_ref, k_ref, vt_ref):
    lat = lat_ref[...]
    qn = (_rms(lat[:, :MLA_Q_RANK]) * gq_ref[...]).astype(BF16)
    kvn = (_rms(lat[:, MLA_Q_RANK:]) * gkv_ref[...]).astype(BF16)
    q = _dot(qn, wuq_ref[...])
    kn = _dot(kvn, wuk_ref[...])
    c, sa, sb = c_ref[...], sa_ref[...], sb_ref[...]
    half = MLA_ROPE // 2
    qs = (MLA_NOPE + MLA_ROPE) ** -0.5 * LOG2E
    kpe = _rope_lanes(kpe_ref[...], c, sa, sb, half).astype(BF16)
    for h in range(MLA_HEADS):
        o = h * MLA_QK_PAD
        q_ref[:, o:o + LANES] = (q[:, o:o + LANES] * qs).astype(BF16)
        q_ref[:, o + LANES:o + 2 * LANES] = (
            _rope_lanes(q[:, o + LANES:o + 2 * LANES], c, sa, sb, half) * qs).astype(BF16)
        k_ref[:, o:o + LANES] = kn[:, h * MLA_NOPE:(h + 1) * MLA_NOPE].astype(BF16)
        k_ref[:, o + LANES:o + 2 * LANES] = kpe
    vt_ref[...] = _dot_nt(wvt_ref[...], kvn).astype(BF16)


def _mla_prep(lat, kpe, gq, gkv, wuq_r, wuk_r, wv_t, tabs):
    s = lat.shape[0]
    tm = min(TM_PROJ, s)
    row = lambda n: pl.BlockSpec((tm, n), lambda i: (i, 0))
    hq = MLA_HEADS * MLA_QK_PAD
    hv = MLA_HEADS * MLA_V
    return pl.pallas_call(
        _mla_prep_kernel,
        grid=(s // tm,),
        in_specs=[row(OFF_LAT), row(LANES), _const_spec(gq.shape), _const_spec(gkv.shape),
                  _const_spec(wuq_r.shape), _const_spec(wuk_r.shape), _const_spec(wv_t.shape),
                  row(LANES), row(LANES), row(LANES)],
        out_specs=[row(hq), row(hq), pl.BlockSpec((hv, tm), lambda i: (0, i))],
        out_shape=[jax.ShapeDtypeStruct((s, hq), BF16),
                   jax.ShapeDtypeStruct((s, hq), BF16),
                   jax.ShapeDtypeStruct((hv, s), BF16)],
        compiler_params=_cparams(("arbitrary",)),
        name="mla_prep",
    )(lat, kpe, gq, gkv, wuq_r, wuk_r, wv_t, *tabs)


def _pair_tables(nq):
    qi = np.concatenate([np.full(i + 1, i) for i in range(nq)]).astype(np.int32)
    kj = np.concatenate([np.arange(i + 1) for i in range(nq)]).astype(np.int32)
    return jnp.asarray(qi), jnp.asarray(kj)


def _chain_update(s, vt, m_ref, l_ref, acc_ref, cols):
    m_prev = m_ref[:, cols]
    m_new = jnp.maximum(m_prev, jnp.max(s, axis=0, keepdims=True))
    alpha = jnp.exp2(m_prev - m_new)
    p = jnp.exp2(s - m_new)
    l_ref[:, cols] = alpha * l_ref[:, cols] + jnp.sum(p, axis=0, keepdims=True)
    acc_ref[:, cols] = alpha * acc_ref[:, cols] + _dot(vt, p.astype(BF16))
    m_ref[:, cols] = m_new


def _attn_block(q_ref, k_ref, vt_ref, m_ref, l_ref, acc_ref, diag):
    bq = q_ref.shape[0]
    for c in range(bq // CHAIN_W):
        cols = pl.ds(c * CHAIN_W, CHAIN_W)
        q_c = q_ref[cols, :]
        if diag:
            nk = (c + 1) * CHAIN_W
            s = _dot_nt(k_ref[0:nk, :], q_c)
            kc = lax.broadcasted_iota(I32, (nk, CHAIN_W), 0) // CHUNK
            qc = (lax.broadcasted_iota(I32, (nk, CHAIN_W), 1) + c * CHAIN_W) // CHUNK
            s = jnp.where(kc <= qc, s, -jnp.inf)
            vt = vt_ref[:, 0:nk]
        else:
            s = _dot_nt(k_ref[...], q_c)
            vt = vt_ref[...]
        _chain_update(s, vt, m_ref, l_ref, acc_ref, cols)


def _init_stats(m, l, a):
    m[...] = jnp.full(m.shape, -jnp.inf, F32)
    l[...] = jnp.zeros(l.shape, F32)
    a[...] = jnp.zeros(a.shape, F32)


def _mla_attn_kernel(qi_ref, kj_ref, q_ref, k_ref, vt_ref, o_ref, m_scr, l_scr, acc_scr):
    t = pl.program_id(1)
    i = qi_ref[t]
    j = kj_ref[t]

    @pl.when(j == 0)
    def _():
        _init_stats(m_scr, l_scr, acc_scr)

    @pl.when(j < i)
    def _():
        _attn_block(q_ref, k_ref, vt_ref, m_scr, l_scr, acc_scr, False)

    @pl.when(j == i)
    def _():
        _attn_block(q_ref, k_ref, vt_ref, m_scr, l_scr, acc_scr, True)
        o_ref[...] = (acc_scr[...] / l_scr[...]).T.astype(o_ref.dtype)


def _mla_attn(qc, kc, vt):
    s = qc.shape[0]
    bq = min(ATT_BLK, s)
    nq = s // bq
    qi, kj = _pair_tables(nq)
    gs = pltpu.PrefetchScalarGridSpec(
        num_scalar_prefetch=2,
        grid=(MLA_HEADS, qi.shape[0]),
        in_specs=[pl.BlockSpec((bq, MLA_QK_PAD), lambda h, t, qi, kj: (qi[t], h)),
                  pl.BlockSpec((bq, MLA_QK_PAD), lambda h, t, qi, kj: (kj[t], h)),
                  pl.BlockSpec((MLA_V, bq), lambda h, t, qi, kj: (h, kj[t]))],
        out_specs=pl.BlockSpec((bq, MLA_V), lambda h, t, qi, kj: (qi[t], h)),
        scratch_shapes=[pltpu.VMEM((1, bq), F32), pltpu.VMEM((1, bq), F32),
                        pltpu.VMEM((MLA_V, bq), F32)],
    )
    return pl.pallas_call(
        _mla_attn_kernel,
        grid_spec=gs,
        out_shape=jax.ShapeDtypeStruct((s, MLA_HEADS * MLA_V), BF16),
        compiler_params=_cparams(("arbitrary", "arbitrary")),
        name="mla_attn",
    )(qi, kj, qc, kc, vt)


def _diff_attn_kernel(qi_ref, kj_ref, q_ref, k_ref, vt_ref, lq1_ref, lk1_ref, lq2_ref, lk2_ref,
                      g_ref, o_ref, q0_scr, q1_scr, m0, l0, a0, m1, l1, a1):
    t = pl.program_id(1)
    i = qi_ref[t]
    j = kj_ref[t]

    @pl.when(j == 0)
    def _():
        q = q_ref[...]
        lane = lax.broadcasted_iota(I32, q.shape, 1)
        zero = jnp.zeros_like(q)
        q0_scr[...] = jnp.where(lane < DIFF_HEAD_DIM, q, zero)
        q1_scr[...] = jnp.where(lane >= DIFF_HEAD_DIM, q, zero)
        _init_stats(m0, l0, a0)
        _init_stats(m1, l1, a1)

    @pl.when(j < i)
    def _():
        _attn_block(q0_scr, k_ref, vt_ref, m0, l0, a0, False)
        _attn_block(q1_scr, k_ref, vt_ref, m1, l1, a1, False)

    @pl.when(j == i)
    def _():
        _attn_block(q0_scr, k_ref, vt_ref, m0, l0, a0, True)
        _attn_block(q1_scr, k_ref, vt_ref, m1, l1, a1, True)
        lam = (jnp.exp(jnp.sum(lq1_ref[...] * lk1_ref[...], axis=1, keepdims=True))
               - jnp.exp(jnp.sum(lq2_ref[...] * lk2_ref[...], axis=1, keepdims=True))
               + LAMBDA_INIT)
        o = a0[...] / l0[...] - lam * (a1[...] / l1[...])
        o = o * lax.rsqrt(jnp.mean(o * o, axis=0, keepdims=True) + EPS)
        o_ref[...] = (o * g_ref[...] * (1.0 - LAMBDA_INIT)).T.astype(o_ref.dtype)


def _diff_attn(dq, dk, dvt, lq1, lk1, lq2, lk2, g_col):
    s = dq.shape[0]
    bq = min(ATT_BLK, s)
    nq = s // bq
    qi, kj = _pair_tables(nq)
    hw = 2 * DIFF_HEAD_DIM
    small = lambda a: pl.BlockSpec(a.shape, lambda h, t, qi, kj: (0, 0))
    stats = lambda: [pltpu.VMEM((1, bq), F32), pltpu.VMEM((1, bq), F32), pltpu.VMEM((hw, bq), F32)]
    gs = pltpu.PrefetchScalarGridSpec(
        num_scalar_prefetch=2,
        grid=(DIFF_HEADS, qi.shape[0]),
        in_specs=[pl.BlockSpec((bq, hw), lambda h, t, qi, kj: (qi[t], h)),
                  pl.BlockSpec((bq, hw), lambda h, t, qi, kj: (kj[t], h)),
                  pl.BlockSpec((hw, bq), lambda h, t, qi, kj: (h, kj[t])),
                  small(lq1), small(lk1), small(lq2), small(lk2), small(g_col)],
        out_specs=pl.BlockSpec((bq, hw), lambda h, t, qi, kj: (qi[t], h)),
        scratch_shapes=[pltpu.VMEM((bq, hw), BF16), pltpu.VMEM((bq, hw), BF16)] + stats() + stats(),
    )
    return pl.pallas_call(
        _diff_attn_kernel,
        grid_spec=gs,
        out_shape=jax.ShapeDtypeStruct((s, DIFF_W), BF16),
        compiler_params=_cparams(("arbitrary", "arbitrary")),
        name="diff_attn",
    )(qi, kj, dq, dk, dvt, lq1, lk1, lq2, lk2, g_col)


def _pack_words(y):
    lo = pltpu.bitcast(y[:, :HALF].astype(BF16).astype(F32), U32) >> 16
    hi = pltpu.bitcast(y[:, HALF:].astype(BF16).astype(F32), U32) & jnp.uint32(0xFFFF0000)
    return lo | hi


def _store_packed(ref, words, rows):
    for s in range(ROW_TILES):
        ref[pl.ds(s, rows, stride=ROW_TILES), :] = words[:, s * LANES:(s + 1) * LANES]


def _load_packed(ref, rows):
    return jnp.concatenate(
        [ref[pl.ds(s, rows, stride=ROW_TILES), :] for s in range(ROW_TILES)], axis=1)


def _unpack_words(w):
    lo = pltpu.bitcast(w << 16, F32)
    hi = pltpu.bitcast(w & jnp.uint32(0xFFFF0000), F32)
    return lo, hi


def _outproj_kernel(om_ref, od_ref, x_ref, w_ref, gt_ref, gpost_ref, gpre_ref, sc_ref, sh_ref,
                    wr_ref, x1_ref, h2_ref, h2p_ref, lg_ref):
    nm = om_ref.shape[1]
    y = _dot(om_ref[...], w_ref[0:nm, :]) + _dot(od_ref[...], w_ref[nm:, :])
    x1 = x_ref[...] + gt_ref[...] * (_rms(y) * gpost_ref[...])
    x1_ref[...] = x1
    h2 = _rms(x1) * gpre_ref[...] * (1.0 + sc_ref[...]) + sh_ref[...]
    h2_ref[...] = h2.astype(BF16)
    _store_packed(h2p_ref, _pack_words(h2), h2.shape[0])
    h_hi, h_lo = _split_bf16(h2)
    w_hi, w_lo = _split_bf16(wr_ref[...])
    lg_ref[...] = _dot_nt(w_hi, h_hi) + _dot_nt(w_hi, h_lo) + _dot_nt(w_lo, h_hi)


def _outproj(om, od, x, w_out, gt, gpost, gpre, sc, sh, wr_t):
    s, d = x.shape
    tm = min(TM_PROJ, s)
    row = lambda n: pl.BlockSpec((tm, n), lambda i: (i, 0))
    vec = _const_spec((1, d))
    return pl.pallas_call(
        _outproj_kernel,
        grid=(s // tm,),
        in_specs=[row(om.shape[1]), row(od.shape[1]), row(d), _const_spec(w_out.shape),
                  vec, vec, vec, vec, vec, _const_spec(wr_t.shape)],
        out_specs=[row(d), row(d),
                   pl.BlockSpec((tm * ROW_TILES, LANES), lambda i: (i, 0)),
                   pl.BlockSpec((N_EXPERTS, tm), lambda i: (0, i))],
        out_shape=[jax.ShapeDtypeStruct((s, d), F32),
                   jax.ShapeDtypeStruct((s, d), BF16),
                   jax.ShapeDtypeStruct((s * ROW_TILES, LANES), U32),
                   jax.ShapeDtypeStruct((N_EXPERTS, s), F32)],
        compiler_params=_cparams(("arbitrary",)),
        name="out_proj",
    )(om, od, x, w_out, gt, gpost, gpre, sc, sh, wr_t)


def _route_kernel(lg_ref, b_ref, eidx_ref, w_ref, rank_ref, cnt_ref, carry):
    step = pl.program_id(0)

    @pl.when(step == 0)
    def _():
        carry[...] = jnp.zeros(carry.shape, F32)

    tr = lg_ref.shape[1]
    ninf = jnp.float32(-jnp.inf)
    ie = lax.broadcasted_iota(I32, (GROUP_SIZE, tr), 0)
    rmax = lambda a: jnp.max(a, axis=0, keepdims=True)
    rmin = lambda a: jnp.min(a, axis=0, keepdims=True)
    rsum = lambda a: jnp.sum(a, axis=0, keepdims=True)

    sc, bi, gscore = [], [], []
    for g in range(N_GROUPS):
        lg = lg_ref[g * GROUP_SIZE:(g + 1) * GROUP_SIZE, :]
        s = jax.nn.sigmoid(lg)
        b = s + b_ref[g * GROUP_SIZE:(g + 1) * GROUP_SIZE, :]
        m1 = rmax(b)
        i1 = rmin(jnp.where(b == m1, ie, GROUP_SIZE))
        m2 = rmax(jnp.where(ie == i1, ninf, b))
        sc.append(s)
        bi.append(b)
        gscore.append(m1 + m2)

    gsel = [jnp.zeros((1, tr), I32) for _ in range(N_GROUPS)]
    for _ in range(TOPK_GROUPS):
        gm = functools.reduce(jnp.maximum, gscore)
        gi = functools.reduce(
            jnp.minimum, [jnp.where(gscore[g] == gm, g, N_GROUPS) for g in range(N_GROUPS)])
        for g in range(N_GROUPS):
            hit = gi == g
            gsel[g] = jnp.where(hit, 1, gsel[g])
            gscore[g] = jnp.where(hit, ninf, gscore[g])

    masked = [jnp.where(jnp.broadcast_to(gsel[g], (GROUP_SIZE, tr)) > 0, bi[g], ninf)
              for g in range(N_GROUPS)]
    sel = [jnp.zeros((GROUP_SIZE, tr), F32) for _ in range(N_GROUPS)]
    idxs, svals = [], []
    for _ in range(TOP_K):
        m = functools.reduce(jnp.maximum, [rmax(x) for x in masked])
        idx = functools.reduce(
            jnp.minimum,
            [rmin(jnp.where(masked[g] == m, ie + g * GROUP_SIZE, N_EXPERTS)) for g in range(N_GROUPS)])
        sv = jnp.zeros((1, tr), F32)
        for g in range(N_GROUPS):
            hit = (ie + g * GROUP_SIZE) == idx
            sv = sv + rsum(jnp.where(hit, sc[g], 0.0))
            masked[g] = jnp.where(hit, ninf, masked[g])
            sel[g] = jnp.where(hit, 1.0, sel[g])
        idxs.append(idx)
        svals.append(sv)
    tot = functools.reduce(lambda a, b: a + b, svals)

    sel2 = jnp.concatenate(sel, axis=0)
    before = (lax.broadcasted_iota(I32, (tr, tr), 0) < lax.broadcasted_iota(I32, (tr, tr), 1))
    rank2 = _dot(sel2.astype(BF16), before.astype(BF16)) + carry[:, 0:1]
    for k in range(TOP_K):
        rk = jnp.zeros((1, tr), F32)
        for g in range(N_GROUPS):
            hit = (ie + g * GROUP_SIZE) == idxs[k]
            rk = rk + rsum(jnp.where(hit, rank2[g * GROUP_SIZE:(g + 1) * GROUP_SIZE, :], 0.0))
        eidx_ref[k:k + 1, :] = idxs[k]
        w_ref[k:k + 1, :] = svals[k] / tot * ROUTED_SCALE
        rank_ref[k:k + 1, :] = rk.astype(I32)
    carry[...] = carry[...] + jnp.sum(sel2, axis=1, keepdims=True)
    cnt_ref[...] = carry[...]


def _route(lg_t, b_col):
    e, t = lg_t.shape
    tr = min(TR, t)
    tok = lambda: pl.BlockSpec((TOP_K, tr), lambda i: (0, i))
    return pl.pallas_call(
        _route_kernel,
        grid=(t // tr,),
        in_specs=[pl.BlockSpec((e, tr), lambda i: (0, i)), pl.BlockSpec((e, 1), lambda i: (0, 0))],
        out_specs=[tok(), tok(), tok(), pl.BlockSpec((e, LANES), lambda i: (0, 0))],
        out_shape=[jax.ShapeDtypeStruct((TOP_K, t), I32),
                   jax.ShapeDtypeStruct((TOP_K, t), F32),
                   jax.ShapeDtypeStruct((TOP_K, t), I32),
                   jax.ShapeDtypeStruct((e, LANES), F32)],
        scratch_shapes=[pltpu.VMEM((e, LANES), F32)],
        compiler_params=_cparams(("arbitrary",)),
        name="route",
    )(lg_t, b_col)


def _pos_kernel(pstart_ref, eidx_ref, rank_ref, pos_ref):
    e = eidx_ref[...]
    pos = rank_ref[...]
    for x in range(N_EXPERTS):
        pos = pos + jnp.where(e == x, pstart_ref[x], 0)
    pos_ref[...] = pos


def _positions(pstart, eidx, rank):
    k, t = eidx.shape
    tr = min(TR, t)
    gs = pltpu.PrefetchScalarGridSpec(
        num_scalar_prefetch=1,
        grid=(t // tr,),
        in_specs=[pl.BlockSpec((k, tr), lambda i, p: (0, i)), pl.BlockSpec((k, tr), lambda i, p: (0, i))],
        out_specs=pl.BlockSpec((k, tr), lambda i, p: (0, i)),
    )
    return pl.pallas_call(
        _pos_kernel, grid_spec=gs,
        out_shape=jax.ShapeDtypeStruct((k, t), I32),
        compiler_params=_cparams(("arbitrary",)),
        name="positions",
    )(pstart, eidx, rank)


def _row_copy(src, src_row, dst, dst_row, sem):
    return pltpu.make_async_copy(src.at[pl.ds(src_row * ROW_TILES, ROW_TILES)],
                                 dst.at[pl.ds(dst_row * ROW_TILES, ROW_TILES)], sem)


def _dispatch_kernel(pos_ref, h_ref, xs_ref, sem):
    td = pos_ref.shape[1]

    def issue(t, c):
        for k in range(TOP_K):
            _row_copy(h_ref, t, xs_ref, pos_ref[k, t], sem).start()
        return c

    lax.fori_loop(0, td, issue, 0)
    for k in range(TOP_K):
        pltpu.make_async_copy(h_ref, xs_ref.at[pl.ds(0, td * ROW_TILES)], sem).wait()


def _dispatch(pos, h2p, m_pad):
    k, t = pos.shape
    td = min(TD, t)
    return pl.pallas_call(
        _dispatch_kernel,
        grid=(t // td,),
        in_specs=[pl.BlockSpec((k, td), lambda i: (0, i), memory_space=pltpu.SMEM),
                  pl.BlockSpec((td * ROW_TILES, LANES), lambda i: (i, 0))],
        out_specs=pl.BlockSpec(memory_space=pl.ANY),
        out_shape=jax.ShapeDtypeStruct((m_pad * ROW_TILES, LANES), U32),
        scratch_shapes=[pltpu.SemaphoreType.DMA],
        compiler_params=_cparams(("arbitrary",)),
        name="dispatch",
    )(pos, h2p)


def _experts_kernel(be_ref, nu_ref, x_ref, wg_ref, wu_ref, wd_ref, y_ref, wg_s, wu_s, wd_s):
    i = pl.program_id(0)
    prev = be_ref[jnp.maximum(i - 1, 0)]
    fresh = jnp.logical_or(i == 0, be_ref[i] != prev)

    @pl.when(jnp.logical_and(fresh, i < nu_ref[0]))
    def _():
        wg_s[...] = wg_ref[0].astype(BF16)
        wu_s[...] = wu_ref[0].astype(BF16)
        wd_s[...] = wd_ref[0].astype(BF16)

    @pl.when(i < nu_ref[0])
    def _():
        rows = x_ref.shape[0] // ROW_TILES
        xa, xb = _unpack_words(_load_packed(x_ref, rows))
        xa = xa.astype(BF16)
        xb = xb.astype(BF16)
        g = _dot(xa, wg_s[0:HALF, :]) + _dot(xb, wg_s[HALF:, :])
        u = _dot(xa, wu_s[0:HALF, :]) + _dot(xb, wu_s[HALF:, :])
        a = (g * jax.nn.sigmoid(g) * u).astype(BF16)
        y = _dot(a, wd_s[...])
        _store_packed(y_ref, _pack_words(y), rows)


def _experts(blk_e, nused, xs, w_gate, w_up, w_down, blk):
    n_blk = blk_e.shape[0]
    ne, d, de = w_gate.shape
    xmap = lambda i, be, nu: (jnp.minimum(i, nu[0] - 1), 0)
    wmap = lambda i, be, nu: (be[i], 0, 0)
    gs = pltpu.PrefetchScalarGridSpec(
        num_scalar_prefetch=2,
        grid=(n_blk,),
        in_specs=[pl.BlockSpec((blk * ROW_TILES, LANES), xmap),
                  pl.BlockSpec((1, d, de), wmap), pl.BlockSpec((1, d, de), wmap),
                  pl.BlockSpec((1, de, d), wmap)],
        out_specs=pl.BlockSpec((blk * ROW_TILES, LANES), xmap),
        scratch_shapes=[pltpu.VMEM((d, de), BF16), pltpu.VMEM((d, de), BF16), pltpu.VMEM((de, d), BF16)],
    )
    return pl.pallas_call(
        _experts_kernel, grid_spec=gs,
        out_shape=jax.ShapeDtypeStruct(xs.shape, U32),
        compiler_params=_cparams(("arbitrary",)),
        name="experts",
    )(blk_e, nused, xs, w_gate, w_up, w_down)


def _combine_kernel(pos_ref, wt_ref, ys_ref, h_ref, x1_ref, wsg_ref, wsu_ref, wsd_ref,
                    gt_ref, gpost_ref, o_ref, gbuf, sem):
    td = pos_ref.shape[1]

    def issue(t, c):
        for k in range(TOP_K):
            _row_copy(ys_ref, pos_ref[k, t], gbuf.at[k], t, sem).start()
        return c

    lax.fori_loop(0, td, issue, 0)

    hb = h_ref[...]
    g = _dot(hb, wsg_ref[...])
    u = _dot(hb, wsu_ref[...])
    y = _dot((g * jax.nn.sigmoid(g) * u).astype(BF16), wsd_ref[...])

    wpad = jnp.concatenate([wt_ref[...], jnp.zeros((LANES - TOP_K, td), F32)], axis=0)
    wcol = wpad.T

    for k in range(TOP_K):
        pltpu.make_async_copy(ys_ref.at[pl.ds(0, td * ROW_TILES)], gbuf.at[k], sem).wait()
    lo = jnp.zeros((td, HALF), F32)
    hi = jnp.zeros((td, HALF), F32)
    for k in range(TOP_K):
        a, b = _unpack_words(_load_packed(gbuf.at[k], td))
        wk = wcol[:, k:k + 1]
        lo = lo + wk * a
        hi = hi + wk * b
    y = y + jnp.concatenate([lo, hi], axis=1)
    o_ref[...] = x1_ref[...] + gt_ref[...] * (_rms(y) * gpost_ref[...])


def _combine(pos, wts, ys, h2, x1, wsg, wsu, wsd, gt, gpost):
    k, t = pos.shape
    d = x1.shape[1]
    td = min(TD, t)
    assert td == LANES
    row = lambda n: pl.BlockSpec((td, n), lambda i: (i, 0))
    vec = _const_spec((1, d))
    return pl.pallas_call(
        _combine_kernel,
        grid=(t // td,),
        in_specs=[pl.BlockSpec((k, td), lambda i: (0, i), memory_space=pltpu.SMEM),
                  pl.BlockSpec((k, td), lambda i: (0, i)),
                  pl.BlockSpec(memory_space=pl.ANY),
                  row(d), row(d), _const_spec(wsg.shape), _const_spec(wsu.shape), _const_spec(wsd.shape),
                  vec, vec],
        out_specs=row(d),
        out_shape=jax.ShapeDtypeStruct((t, d), F32),
        scratch_shapes=[pltpu.VMEM((k, td * ROW_TILES, LANES), U32), pltpu.SemaphoreType.DMA],
        compiler_params=_cparams(("arbitrary",)),
        name="combine",
    )(pos, wts, ys, h2, x1, wsg, wsu, wsd, gt, gpost)


def _rope_tables(seq, dim):
    pos = jnp.arange(seq, dtype=F32)
    inv = ROPE_THETA ** (-jnp.arange(0, dim, 2, dtype=F32) / dim)
    ang = pos[:, None] * inv[None, :]
    return jnp.cos(ang), jnp.sin(ang)


def _lane_tables(seq, dim, period):
    cos, sin = _rope_tables(seq, dim)
    half = dim // 2
    r = jnp.arange(LANES) % period
    idx = r % half
    c = jnp.where(r < dim, cos[:, idx], 1.0)
    sa = jnp.where(r < half, -sin[:, idx], 0.0)
    sb = jnp.where((r >= half) & (r < dim), sin[:, idx], 0.0)
    return c, sa, sb


def _moe(h2, h2p, x1, lg_t, b_router, w_gate, w_up, w_down, wsg, wsu, wsd, gt_f, g_post_ffn):
    t = h2.shape[0]
    blk = EXP_BLK
    m_pad = t * TOP_K + N_EXPERTS * blk
    eidx, wts, rank, cnt = _route(lg_t, b_router.reshape(N_EXPERTS, 1))
    counts = cnt[:, 0].astype(I32)
    padded = (counts + blk - 1) // blk * blk
    pend = jnp.cumsum(padded)
    pstart = pend - padded
    n_blk = m_pad // blk
    blk_start = jnp.arange(n_blk, dtype=I32) * blk
    blk_e = jnp.minimum(jnp.sum(pend[None, :] <= blk_start[:, None], axis=1), N_EXPERTS - 1).astype(I32)
    nused = (pend[-1:] // blk).astype(I32)
    pos = _positions(pstart.astype(I32), eidx, rank)
    xs = _dispatch(pos, h2p, m_pad)
    ys = _experts(blk_e, nused, xs, w_gate, w_up, w_down, blk)
    return _combine(pos, wts, ys, h2, x1, wsg, wsu, wsd, gt_f, g_post_ffn)


def _layer(x, c, w_ada, b_ada, g_pre_mix, w_in, g_q_lat, w_uq, g_kv_lat, w_ukv,
           lq1, lk1, lq2, lk2, g_diff_sub, w_out, g_post_mix, g_pre_ffn, w_router, b_router,
           w_gate, w_up, w_down, ws_gate, ws_up, ws_down, g_post_ffn):
    s, d = x.shape
    row = lambda a: a.reshape(1, -1)

    mod = _ada(c, w_ada, row(b_ada))
    sh_a, sc_a, gt_a, sh_f, sc_f, gt_f = [mod[:, i * d:(i + 1) * d] for i in range(6)]

    o1 = MLA_Q_RANK + MLA_KV_RANK
    o2 = o1 + MLA_ROPE
    o3 = o2 + 2 * DIFF_W
    w_in_r = jnp.concatenate(
        [w_in[:, :o1], w_in[:, o2:o3], w_in[:, o1:o2], jnp.zeros((d, LANES - MLA_ROPE), w_in.dtype)],
        axis=1).astype(BF16)
    lat, kpe, dq, dk, dvt = _inproj(x, row(g_pre_mix), sc_a, sh_a, w_in_r, w_in[:, o3:].T.astype(BF16),
                                    _lane_tables(s, DIFF_ROT, DIFF_HEAD_DIM))

    wq = w_uq.reshape(MLA_Q_RANK, MLA_HEADS, MLA_NOPE + MLA_ROPE)
    wq = jnp.pad(wq, ((0, 0), (0, 0), (0, MLA_QK_PAD - MLA_NOPE - MLA_ROPE)))
    wuq_r = wq.reshape(MLA_Q_RANK, MLA_HEADS * MLA_QK_PAD).astype(BF16)
    wkv = w_ukv.reshape(MLA_KV_RANK, MLA_HEADS, MLA_NOPE + MLA_V)
    wuk_r = wkv[:, :, :MLA_NOPE].reshape(MLA_KV_RANK, -1).astype(BF16)
    wv_t = wkv[:, :, MLA_NOPE:].reshape(MLA_KV_RANK, -1).T.astype(BF16)
    qc, kc, vt = _mla_prep(lat, kpe, row(g_q_lat), row(g_kv_lat), wuq_r, wuk_r, wv_t,
                           _lane_tables(s, MLA_ROPE, MLA_ROPE))

    o_mla = _mla_attn(qc, kc, vt)
    o_diff = _diff_attn(dq, dk, dvt, row(lq1), row(lk1), row(lq2), row(lk2), g_diff_sub.reshape(-1, 1))

    x1, h2, h2p, lg_t = _outproj(o_mla, o_diff, x, w_out.astype(BF16), gt_a, row(g_post_mix),
                                 row(g_pre_ffn), sc_f, sh_f, w_router.T)
    return _moe(h2, h2p, x1, lg_t, b_router, w_gate, w_up, w_down,
                ws_gate.astype(BF16), ws_up.astype(BF16), ws_down.astype(BF16), gt_f, row(g_post_ffn))


def kernel(x, c, w_ada, b_ada, g_pre_mix, w_in, g_q_lat, w_uq, g_kv_lat, w_ukv, lambda_q1, lambda_k1, lambda_q2, lambda_k2, g_diff_sub, w_out, g_post_mix, g_pre_ffn, w_router, b_router, w_gate, w_up, w_down, ws_gate, ws_up, ws_down, g_post_ffn):
    assert x.shape[0] == 1 and w_ada.shape[0] == 1
    out = _layer(x[0], c, w_ada[0], b_ada[0], g_pre_mix[0], w_in[0], g_q_lat[0], w_uq[0],
                 g_kv_lat[0], w_ukv[0], lambda_q1[0], lambda_k1[0], lambda_q2[0], lambda_k2[0],
                 g_diff_sub[0], w_out[0], g_post_mix[0], g_pre_ffn[0], w_router[0], b_router[0],
                 w_gate[0], w_up[0], w_down[0], ws_gate[0], ws_up[0], ws_down[0], g_post_ffn[0])
    return out[None]
```

```python
import functools
import math

import jax
import jax.numpy as jnp
import numpy as np
from jax import lax
from jax.experimental import pallas as pl
from jax.experimental.pallas import tpu as pltpu

F32 = jnp.float32
BF16 = jnp.bfloat16
U32 = jnp.uint32
I32 = jnp.int32

D_MODEL = 2048
CHUNK = 64
ROPE_THETA = 500000.0
EPS = 1e-6
LOG2E = 1.4426950408889634

MLA_HEADS = 8
MLA_Q_RANK = 768
MLA_KV_RANK = 512
MLA_NOPE = 128
MLA_ROPE = 64
MLA_V = 128
MLA_QK_PAD = 256

DIFF_HEADS = 8
DIFF_HEAD_DIM = 64
DIFF_ROT = DIFF_HEAD_DIM // 4
DIFF_W = DIFF_HEADS * 2 * DIFF_HEAD_DIM

N_EXPERTS = 64
TOP_K = 8
N_GROUPS = 8
GROUP_SIZE = N_EXPERTS // N_GROUPS
TOPK_GROUPS = 4
D_EXPERT = 512
ROUTED_SCALE = 2.5
LAMBDA_INIT = 0.8 - 0.6 * math.exp(-0.3 * 0)

LANES = 128
SUBLANES = 8
HALF = D_MODEL // 2
ROW_TILES = HALF // LANES

TM_PROJ = 256
ATT_BLK = 1024
UNIT_W = 512
V_EXT = MLA_V + 16
EXP_BLK = 256
TD = 128
TR = 512
TN_ADA = 1024

VMEM_LIMIT = 56 * 1024 * 1024

assert ROW_TILES == SUBLANES


def _cparams(sem, vmem=VMEM_LIMIT, flags=None):
    return pltpu.CompilerParams(dimension_semantics=sem, vmem_limit_bytes=vmem, flags=flags)


ATTN_FLAGS = None


def _dot(a, b):
    return jnp.dot(a, b, preferred_element_type=F32)


def _dot_nt(a, b):
    return lax.dot_general(a, b, (((1,), (1,)), ((), ())), preferred_element_type=F32)


def _rms(x):
    return x * lax.rsqrt(jnp.mean(x * x, axis=-1, keepdims=True) + EPS)


def _split_bf16(x):
    hi = x.astype(BF16)
    lo = (x - hi.astype(F32)).astype(BF16)
    return hi, lo


def _const_spec(shape):
    nd = len(shape)
    return pl.BlockSpec(shape, lambda *a: (0,) * nd, pipeline_mode=pl.Buffered(1))


def _ada_kernel(c_ref, w_ref, b_ref, o_ref):
    c = c_ref[...]
    a = c * jax.nn.sigmoid(c)
    a8 = jnp.broadcast_to(a, (SUBLANES, a.shape[1]))
    a_hi, a_lo = _split_bf16(a8)
    w_hi, w_lo = _split_bf16(w_ref[...])
    r = _dot(a_hi, w_hi) + _dot(a_lo, w_hi) + _dot(a_hi, w_lo)
    o_ref[...] = r[0:1] + b_ref[...]


def _ada(c, w, b):
    d, n = w.shape
    return pl.pallas_call(
        _ada_kernel,
        grid=(n // TN_ADA,),
        in_specs=[pl.BlockSpec((1, d), lambda j: (0, 0)),
                  pl.BlockSpec((d, TN_ADA), lambda j: (0, j)),
                  pl.BlockSpec((1, TN_ADA), lambda j: (0, j))],
        out_specs=pl.BlockSpec((1, TN_ADA), lambda j: (0, j)),
        out_shape=jax.ShapeDtypeStruct((1, n), F32),
        compiler_params=_cparams(("arbitrary",)),
        name="ada_mod",
    )(c, w, b)


def _tile_lanes(t, reps):
    return jnp.concatenate([t] * reps, axis=1)


def _rope_lanes(x, c, sa, sb, half):
    n = x.shape[1]
    return x * c + pltpu.roll(x, n - half, 1) * sa + pltpu.roll(x, half, 1) * sb


OFF_LAT = MLA_Q_RANK + MLA_KV_RANK
OFF_DQ = OFF_LAT
OFF_DK = OFF_DQ + DIFF_W
OFF_KPE = OFF_DK + DIFF_W
D_IN_PAD = OFF_KPE + LANES


def _inproj_kernel(x_ref, g_ref, sc_ref, sh_ref, w_ref, wvt_ref, c_ref, sa_ref, sb_ref,
                   lat_ref, kpe_ref, dq_ref, dk_ref, dvt_ref):
    x = x_ref[...]
    h = _rms(x) * g_ref[...] * (1.0 + sc_ref[...]) + sh_ref[...]
    hb = h.astype(BF16)
    reps = DIFF_W // LANES
    c = _tile_lanes(c_ref[...], reps)
    sa = _tile_lanes(sa_ref[...], reps)
    sb = _tile_lanes(sb_ref[...], reps)
    half = DIFF_ROT // 2
    lat_ref[...] = _dot(hb, w_ref[:, 0:OFF_LAT])
    q = _dot(hb, w_ref[:, OFF_DQ:OFF_DK])
    dq_ref[...] = (_rope_lanes(q, c, sa, sb, half) * (DIFF_HEAD_DIM ** -0.5 * LOG2E)).astype(BF16)
    k = _dot(hb, w_ref[:, OFF_DK:OFF_KPE])
    dk_ref[...] = _rope_lanes(k, c, sa, sb, half).astype(BF16)
    kpe_ref[...] = _dot(hb, w_ref[:, OFF_KPE:D_IN_PAD])
    _store_vt_ext(dvt_ref, _dot_nt(wvt_ref[...], hb), DIFF_HEADS)


def _inproj(x, g, sc, sh, w_r, wv_t, tabs):
    s, d = x.shape
    tm = min(TM_PROJ, s)
    row = lambda n: pl.BlockSpec((tm, n), lambda i: (i, 0))
    return pl.pallas_call(
        _inproj_kernel,
        grid=(s // tm,),
        in_specs=[row(d), _const_spec((1, d)), _const_spec((1, d)), _const_spec((1, d)),
                  _const_spec(w_r.shape), _const_spec(wv_t.shape), row(LANES), row(LANES), row(LANES)],
        out_specs=[row(OFF_LAT), row(LANES), row(DIFF_W), row(DIFF_W),
                   pl.BlockSpec((DIFF_HEADS * V_EXT, tm), lambda i: (0, i))],
        out_shape=[jax.ShapeDtypeStruct((s, OFF_LAT), F32),
                   jax.ShapeDtypeStruct((s, LANES), F32),
                   jax.ShapeDtypeStruct((s, DIFF_W), BF16),
                   jax.ShapeDtypeStruct((s, DIFF_W), BF16),
                   jax.ShapeDtypeStruct((DIFF_HEADS * V_EXT, s), BF16)],
        compiler_params=_cparams(("arbitrary",)),
        name="in_proj",
    )(x, g, sc, sh, w_r, wv_t, *tabs)


def _mla_prep_kernel(lat_ref, kpe_ref, gq_ref, gkv_ref, wuq_ref, wuk_ref, wvt_ref,
                     c_ref, sa_ref, sb_ref, q_ref, k_ref, vt_ref):
    lat = lat_ref[...]
    qn = (_rms(lat[:, :MLA_Q_RANK]) * gq_ref[...]).astype(BF16)
    kvn = (_rms(lat[:, MLA_Q_RANK:]) * gkv_ref[...]).astype(BF16)
    q = _dot(qn, wuq_ref[...])
    kn = _dot(kvn, wuk_ref[...])
    c, sa, sb = c_ref[...], sa_ref[...], sb_ref[...]
    half = MLA_ROPE // 2
    qs = (MLA_NOPE + MLA_ROPE) ** -0.5 * LOG2E
    kpe = _rope_lanes(kpe_ref[...], c, sa, sb, half).astype(BF16)
    for h in range(MLA_HEADS):
        o = h * MLA_QK_PAD
        q_ref[:, o:o + LANES] = (q[:, o:o + LANES] * qs).astype(BF16)
        q_ref[:, o + LANES:o + 2 * LANES] = (
            _rope_lanes(q[:, o + LANES:o + 2 * LANES], c, sa, sb, half) * qs).astype(BF16)
        k_ref[:, o:o + LANES] = kn[:, h * MLA_NOPE:(h + 1) * MLA_NOPE].astype(BF16)
        k_ref[:, o + LANES:o + 2 * LANES] = kpe
    _store_vt_ext(vt_ref, _dot_nt(wvt_ref[...], kvn), MLA_HEADS)


def _mla_prep(lat, kpe, gq, gkv, wuq_r, wuk_r, wv_t, tabs):
    s = lat.shape[0]
    tm = min(TM_PROJ, s)
    row = lambda n: pl.BlockSpec((tm, n), lambda i: (i, 0))
    hq = MLA_HEADS * MLA_QK_PAD
    hv = MLA_HEADS * V_EXT
    return pl.pallas_call(
        _mla_prep_kernel,
        grid=(s // tm,),
        in_specs=[row(OFF_LAT), row(LANES), _const_spec(gq.shape), _const_spec(gkv.shape),
                  _const_spec(wuq_r.shape), _const_spec(wuk_r.shape), _const_spec(wv_t.shape),
                  row(LANES), row(LANES), row(LANES)],
        out_specs=[row(hq), row(hq), pl.BlockSpec((hv, tm), lambda i: (0, i))],
        out_shape=[jax.ShapeDtypeStruct((s, hq), BF16),
                   jax.ShapeDtypeStruct((s, hq), BF16),
                   jax.ShapeDtypeStruct((hv, s), BF16)],
        compiler_params=_cparams(("arbitrary",)),
        name="mla_prep",
    )(lat, kpe, gq, gkv, wuq_r, wuk_r, wv_t, *tabs)


def _pair_tables(nq):
    qi = np.concatenate([np.full(i + 1, i) for i in range(nq)]).astype(np.int32)
    kj = np.concatenate([np.arange(i + 1) for i in range(nq)]).astype(np.int32)
    return jnp.asarray(qi), jnp.asarray(kj)


def _ones_rows(n):
    rows = V_EXT - MLA_V
    return (lax.broadcasted_iota(I32, (rows, n), 0) == 0).astype(BF16)


def _store_vt_ext(vt_ref, vt, heads):
    ones = _ones_rows(vt.shape[1])
    for h in range(heads):
        vt_ref[h * V_EXT:h * V_EXT + MLA_V, :] = vt[h * MLA_V:(h + 1) * MLA_V, :].astype(BF16)
        vt_ref[h * V_EXT + MLA_V:(h + 1) * V_EXT, :] = ones


def _attn_step(q_refs, k_ref, vt_ref, m_refs, acc_refs, diag):
    w = UNIT_W
    nu = q_refs[0].shape[0] // w
    units = [(si, j) for si in range(len(q_refs)) for j in range(nu)]

    def scores(u):
        si, j = u
        lo, hi = j * w, (j + 1) * w
        if not diag:
            return _dot_nt(k_ref[...], q_refs[si][lo:hi, :])
        s = _dot_nt(k_ref[0:hi, :], q_refs[si][lo:hi, :])
        kc = lax.broadcasted_iota(I32, (hi, w), 0) // CHUNK
        qc = (lax.broadcasted_iota(I32, (hi, w), 1) + lo) // CHUNK
        return jnp.where(kc <= qc, s, -jnp.inf)

    s_next = scores(units[0])
    for n, (si, j) in enumerate(units):
        s = s_next
        if n + 1 < len(units):
            s_next = scores(units[n + 1])
        lo, hi = j * w, (j + 1) * w
        vt = vt_ref[:, 0:hi] if diag else vt_ref[...]
        m_prev = m_refs[si][:, lo:hi]
        m_new = jnp.maximum(m_prev, jnp.max(s, axis=0, keepdims=True))
        m_refs[si][:, lo:hi] = m_new
        acc_refs[si][:, lo:hi] = jnp.exp2(m_prev - m_new) * acc_refs[si][:, lo:hi]
        p = jnp.exp2(s - m_new).astype(BF16)
        acc_refs[si][:, lo:hi] += _dot(vt, p)


def _init_stats(m, a):
    m[...] = jnp.full(m.shape, -jnp.inf, F32)
    a[...] = jnp.zeros(a.shape, F32)


def _normalized(acc_ref):
    acc = acc_ref[...]
    return acc[0:MLA_V, :] / acc[MLA_V:MLA_V + 1, :]


def _mla_attn_kernel(qi_ref, kj_ref, q_ref, k_ref, vt_ref, o_ref, m_scr, acc_scr):
    t = pl.program_id(1)
    i = qi_ref[t]
    j = kj_ref[t]

    @pl.when(j == 0)
    def _():
        _init_stats(m_scr, acc_scr)

    @pl.when(j < i)
    def _():
        _attn_step([q_ref], k_ref, vt_ref, [m_scr], [acc_scr], False)

    @pl.when(j == i)
    def _():
        _attn_step([q_ref], k_ref, vt_ref, [m_scr], [acc_scr], True)
        o_ref[...] = _normalized(acc_scr).T.astype(o_ref.dtype)


def _mla_attn(qc, kc, vt):
    s = qc.shape[0]
    bq = min(ATT_BLK, s)
    nq = s // bq
    qi, kj = _pair_tables(nq)
    gs = pltpu.PrefetchScalarGridSpec(
        num_scalar_prefetch=2,
        grid=(MLA_HEADS, qi.shape[0]),
        in_specs=[pl.BlockSpec((bq, MLA_QK_PAD), lambda h, t, qi, kj: (qi[t], h)),
                  pl.BlockSpec((bq, MLA_QK_PAD), lambda h, t, qi, kj: (kj[t], h)),
                  pl.BlockSpec((V_EXT, bq), lambda h, t, qi, kj: (h, kj[t]))],
        out_specs=pl.BlockSpec((bq, MLA_V), lambda h, t, qi, kj: (qi[t], h)),
        scratch_shapes=[pltpu.VMEM((1, bq), F32), pltpu.VMEM((V_EXT, bq), F32)],
    )
    return pl.pallas_call(
        _mla_attn_kernel,
        grid_spec=gs,
        out_shape=jax.ShapeDtypeStruct((s, MLA_HEADS * MLA_V), BF16),
        compiler_params=_cparams(("arbitrary", "arbitrary"), flags=ATTN_FLAGS),
        name="mla_attn",
    )(qi, kj, qc, kc, vt)


def _diff_attn_kernel(qi_ref, kj_ref, q_ref, k_ref, vt_ref, lq1_ref, lk1_ref, lq2_ref, lk2_ref,
                      g_ref, o_ref, q0_scr, q1_scr, m0, a0, m1, a1):
    t = pl.program_id(1)
    i = qi_ref[t]
    j = kj_ref[t]

    @pl.when(j == 0)
    def _():
        q = q_ref[...]
        lane = lax.broadcasted_iota(I32, q.shape, 1)
        zero = jnp.zeros_like(q)
        q0_scr[...] = jnp.where(lane < DIFF_HEAD_DIM, q, zero)
        q1_scr[...] = jnp.where(lane >= DIFF_HEAD_DIM, q, zero)
        _init_stats(m0, a0)
        _init_stats(m1, a1)

    @pl.when(j < i)
    def _():
        _attn_step([q0_scr, q1_scr], k_ref, vt_ref, [m0, m1], [a0, a1], False)

    @pl.when(j == i)
    def _():
        _attn_step([q0_scr, q1_scr], k_ref, vt_ref, [m0, m1], [a0, a1], True)
        lam = (jnp.exp(jnp.sum(lq1_ref[...] * lk1_ref[...], axis=1, keepdims=True))
               - jnp.exp(jnp.sum(lq2_ref[...] * lk2_ref[...], axis=1, keepdims=True))
               + LAMBDA_INIT)
        o = _normalized(a0) - lam * _normalized(a1)
        o = o * lax.rsqrt(jnp.mean(o * o, axis=0, keepdims=True) + EPS)
        o_ref[...] = (o * g_ref[...] * (1.0 - LAMBDA_INIT)).T.astype(o_ref.dtype)


def _diff_attn(dq, dk, dvt, lq1, lk1, lq2, lk2, g_col):
    s = dq.shape[0]
    bq = min(ATT_BLK, s)
    nq = s // bq
    qi, kj = _pair_tables(nq)
    hw = 2 * DIFF_HEAD_DIM
    small = lambda a: pl.BlockSpec(a.shape, lambda h, t, qi, kj: (0, 0))
    stats = lambda: [pltpu.VMEM((1, bq), F32), pltpu.VMEM((V_EXT, bq), F32)]
    gs = pltpu.PrefetchScalarGridSpec(
        num_scalar_prefetch=2,
        grid=(DIFF_HEADS, qi.shape[0]),
        in_specs=[pl.BlockSpec((bq, hw), lambda h, t, qi, kj: (qi[t], h)),
                  pl.BlockSpec((bq, hw), lambda h, t, qi, kj: (kj[t], h)),
                  pl.BlockSpec((V_EXT, bq), lambda h, t, qi, kj: (h, kj[t])),
                  small(lq1), small(lk1), small(lq2), small(lk2), small(g_col)],
        out_specs=pl.BlockSpec((bq, hw), lambda h, t, qi, kj: (qi[t], h)),
        scratch_shapes=[pltpu.VMEM((bq, hw), BF16), pltpu.VMEM((bq, hw), BF16)] + stats() + stats(),
    )
    return pl.pallas_call(
        _diff_attn_kernel,
        grid_spec=gs,
        out_shape=jax.ShapeDtypeStruct((s, DIFF_W), BF16),
        compiler_params=_cparams(("arbitrary", "arbitrary"), flags=ATTN_FLAGS),
        name="diff_attn",
    )(qi, kj, dq, dk, dvt, lq1, lk1, lq2, lk2, g_col)


def _pack_words(y):
    lo = pltpu.bitcast(y[:, :HALF].astype(BF16).astype(F32), U32) >> 16
    hi = pltpu.bitcast(y[:, HALF:].astype(BF16).astype(F32), U32) & jnp.uint32(0xFFFF0000)
    return lo | hi


def _store_packed(ref, words, rows):
    for s in range(ROW_TILES):
        ref[pl.ds(s, rows, stride=ROW_TILES), :] = words[:, s * LANES:(s + 1) * LANES]


def _load_packed(ref, rows):
    return jnp.concatenate(
        [ref[pl.ds(s, rows, stride=ROW_TILES), :] for s in range(ROW_TILES)], axis=1)


def _unpack_words(w):
    lo = pltpu.bitcast(w << 16, F32)
    hi = pltpu.bitcast(w & jnp.uint32(0xFFFF0000), F32)
    return lo, hi


def _outproj_kernel(om_ref, od_ref, x_ref, w_ref, gt_ref, gpost_ref, gpre_ref, sc_ref, sh_ref,
                    wr_ref, x1_ref, h2_ref, h2p_ref, lg_ref):
    nm = om_ref.shape[1]
    y = _dot(om_ref[...], w_ref[0:nm, :]) + _dot(od_ref[...], w_ref[nm:, :])
    x1 = x_ref[...] + gt_ref[...] * (_rms(y) * gpost_ref[...])
    x1_ref[...] = x1
    h2 = _rms(x1) * gpre_ref[...] * (1.0 + sc_ref[...]) + sh_ref[...]
    h2_ref[...] = h2.astype(BF16)
    _store_packed(h2p_ref, _pack_words(h2), h2.shape[0])
    h_hi, h_lo = _split_bf16(h2)
    w_hi, w_lo = _split_bf16(wr_ref[...])
    lg_ref[...] = _dot_nt(w_hi, h_hi) + _dot_nt(w_hi, h_lo) + _dot_nt(w_lo, h_hi)


def _outproj(om, od, x, w_out, gt, gpost, gpre, sc, sh, wr_t):
    s, d = x.shape
    tm = min(TM_PROJ, s)
    row = lambda n: pl.BlockSpec((tm, n), lambda i: (i, 0))
    vec = _const_spec((1, d))
    return pl.pallas_call(
        _outproj_kernel,
        grid=(s // tm,),
        in_specs=[row(om.shape[1]), row(od.shape[1]), row(d), _const_spec(w_out.shape),
                  vec, vec, vec, vec, vec, _const_spec(wr_t.shape)],
        out_specs=[row(d), row(d),
                   pl.BlockSpec((tm * ROW_TILES, LANES), lambda i: (i, 0)),
                   pl.BlockSpec((N_EXPERTS, tm), lambda i: (0, i))],
        out_shape=[jax.ShapeDtypeStruct((s, d), F32),
                   jax.ShapeDtypeStruct((s, d), BF16),
                   jax.ShapeDtypeStruct((s * ROW_TILES, LANES), U32),
                   jax.ShapeDtypeStruct((N_EXPERTS, s), F32)],
        compiler_params=_cparams(("arbitrary",)),
        name="out_proj",
    )(om, od, x, w_out, gt, gpost, gpre, sc, sh, wr_t)


def _route_kernel(lg_ref, b_ref, eidx_ref, w_ref, rank_ref, cnt_ref, carry):
    step = pl.program_id(0)

    @pl.when(step == 0)
    def _():
        carry[...] = jnp.zeros(carry.shape, F32)

    tr = lg_ref.shape[1]
    ninf = jnp.float32(-jnp.inf)
    ie = lax.broadcasted_iota(I32, (GROUP_SIZE, tr), 0)
    rmax = lambda a: jnp.max(a, axis=0, keepdims=True)
    rmin = lambda a: jnp.min(a, axis=0, keepdims=True)
    rsum = lambda a: jnp.sum(a, axis=0, keepdims=True)

    sc, bi, gscore = [], [], []
    for g in range(N_GROUPS):
        lg = lg_ref[g * GROUP_SIZE:(g + 1) * GROUP_SIZE, :]
        s = jax.nn.sigmoid(lg)
        b = s + b_ref[g * GROUP_SIZE:(g + 1) * GROUP_SIZE, :]
        m1 = rmax(b)
        i1 = rmin(jnp.where(b == m1, ie, GROUP_SIZE))
        m2 = rmax(jnp.where(ie == i1, ninf, b))
        sc.append(s)
        bi.append(b)
        gscore.append(m1 + m2)

    gsel = [jnp.zeros((1, tr), I32) for _ in range(N_GROUPS)]
    for _ in range(TOPK_GROUPS):
        gm = functools.reduce(jnp.maximum, gscore)
        gi = functools.reduce(
            jnp.minimum, [jnp.where(gscore[g] == gm, g, N_GROUPS) for g in range(N_GROUPS)])
        for g in range(N_GROUPS):
            hit = gi == g
            gsel[g] = jnp.where(hit, 1, gsel[g])
            gscore[g] = jnp.where(hit, ninf, gscore[g])

    masked = [jnp.where(jnp.broadcast_to(gsel[g], (GROUP_SIZE, tr)) > 0, bi[g], ninf)
              for g in range(N_GROUPS)]
    sel = [jnp.zeros((GROUP_SIZE, tr), F32) for _ in range(N_GROUPS)]
    idxs, svals = [], []
    for _ in range(TOP_K):
        m = functools.reduce(jnp.maximum, [rmax(x) for x in masked])
        idx = functools.reduce(
            jnp.minimum,
            [rmin(jnp.where(masked[g] == m, ie + g * GROUP_SIZE, N_EXPERTS)) for g in range(N_GROUPS)])
        sv = jnp.zeros((1, tr), F32)
        for g in range(N_GROUPS):
            hit = (ie + g * GROUP_SIZE) == idx
            sv = sv + rsum(jnp.where(hit, sc[g], 0.0))
            masked[g] = jnp.where(hit, ninf, masked[g])
            sel[g] = jnp.where(hit, 1.0, sel[g])
        idxs.append(idx)
        svals.append(sv)
    tot = functools.reduce(lambda a, b: a + b, svals)

    sel2 = jnp.concatenate(sel, axis=0)
    before = (lax.broadcasted_iota(I32, (tr, tr), 0) < lax.broadcasted_iota(I32, (tr, tr), 1))
    rank2 = _dot(sel2.astype(BF16), before.astype(BF16)) + carry[:, 0:1]
    for k in range(TOP_K):
        rk = jnp.zeros((1, tr), F32)
        for g in range(N_GROUPS):
            hit = (ie + g * GROUP_SIZE) == idxs[k]
            rk = rk + rsum(jnp.where(hit, rank2[g * GROUP_SIZE:(g + 1) * GROUP_SIZE, :], 0.0))
        eidx_ref[k:k + 1, :] = idxs[k]
        w_ref[k:k + 1, :] = svals[k] / tot * ROUTED_SCALE
        rank_ref[k:k + 1, :] = rk.astype(I32)
    carry[...] = carry[...] + jnp.sum(sel2, axis=1, keepdims=True)
    cnt_ref[...] = carry[...]


def _route(lg_t, b_col):
    e, t = lg_t.shape
    tr = min(TR, t)
    tok = lambda: pl.BlockSpec((TOP_K, tr), lambda i: (0, i))
    return pl.pallas_call(
        _route_kernel,
        grid=(t // tr,),
        in_specs=[pl.BlockSpec((e, tr), lambda i: (0, i)), pl.BlockSpec((e, 1), lambda i: (0, 0))],
        out_specs=[tok(), tok(), tok(), pl.BlockSpec((e, LANES), lambda i: (0, 0))],
        out_shape=[jax.ShapeDtypeStruct((TOP_K, t), I32),
                   jax.ShapeDtypeStruct((TOP_K, t), F32),
                   jax.ShapeDtypeStruct((TOP_K, t), I32),
                   jax.ShapeDtypeStruct((e, LANES), F32)],
        scratch_shapes=[pltpu.VMEM((e, LANES), F32)],
        compiler_params=_cparams(("arbitrary",)),
        name="route",
    )(lg_t, b_col)


def _pos_kernel(pstart_ref, eidx_ref, rank_ref, pos_ref):
    e = eidx_ref[...]
    pos = rank_ref[...]
    for x in range(N_EXPERTS):
        pos = pos + jnp.where(e == x, pstart_ref[x], 0)
    pos_ref[...] = pos


def _positions(pstart, eidx, rank):
    k, t = eidx.shape
    tr = min(TR, t)
    gs = pltpu.PrefetchScalarGridSpec(
        num_scalar_prefetch=1,
        grid=(t // tr,),
        in_specs=[pl.BlockSpec((k, tr), lambda i, p: (0, i)), pl.BlockSpec((k, tr), lambda i, p: (0, i))],
        out_specs=pl.BlockSpec((k, tr), lambda i, p: (0, i)),
    )
    return pl.pallas_call(
        _pos_kernel, grid_spec=gs,
        out_shape=jax.ShapeDtypeStruct((k, t), I32),
        compiler_params=_cparams(("arbitrary",)),
        name="positions",
    )(pstart, eidx, rank)


def _row_copy(src, src_row, dst, dst_row, sem):
    return pltpu.make_async_copy(src.at[pl.ds(src_row * ROW_TILES, ROW_TILES)],
                                 dst.at[pl.ds(dst_row * ROW_TILES, ROW_TILES)], sem)


def _dispatch_kernel(pos_ref, h_ref, xs_ref, sem):
    td = pos_ref.shape[1]

    def issue(t, c):
        for k in range(TOP_K):
            _row_copy(h_ref, t, xs_ref, pos_ref[k, t], sem).start()
        return c

    lax.fori_loop(0, td, issue, 0)
    for k in range(TOP_K):
        pltpu.make_async_copy(h_ref, xs_ref.at[pl.ds(0, td * ROW_TILES)], sem).wait()


def _dispatch(pos, h2p, m_pad):
    k, t = pos.shape
    td = min(TD, t)
    return pl.pallas_call(
        _dispatch_kernel,
        grid=(t // td,),
        in_specs=[pl.BlockSpec((k, td), lambda i: (0, i), memory_space=pltpu.SMEM),
                  pl.BlockSpec((td * ROW_TILES, LANES), lambda i: (i, 0))],
        out_specs=pl.BlockSpec(memory_space=pl.ANY),
        out_shape=jax.ShapeDtypeStruct((m_pad * ROW_TILES, LANES), U32),
        scratch_shapes=[pltpu.SemaphoreType.DMA],
        compiler_params=_cparams(("arbitrary",)),
        name="dispatch",
    )(pos, h2p)


def _experts_kernel(be_ref, nu_ref, x_ref, wg_ref, wu_ref, wd_ref, y_ref, wg_s, wu_s, wd_s):
    i = pl.program_id(0)
    prev = be_ref[jnp.maximum(i - 1, 0)]
    fresh = jnp.logical_or(i == 0, be_ref[i] != prev)

    @pl.when(jnp.logical_and(fresh, i < nu_ref[0]))
    def _():
        wg_s[...] = wg_ref[0].astype(BF16)
        wu_s[...] = wu_ref[0].astype(BF16)
        wd_s[...] = wd_ref[0].astype(BF16)

    @pl.when(i < nu_ref[0])
    def _():
        rows = x_ref.shape[0] // ROW_TILES
        xa, xb = _unpack_words(_load_packed(x_ref, rows))
        xa = xa.astype(BF16)
        xb = xb.astype(BF16)
        g = _dot(xa, wg_s[0:HALF, :]) + _dot(xb, wg_s[HALF:, :])
        u = _dot(xa, wu_s[0:HALF, :]) + _dot(xb, wu_s[HALF:, :])
        a = (g * jax.nn.sigmoid(g) * u).astype(BF16)
        y = _dot(a, wd_s[...])
        _store_packed(y_ref, _pack_words(y), rows)


def _experts(blk_e, nused, xs, w_gate, w_up, w_down, blk):
    n_blk = blk_e.shape[0]
    ne, d, de = w_gate.shape
    xmap = lambda i, be, nu: (jnp.minimum(i, nu[0] - 1), 0)
    wmap = lambda i, be, nu: (be[i], 0, 0)
    gs = pltpu.PrefetchScalarGridSpec(
        num_scalar_prefetch=2,
        grid=(n_blk,),
        in_specs=[pl.BlockSpec((blk * ROW_TILES, LANES), xmap),
                  pl.BlockSpec((1, d, de), wmap), pl.BlockSpec((1, d, de), wmap),
                  pl.BlockSpec((1, de, d), wmap)],
        out_specs=pl.BlockSpec((blk * ROW_TILES, LANES), xmap),
        scratch_shapes=[pltpu.VMEM((d, de), BF16), pltpu.VMEM((d, de), BF16), pltpu.VMEM((de, d), BF16)],
    )
    return pl.pallas_call(
        _experts_kernel, grid_spec=gs,
        out_shape=jax.ShapeDtypeStruct(xs.shape, U32),
        compiler_params=_cparams(("arbitrary",)),
        name="experts",
    )(blk_e, nused, xs, w_gate, w_up, w_down)


def _combine_kernel(pos_ref, wt_ref, ys_ref, h_ref, x1_ref, wsg_ref, wsu_ref, wsd_ref,
                    gt_ref, gpost_ref, o_ref, gbuf, sem):
    td = pos_ref.shape[1]

    def issue(t, c):
        for k in range(TOP_K):
            _row_copy(ys_ref, pos_ref[k, t], gbuf.at[k], t, sem).start()
        return c

    lax.fori_loop(0, td, issue, 0)

    hb = h_ref[...]
    g = _dot(hb, wsg_ref[...])
    u = _dot(hb, wsu_ref[...])
    y = _dot((g * jax.nn.sigmoid(g) * u).astype(BF16), wsd_ref[...])

    wpad = jnp.concatenate([wt_ref[...], jnp.zeros((LANES - TOP_K, td), F32)], axis=0)
    wcol = wpad.T

    for k in range(TOP_K):
        pltpu.make_async_copy(ys_ref.at[pl.ds(0, td * ROW_TILES)], gbuf.at[k], sem).wait()
    lo = jnp.zeros((td, HALF), F32)
    hi = jnp.zeros((td, HALF), F32)
    for k in range(TOP_K):
        a, b = _unpack_words(_load_packed(gbuf.at[k], td))
        wk = wcol[:, k:k + 1]
        lo = lo + wk * a
        hi = hi + wk * b
    y = y + jnp.concatenate([lo, hi], axis=1)
    o_ref[...] = x1_ref[...] + gt_ref[...] * (_rms(y) * gpost_ref[...])


def _combine(pos, wts, ys, h2, x1, wsg, wsu, wsd, gt, gpost):
    k, t = pos.shape
    d = x1.shape[1]
    td = min(TD, t)
    assert td == LANES
    row = lambda n: pl.BlockSpec((td, n), lambda i: (i, 0))
    vec = _const_spec((1, d))
    return pl.pallas_call(
        _combine_kernel,
        grid=(t // td,),
        in_specs=[pl.BlockSpec((k, td), lambda i: (0, i), memory_space=pltpu.SMEM),
                  pl.BlockSpec((k, td), lambda i: (0, i)),
                  pl.BlockSpec(memory_space=pl.ANY),
                  row(d), row(d), _const_spec(wsg.shape), _const_spec(wsu.shape), _const_spec(wsd.shape),
                  vec, vec],
        out_specs=row(d),
        out_shape=jax.ShapeDtypeStruct((t, d), F32),
        scratch_shapes=[pltpu.VMEM((k, td * ROW_TILES, LANES), U32), pltpu.SemaphoreType.DMA],
        compiler_params=_cparams(("arbitrary",)),
        name="combine",
    )(pos, wts, ys, h2, x1, wsg, wsu, wsd, gt, gpost)


def _rope_tables(seq, dim):
    pos = jnp.arange(seq, dtype=F32)
    inv = ROPE_THETA ** (-jnp.arange(0, dim, 2, dtype=F32) / dim)
    ang = pos[:, None] * inv[None, :]
    return jnp.cos(ang), jnp.sin(ang)


def _lane_tables(seq, dim, period):
    cos, sin = _rope_tables(seq, dim)
    half = dim // 2
    r = jnp.arange(LANES) % period
    idx = r % half
    c = jnp.where(r < dim, cos[:, idx], 1.0)
    sa = jnp.where(r < half, -sin[:, idx], 0.0)
    sb = jnp.where((r >= half) & (r < dim), sin[:, idx], 0.0)
    return c, sa, sb


def _moe(h2, h2p, x1, lg_t, b_router, w_gate, w_up, w_down, wsg, wsu, wsd, gt_f, g_post_ffn):
    t = h2.shape[0]
    blk = EXP_BLK
    m_pad = t * TOP_K + N_EXPERTS * blk
    eidx, wts, rank, cnt = _route(lg_t, b_router.reshape(N_EXPERTS, 1))
    counts = cnt[:, 0].astype(I32)
    padded = (counts + blk - 1) // blk * blk
    pend = jnp.cumsum(padded)
    pstart = pend - padded
    n_blk = m_pad // blk
    blk_start = jnp.arange(n_blk, dtype=I32) * blk
    blk_e = jnp.minimum(jnp.sum(pend[None, :] <= blk_start[:, None], axis=1), N_EXPERTS - 1).astype(I32)
    nused = (pend[-1:] // blk).astype(I32)
    pos = _positions(pstart.astype(I32), eidx, rank)
    xs = _dispatch(pos, h2p, m_pad)
    ys = _experts(blk_e, nused, xs, w_gate, w_up, w_down, blk)
    return _combine(pos, wts, ys, h2, x1, wsg, wsu, wsd, gt_f, g_post_ffn)


def _layer(x, c, w_ada, b_ada, g_pre_mix, w_in, g_q_lat, w_uq, g_kv_lat, w_ukv,
           lq1, lk1, lq2, lk2, g_diff_sub, w_out, g_post_mix, g_pre_ffn, w_router, b_router,
           w_gate, w_up, w_down, ws_gate, ws_up, ws_down, g_post_ffn):
    s, d = x.shape
    row = lambda a: a.reshape(1, -1)

    mod = _ada(c, w_ada, row(b_ada))
    sh_a, sc_a, gt_a, sh_f, sc_f, gt_f = [mod[:, i * d:(i + 1) * d] for i in range(6)]

    o1 = MLA_Q_RANK + MLA_KV_RANK
    o2 = o1 + MLA_ROPE
    o3 = o2 + 2 * DIFF_W
    w_in_r = jnp.concatenate(
        [w_in[:, :o1], w_in[:, o2:o3], w_in[:, o1:o2], jnp.zeros((d, LANES - MLA_ROPE), w_in.dtype)],
        axis=1).astype(BF16)
    lat, kpe, dq, dk, dvt = _inproj(x, row(g_pre_mix), sc_a, sh_a, w_in_r, w_in[:, o3:].T.astype(BF16),
                                    _lane_tables(s, DIFF_ROT, DIFF_HEAD_DIM))

    wq = w_uq.reshape(MLA_Q_RANK, MLA_HEADS, MLA_NOPE + MLA_ROPE)
    wq = jnp.pad(wq, ((0, 0), (0, 0), (0, MLA_QK_PAD - MLA_NOPE - MLA_ROPE)))
    wuq_r = wq.reshape(MLA_Q_RANK, MLA_HEADS * MLA_QK_PAD).astype(BF16)
    wkv = w_ukv.reshape(MLA_KV_RANK, MLA_HEADS, MLA_NOPE + MLA_V)
    wuk_r = wkv[:, :, :MLA_NOPE].reshape(MLA_KV_RANK, -1).astype(BF16)
    wv_t = wkv[:, :, MLA_NOPE:].reshape(MLA_KV_RANK, -1).T.astype(BF16)
    qc, kc, vt = _mla_prep(lat, kpe, row(g_q_lat), row(g_kv_lat), wuq_r, wuk_r, wv_t,
                           _lane_tables(s, MLA_ROPE, MLA_ROPE))

    o_mla = _mla_attn(qc, kc, vt)
    o_diff = _diff_attn(dq, dk, dvt, row(lq1), row(lk1), row(lq2), row(lk2), g_diff_sub.reshape(-1, 1))

    x1, h2, h2p, lg_t = _outproj(o_mla, o_diff, x, w_out.astype(BF16), gt_a, row(g_post_mix),
                                 row(g_pre_ffn), sc_f, sh_f, w_router.T)
    return _moe(h2, h2p, x1, lg_t, b_router, w_gate, w_up, w_down,
                ws_gate.astype(BF16), ws_up.astype(BF16), ws_down.astype(BF16), gt_f, row(g_post_ffn))


def kernel(x, c, w_ada, b_ada, g_pre_mix, w_in, g_q_lat, w_uq, g_kv_lat, w_ukv, lambda_q1, lambda_k1, lambda_q2, lambda_k2, g_diff_sub, w_out, g_post_mix, g_pre_ffn, w_router, b_router, w_gate, w_up, w_down, ws_gate, ws_up, ws_down, g_post_ffn):
    assert x.shape[0] == 1 and w_ada.shape[0] == 1
    out = _layer(x[0], c, w_ada[0], b_ada[0], g_pre_mix[0], w_in[0], g_q_lat[0], w_uq[0],
                 g_kv_lat[0], w_ukv[0], lambda_q1[0], lambda_k1[0], lambda_q2[0], lambda_k2[0],
                 g_diff_sub[0], w_out[0], g_post_mix[0], g_pre_ffn[0], w_router[0], b_router[0],
                 w_gate[0], w_up[0], w_down[0], ws_gate[0], ws_up[0], ws_down[0], g_post_ffn[0])
    return out[None]
```

```python
import functools
import math

import jax
import jax.numpy as jnp
import numpy as np
from jax import lax
from jax.experimental import pallas as pl
from jax.experimental.pallas import tpu as pltpu

F32 = jnp.float32
BF16 = jnp.bfloat16
U32 = jnp.uint32
I32 = jnp.int32

D_MODEL = 2048
CHUNK = 64
ROPE_THETA = 500000.0
EPS = 1e-6
LOG2E = 1.4426950408889634

MLA_HEADS = 8
MLA_Q_RANK = 768
MLA_KV_RANK = 512
MLA_NOPE = 128
MLA_ROPE = 64
MLA_V = 128
MLA_QK_PAD = 256

DIFF_HEADS = 8
DIFF_HEAD_DIM = 64
DIFF_ROT = DIFF_HEAD_DIM // 4
DIFF_W = DIFF_HEADS * 2 * DIFF_HEAD_DIM

N_EXPERTS = 64
TOP_K = 8
N_GROUPS = 8
GROUP_SIZE = N_EXPERTS // N_GROUPS
TOPK_GROUPS = 4
D_EXPERT = 512
ROUTED_SCALE = 2.5
LAMBDA_INIT = 0.8 - 0.6 * math.exp(-0.3 * 0)

LANES = 128
SUBLANES = 8
HALF = D_MODEL // 2
ROW_TILES = HALF // LANES

TM_PROJ = 256
ATT_BQ = 2048
ATT_BK = 1024
ATT_KV_PER_Q = ATT_BQ // ATT_BK
UNIT_W = 512
V_EXT = MLA_V + 16
EXP_BLK = 256
TD = 128
TR = 512
TN_ADA = 1024

VMEM_LIMIT = 56 * 1024 * 1024

assert ROW_TILES == SUBLANES


def _cparams(sem, vmem=VMEM_LIMIT, flags=None):
    return pltpu.CompilerParams(dimension_semantics=sem, vmem_limit_bytes=vmem, flags=flags)


def _dot(a, b):
    return jnp.dot(a, b, preferred_element_type=F32)


def _dot_nt(a, b):
    return lax.dot_general(a, b, (((1,), (1,)), ((), ())), preferred_element_type=F32)


def _rms(x):
    return x * lax.rsqrt(jnp.mean(x * x, axis=-1, keepdims=True) + EPS)


def _split_bf16(x):
    hi = x.astype(BF16)
    lo = (x - hi.astype(F32)).astype(BF16)
    return hi, lo


def _const_spec(shape):
    nd = len(shape)
    return pl.BlockSpec(shape, lambda *a: (0,) * nd, pipeline_mode=pl.Buffered(1))


def _ada_kernel(c_ref, w_ref, b_ref, o_ref):
    c = c_ref[...]
    a = c * jax.nn.sigmoid(c)
    a8 = jnp.broadcast_to(a, (SUBLANES, a.shape[1]))
    a_hi, a_lo = _split_bf16(a8)
    w_hi, w_lo = _split_bf16(w_ref[...])
    r = _dot(a_hi, w_hi) + _dot(a_lo, w_hi) + _dot(a_hi, w_lo)
    o_ref[...] = r[0:1] + b_ref[...]


def _ada(c, w, b):
    d, n = w.shape
    return pl.pallas_call(
        _ada_kernel,
        grid=(n // TN_ADA,),
        in_specs=[pl.BlockSpec((1, d), lambda j: (0, 0)),
                  pl.BlockSpec((d, TN_ADA), lambda j: (0, j)),
                  pl.BlockSpec((1, TN_ADA), lambda j: (0, j))],
        out_specs=pl.BlockSpec((1, TN_ADA), lambda j: (0, j)),
        out_shape=jax.ShapeDtypeStruct((1, n), F32),
        compiler_params=_cparams(("arbitrary",)),
        name="ada_mod",
    )(c, w, b)


def _tile_lanes(t, reps):
    return jnp.concatenate([t] * reps, axis=1)


def _rope_lanes(x, c, sa, sb, half):
    n = x.shape[1]
    return x * c + pltpu.roll(x, n - half, 1) * sa + pltpu.roll(x, half, 1) * sb


OFF_LAT = MLA_Q_RANK + MLA_KV_RANK
OFF_DQ = OFF_LAT
OFF_DK = OFF_DQ + DIFF_W
OFF_KPE = OFF_DK + DIFF_W
D_IN_PAD = OFF_KPE + LANES


def _inproj_kernel(x_ref, g_ref, sc_ref, sh_ref, w_ref, wvt_ref, c_ref, sa_ref, sb_ref,
                   lat_ref, kpe_ref, dq_ref, dk_ref, dvt_ref):
    x = x_ref[...]
    h = _rms(x) * g_ref[...] * (1.0 + sc_ref[...]) + sh_ref[...]
    hb = h.astype(BF16)
    reps = DIFF_W // LANES
    c = _tile_lanes(c_ref[...], reps)
    sa = _tile_lanes(sa_ref[...], reps)
    sb = _tile_lanes(sb_ref[...], reps)
    half = DIFF_ROT // 2
    lat_ref[...] = _dot(hb, w_ref[:, 0:OFF_LAT])
    q = _dot(hb, w_ref[:, OFF_DQ:OFF_DK])
    dq_ref[...] = (_rope_lanes(q, c, sa, sb, half) * (DIFF_HEAD_DIM ** -0.5 * LOG2E)).astype(BF16)
    k = _dot(hb, w_ref[:, OFF_DK:OFF_KPE])
    dk_ref[...] = _rope_lanes(k, c, sa, sb, half).astype(BF16)
    kpe_ref[...] = _dot(hb, w_ref[:, OFF_KPE:D_IN_PAD])
    _store_vt_ext(dvt_ref, _dot_nt(wvt_ref[...], hb), DIFF_HEADS)


def _inproj(x, g, sc, sh, w_r, wv_t, tabs):
    s, d = x.shape
    tm = min(TM_PROJ, s)
    row = lambda n: pl.BlockSpec((tm, n), lambda i: (i, 0))
    return pl.pallas_call(
        _inproj_kernel,
        grid=(s // tm,),
        in_specs=[row(d), _const_spec((1, d)), _const_spec((1, d)), _const_spec((1, d)),
                  _const_spec(w_r.shape), _const_spec(wv_t.shape), row(LANES), row(LANES), row(LANES)],
        out_specs=[row(OFF_LAT), row(LANES), row(DIFF_W), row(DIFF_W),
                   pl.BlockSpec((DIFF_HEADS * V_EXT, tm), lambda i: (0, i))],
        out_shape=[jax.ShapeDtypeStruct((s, OFF_LAT), F32),
                   jax.ShapeDtypeStruct((s, LANES), F32),
                   jax.ShapeDtypeStruct((s, DIFF_W), BF16),
                   jax.ShapeDtypeStruct((s, DIFF_W), BF16),
                   jax.ShapeDtypeStruct((DIFF_HEADS * V_EXT, s), BF16)],
        compiler_params=_cparams(("arbitrary",)),
        name="in_proj",
    )(x, g, sc, sh, w_r, wv_t, *tabs)


def _mla_prep_kernel(lat_ref, kpe_ref, gq_ref, gkv_ref, wuq_ref, wuk_ref, wvt_ref,
                     c_ref, sa_ref, sb_ref, q_ref, k_ref, vt_ref):
    lat = lat_ref[...]
    qn = (_rms(lat[:, :MLA_Q_RANK]) * gq_ref[...]).astype(BF16)
    kvn = (_rms(lat[:, MLA_Q_RANK:]) * gkv_ref[...]).astype(BF16)
    q = _dot(qn, wuq_ref[...])
    kn = _dot(kvn, wuk_ref[...])
    c, sa, sb = c_ref[...], sa_ref[...], sb_ref[...]
    half = MLA_ROPE // 2
    qs = (MLA_NOPE + MLA_ROPE) ** -0.5 * LOG2E
    kpe = _rope_lanes(kpe_ref[...], c, sa, sb, half).astype(BF16)
    for h in range(MLA_HEADS):
        o = h * MLA_QK_PAD
        q_ref[:, o:o + LANES] = (q[:, o:o + LANES] * qs).astype(BF16)
        q_ref[:, o + LANES:o + 2 * LANES] = (
            _rope_lanes(q[:, o + LANES:o + 2 * LANES], c, sa, sb, half) * qs).astype(BF16)
        k_ref[:, o:o + LANES] = kn[:, h * MLA_NOPE:(h + 1) * MLA_NOPE].astype(BF16)
        k_ref[:, o + LANES:o + 2 * LANES] = kpe
    _store_vt_ext(vt_ref, _dot_nt(wvt_ref[...], kvn), MLA_HEADS)


def _mla_prep(lat, kpe, gq, gkv, wuq_r, wuk_r, wv_t, tabs):
    s = lat.shape[0]
    tm = min(TM_PROJ, s)
    row = lambda n: pl.BlockSpec((tm, n), lambda i: (i, 0))
    hq = MLA_HEADS * MLA_QK_PAD
    hv = MLA_HEADS * V_EXT
    return pl.pallas_call(
        _mla_prep_kernel,
        grid=(s // tm,),
        in_specs=[row(OFF_LAT), row(LANES), _const_spec(gq.shape), _const_spec(gkv.shape),
                  _const_spec(wuq_r.shape), _const_spec(wuk_r.shape), _const_spec(wv_t.shape),
                  row(LANES), row(LANES), row(LANES)],
        out_specs=[row(hq), row(hq), pl.BlockSpec((hv, tm), lambda i: (0, i))],
        out_shape=[jax.ShapeDtypeStruct((s, hq), BF16),
                   jax.ShapeDtypeStruct((s, hq), BF16),
                   jax.ShapeDtypeStruct((hv, s), BF16)],
        compiler_params=_cparams(("arbitrary",)),
        name="mla_prep",
    )(lat, kpe, gq, gkv, wuq_r, wuk_r, wv_t, *tabs)


def _pair_tables(nq):
    n = [ATT_KV_PER_Q * (i + 1) for i in range(nq)]
    qi = np.concatenate([np.full(c, i) for i, c in enumerate(n)]).astype(np.int32)
    kj = np.concatenate([np.arange(c) for c in n]).astype(np.int32)
    return jnp.asarray(qi), jnp.asarray(kj)


def _ones_rows(n):
    rows = V_EXT - MLA_V
    return (lax.broadcasted_iota(I32, (rows, n), 0) == 0).astype(BF16)


def _store_vt_ext(vt_ref, vt, heads):
    ones = _ones_rows(vt.shape[1])
    for h in range(heads):
        vt_ref[h * V_EXT:h * V_EXT + MLA_V, :] = vt[h * MLA_V:(h + 1) * MLA_V, :].astype(BF16)
        vt_ref[h * V_EXT + MLA_V:(h + 1) * V_EXT, :] = ones


def _attn_step(q_refs, k_ref, vt_ref, s_bufs, m_refs, acc_refs, rel):
    w = UNIT_W
    nk = k_ref.shape[0]
    per_kv = nk // w
    specs = []
    for u in range(q_refs[0].shape[0] // w):
        d = None if rel is None else u - rel * per_kv
        if d is None or d >= per_kv:
            specs.append((u, nk, None))
        elif d >= 0:
            specs.append((u, (d + 1) * w, d * w))
    units = [(si, sp) for sp in specs for si in range(len(q_refs))]

    def scores(n):
        si, (u, rows, off) = units[n]
        s = _dot_nt(k_ref[0:rows, :], q_refs[si][u * w:(u + 1) * w, :])
        if off is not None:
            kc = lax.broadcasted_iota(I32, (rows, w), 0) // CHUNK
            qc = (lax.broadcasted_iota(I32, (rows, w), 1) + off) // CHUNK
            s = jnp.where(kc <= qc, s, -jnp.inf)
        s_bufs[n % 2][0:rows, :] = s
        return jnp.max(s, axis=0, keepdims=True)

    cmax_next = scores(0)
    for n, (si, (u, rows, _)) in enumerate(units):
        cmax = cmax_next
        if n + 1 < len(units):
            cmax_next = scores(n + 1)
        m_ref, acc_ref = m_refs[si][u], acc_refs[si][u]
        m_prev = m_ref[...]
        m_new = jnp.maximum(m_prev, cmax)
        m_ref[...] = m_new
        acc_ref[...] = jnp.exp2(m_prev - m_new) * acc_ref[...]
        p = jnp.exp2(s_bufs[n % 2][0:rows, :] - m_new).astype(BF16)
        acc_ref[...] += _dot(vt_ref[:, 0:rows], p)


def _init_stats(ms, accs):
    for m, a in zip(ms, accs):
        m[...] = jnp.full(m.shape, -jnp.inf, F32)
        a[...] = jnp.zeros(a.shape, F32)


def _normalized(acc_ref):
    acc = acc_ref[...]
    return acc[0:MLA_V, :] / acc[MLA_V:MLA_V + 1, :]


def _attn_phases(qi_ref, kj_ref, init, step, finish):
    t = pl.program_id(1)
    rel = kj_ref[t] - ATT_KV_PER_Q * qi_ref[t]

    @pl.when(kj_ref[t] == 0)
    def _():
        init()

    @pl.when(rel < 0)
    def _():
        step(None)

    for r in range(ATT_KV_PER_Q):
        @pl.when(rel == r)
        def _(r=r):
            step(r)
            if r == ATT_KV_PER_Q - 1:
                finish()


def _unit_scratch(n_streams):
    nu = ATT_BQ // UNIT_W
    return ([pltpu.VMEM((ATT_BK, UNIT_W), F32)] * 2
            + [pltpu.VMEM((1, UNIT_W), F32)] * (nu * n_streams)
            + [pltpu.VMEM((V_EXT, UNIT_W), F32)] * (nu * n_streams))


def _split_units(scr, n_streams):
    nu = ATT_BQ // UNIT_W
    ms = [list(scr[s * nu:(s + 1) * nu]) for s in range(n_streams)]
    accs = [list(scr[(n_streams + s) * nu:(n_streams + s + 1) * nu]) for s in range(n_streams)]
    return ms, accs


def _mla_attn_kernel(qi_ref, kj_ref, q_ref, k_ref, vt_ref, o_ref, *scr):
    s_bufs, (ms, accs) = scr[:2], _split_units(scr[2:], 1)

    def finish():
        for u, a in enumerate(accs[0]):
            o_ref[u * UNIT_W:(u + 1) * UNIT_W, :] = _normalized(a).T.astype(o_ref.dtype)

    _attn_phases(qi_ref, kj_ref,
                 lambda: _init_stats(ms[0], accs[0]),
                 lambda rel: _attn_step([q_ref], k_ref, vt_ref, s_bufs, ms, accs, rel),
                 finish)


def _mla_attn(qc, kc, vt):
    s = qc.shape[0]
    assert s % ATT_BQ == 0
    qi, kj = _pair_tables(s // ATT_BQ)
    gs = pltpu.PrefetchScalarGridSpec(
        num_scalar_prefetch=2,
        grid=(MLA_HEADS, qi.shape[0]),
        in_specs=[pl.BlockSpec((ATT_BQ, MLA_QK_PAD), lambda h, t, qi, kj: (qi[t], h)),
                  pl.BlockSpec((ATT_BK, MLA_QK_PAD), lambda h, t, qi, kj: (kj[t], h)),
                  pl.BlockSpec((V_EXT, ATT_BK), lambda h, t, qi, kj: (h, kj[t]))],
        out_specs=pl.BlockSpec((ATT_BQ, MLA_V), lambda h, t, qi, kj: (qi[t], h)),
        scratch_shapes=_unit_scratch(1),
    )
    return pl.pallas_call(
        _mla_attn_kernel,
        grid_spec=gs,
        out_shape=jax.ShapeDtypeStruct((s, MLA_HEADS * MLA_V), BF16),
        compiler_params=_cparams(("arbitrary", "arbitrary")),
        name="mla_attn",
    )(qi, kj, qc, kc, vt)


def _diff_attn_kernel(qi_ref, kj_ref, q_ref, k_ref, vt_ref, lq1_ref, lk1_ref, lq2_ref, lk2_ref,
                      g_ref, o_ref, q0_scr, q1_scr, *scr):
    s_bufs, (ms, accs) = scr[:2], _split_units(scr[2:], 2)

    def init():
        q = q_ref[...]
        lane = lax.broadcasted_iota(I32, q.shape, 1)
        zero = jnp.zeros_like(q)
        q0_scr[...] = jnp.where(lane < DIFF_HEAD_DIM, q, zero)
        q1_scr[...] = jnp.where(lane >= DIFF_HEAD_DIM, q, zero)
        _init_stats(ms[0] + ms[1], accs[0] + accs[1])

    def finish():
        lam = (jnp.exp(jnp.sum(lq1_ref[...] * lk1_ref[...], axis=1, keepdims=True))
               - jnp.exp(jnp.sum(lq2_ref[...] * lk2_ref[...], axis=1, keepdims=True))
               + LAMBDA_INIT)
        for u, (a0, a1) in enumerate(zip(accs[0], accs[1])):
            o = _normalized(a0) - lam * _normalized(a1)
            o = o * lax.rsqrt(jnp.mean(o * o, axis=0, keepdims=True) + EPS)
            o_ref[u * UNIT_W:(u + 1) * UNIT_W, :] = (
                o * g_ref[...] * (1.0 - LAMBDA_INIT)).T.astype(o_ref.dtype)

    _attn_phases(qi_ref, kj_ref, init,
                 lambda rel: _attn_step([q0_scr, q1_scr], k_ref, vt_ref, s_bufs, ms, accs, rel),
                 finish)


def _diff_attn(dq, dk, dvt, lq1, lk1, lq2, lk2, g_col):
    s = dq.shape[0]
    assert s % ATT_BQ == 0
    qi, kj = _pair_tables(s // ATT_BQ)
    hw = 2 * DIFF_HEAD_DIM
    small = lambda a: pl.BlockSpec(a.shape, lambda h, t, qi, kj: (0, 0))
    gs = pltpu.PrefetchScalarGridSpec(
        num_scalar_prefetch=2,
        grid=(DIFF_HEADS, qi.shape[0]),
        in_specs=[pl.BlockSpec((ATT_BQ, hw), lambda h, t, qi, kj: (qi[t], h)),
                  pl.BlockSpec((ATT_BK, hw), lambda h, t, qi, kj: (kj[t], h)),
                  pl.BlockSpec((V_EXT, ATT_BK), lambda h, t, qi, kj: (h, kj[t])),
                  small(lq1), small(lk1), small(lq2), small(lk2), small(g_col)],
        out_specs=pl.BlockSpec((ATT_BQ, hw), lambda h, t, qi, kj: (qi[t], h)),
        scratch_shapes=[pltpu.VMEM((ATT_BQ, hw), BF16), pltpu.VMEM((ATT_BQ, hw), BF16)] + _unit_scratch(2),
    )
    return pl.pallas_call(
        _diff_attn_kernel,
        grid_spec=gs,
        out_shape=jax.ShapeDtypeStruct((s, DIFF_W), BF16),
        compiler_params=_cparams(("arbitrary", "arbitrary")),
        name="diff_attn",
    )(qi, kj, dq, dk, dvt, lq1, lk1, lq2, lk2, g_col)


def _pack_words(y):
    lo = pltpu.bitcast(y[:, :HALF].astype(BF16).astype(F32), U32) >> 16
    hi = pltpu.bitcast(y[:, HALF:].astype(BF16).astype(F32), U32) & jnp.uint32(0xFFFF0000)
    return lo | hi


def _store_packed(ref, words, rows):
    for s in range(ROW_TILES):
        ref[pl.ds(s, rows, stride=ROW_TILES), :] = words[:, s * LANES:(s + 1) * LANES]


def _load_packed(ref, rows):
    return jnp.concatenate(
        [ref[pl.ds(s, rows, stride=ROW_TILES), :] for s in range(ROW_TILES)], axis=1)


def _unpack_words(w):
    lo = pltpu.bitcast(w << 16, F32)
    hi = pltpu.bitcast(w & jnp.uint32(0xFFFF0000), F32)
    return lo, hi


def _outproj_kernel(om_ref, od_ref, x_ref, w_ref, gt_ref, gpost_ref, gpre_ref, sc_ref, sh_ref,
                    wr_ref, x1_ref, h2_ref, h2p_ref, lg_ref):
    nm = om_ref.shape[1]
    y = _dot(om_ref[...], w_ref[0:nm, :]) + _dot(od_ref[...], w_ref[nm:, :])
    x1 = x_ref[...] + gt_ref[...] * (_rms(y) * gpost_ref[...])
    x1_ref[...] = x1
    h2 = _rms(x1) * gpre_ref[...] * (1.0 + sc_ref[...]) + sh_ref[...]
    h2_ref[...] = h2.astype(BF16)
    _store_packed(h2p_ref, _pack_words(h2), h2.shape[0])
    h_hi, h_lo = _split_bf16(h2)
    w_hi, w_lo = _split_bf16(wr_ref[...])
    lg_ref[...] = _dot_nt(w_hi, h_hi) + _dot_nt(w_hi, h_lo) + _dot_nt(w_lo, h_hi)


def _outproj(om, od, x, w_out, gt, gpost, gpre, sc, sh, wr_t):
    s, d = x.shape
    tm = min(TM_PROJ, s)
    row = lambda n: pl.BlockSpec((tm, n), lambda i: (i, 0))
    vec = _const_spec((1, d))
    return pl.pallas_call(
        _outproj_kernel,
        grid=(s // tm,),
        in_specs=[row(om.shape[1]), row(od.shape[1]), row(d), _const_spec(w_out.shape),
                  vec, vec, vec, vec, vec, _const_spec(wr_t.shape)],
        out_specs=[row(d), row(d),
                   pl.BlockSpec((tm * ROW_TILES, LANES), lambda i: (i, 0)),
                   pl.BlockSpec((N_EXPERTS, tm), lambda i: (0, i))],
        out_shape=[jax.ShapeDtypeStruct((s, d), F32),
                   jax.ShapeDtypeStruct((s, d), BF16),
                   jax.ShapeDtypeStruct((s * ROW_TILES, LANES), U32),
                   jax.ShapeDtypeStruct((N_EXPERTS, s), F32)],
        compiler_params=_cparams(("arbitrary",)),
        name="out_proj",
    )(om, od, x, w_out, gt, gpost, gpre, sc, sh, wr_t)


def _route_kernel(lg_ref, b_ref, eidx_ref, w_ref, rank_ref, cnt_ref, carry):
    step = pl.program_id(0)

    @pl.when(step == 0)
    def _():
        carry[...] = jnp.zeros(carry.shape, F32)

    tr = lg_ref.shape[1]
    ninf = jnp.float32(-jnp.inf)
    ie = lax.broadcasted_iota(I32, (GROUP_SIZE, tr), 0)
    rmax = lambda a: jnp.max(a, axis=0, keepdims=True)
    rmin = lambda a: jnp.min(a, axis=0, keepdims=True)
    rsum = lambda a: jnp.sum(a, axis=0, keepdims=True)

    sc, bi, gscore = [], [], []
    for g in range(N_GROUPS):
        lg = lg_ref[g * GROUP_SIZE:(g + 1) * GROUP_SIZE, :]
        s = jax.nn.sigmoid(lg)
        b = s + b_ref[g * GROUP_SIZE:(g + 1) * GROUP_SIZE, :]
        m1 = rmax(b)
        i1 = rmin(jnp.where(b == m1, ie, GROUP_SIZE))
        m2 = rmax(jnp.where(ie == i1, ninf, b))
        sc.append(s)
        bi.append(b)
        gscore.append(m1 + m2)

    gsel = [jnp.zeros((1, tr), I32) for _ in range(N_GROUPS)]
    for _ in range(TOPK_GROUPS):
        gm = functools.reduce(jnp.maximum, gscore)
        gi = functools.reduce(
            jnp.minimum, [jnp.where(gscore[g] == gm, g, N_GROUPS) for g in range(N_GROUPS)])
        for g in range(N_GROUPS):
            hit = gi == g
            gsel[g] = jnp.where(hit, 1, gsel[g])
            gscore[g] = jnp.where(hit, ninf, gscore[g])

    masked = [jnp.where(jnp.broadcast_to(gsel[g], (GROUP_SIZE, tr)) > 0, bi[g], ninf)
              for g in range(N_GROUPS)]
    sel = [jnp.zeros((GROUP_SIZE, tr), F32) for _ in range(N_GROUPS)]
    idxs, svals = [], []
    for _ in range(TOP_K):
        m = functools.reduce(jnp.maximum, [rmax(x) for x in masked])
        idx = functools.reduce(
            jnp.minimum,
            [rmin(jnp.where(masked[g] == m, ie + g * GROUP_SIZE, N_EXPERTS)) for g in range(N_GROUPS)])
        sv = jnp.zeros((1, tr), F32)
        for g in range(N_GROUPS):
            hit = (ie + g * GROUP_SIZE) == idx
            sv = sv + rsum(jnp.where(hit, sc[g], 0.0))
            masked[g] = jnp.where(hit, ninf, masked[g])
            sel[g] = jnp.where(hit, 1.0, sel[g])
        idxs.append(idx)
        svals.append(sv)
    tot = functools.reduce(lambda a, b: a + b, svals)

    sel2 = jnp.concatenate(sel, axis=0)
    before = (lax.broadcasted_iota(I32, (tr, tr), 0) < lax.broadcasted_iota(I32, (tr, tr), 1))
    rank2 = _dot(sel2.astype(BF16), before.astype(BF16)) + carry[:, 0:1]
    for k in range(TOP_K):
        rk = jnp.zeros((1, tr), F32)
        for g in range(N_GROUPS):
            hit = (ie + g * GROUP_SIZE) == idxs[k]
            rk = rk + rsum(jnp.where(hit, rank2[g * GROUP_SIZE:(g + 1) * GROUP_SIZE, :], 0.0))
        eidx_ref[k:k + 1, :] = idxs[k]
        w_ref[k:k + 1, :] = svals[k] / tot * ROUTED_SCALE
        rank_ref[k:k + 1, :] = rk.astype(I32)
    carry[...] = carry[...] + jnp.sum(sel2, axis=1, keepdims=True)
    cnt_ref[...] = carry[...]


def _route(lg_t, b_col):
    e, t = lg_t.shape
    tr = min(TR, t)
    tok = lambda: pl.BlockSpec((TOP_K, tr), lambda i: (0, i))
    return pl.pallas_call(
        _route_kernel,
        grid=(t // tr,),
        in_specs=[pl.BlockSpec((e, tr), lambda i: (0, i)), pl.BlockSpec((e, 1), lambda i: (0, 0))],
        out_specs=[tok(), tok(), tok(), pl.BlockSpec((e, LANES), lambda i: (0, 0))],
        out_shape=[jax.ShapeDtypeStruct((TOP_K, t), I32),
                   jax.ShapeDtypeStruct((TOP_K, t), F32),
                   jax.ShapeDtypeStruct((TOP_K, t), I32),
                   jax.ShapeDtypeStruct((e, LANES), F32)],
        scratch_shapes=[pltpu.VMEM((e, LANES), F32)],
        compiler_params=_cparams(("arbitrary",)),
        name="route",
    )(lg_t, b_col)


def _pos_kernel(pstart_ref, eidx_ref, rank_ref, pos_ref):
    e = eidx_ref[...]
    pos = rank_ref[...]
    for x in range(N_EXPERTS):
        pos = pos + jnp.where(e == x, pstart_ref[x], 0)
    pos_ref[...] = pos


def _positions(pstart, eidx, rank):
    k, t = eidx.shape
    tr = min(TR, t)
    gs = pltpu.PrefetchScalarGridSpec(
        num_scalar_prefetch=1,
        grid=(t // tr,),
        in_specs=[pl.BlockSpec((k, tr), lambda i, p: (0, i)), pl.BlockSpec((k, tr), lambda i, p: (0, i))],
        out_specs=pl.BlockSpec((k, tr), lambda i, p: (0, i)),
    )
    return pl.pallas_call(
        _pos_kernel, grid_spec=gs,
        out_shape=jax.ShapeDtypeStruct((k, t), I32),
        compiler_params=_cparams(("arbitrary",)),
        name="positions",
    )(pstart, eidx, rank)


def _row_copy(src, src_row, dst, dst_row, sem):
    return pltpu.make_async_copy(src.at[pl.ds(src_row * ROW_TILES, ROW_TILES)],
                                 dst.at[pl.ds(dst_row * ROW_TILES, ROW_TILES)], sem)


def _dispatch_kernel(pos_ref, h_ref, xs_ref, sem):
    td = pos_ref.shape[1]

    def issue(t, c):
        for k in range(TOP_K):
            _row_copy(h_ref, t, xs_ref, pos_ref[k, t], sem).start(priority=k % 2)
        return c

    lax.fori_loop(0, td, issue, 0, unroll=4)
    for k in range(TOP_K):
        pltpu.make_async_copy(h_ref, xs_ref.at[pl.ds(0, td * ROW_TILES)], sem).wait()


def _dispatch(pos, h2p, m_pad):
    k, t = pos.shape
    td = min(TD, t)
    return pl.pallas_call(
        _dispatch_kernel,
        grid=(t // td,),
        in_specs=[pl.BlockSpec((k, td), lambda i: (0, i), memory_space=pltpu.SMEM),
                  pl.BlockSpec((td * ROW_TILES, LANES), lambda i: (i, 0))],
        out_specs=pl.BlockSpec(memory_space=pl.ANY),
        out_shape=jax.ShapeDtypeStruct((m_pad * ROW_TILES, LANES), U32),
        scratch_shapes=[pltpu.SemaphoreType.DMA],
        compiler_params=_cparams(("arbitrary",)),
        name="dispatch",
    )(pos, h2p)


def _experts_kernel(be_ref, nu_ref, x_ref, wg_ref, wu_ref, wd_ref, y_ref, wg_s, wu_s, wd_s):
    i = pl.program_id(0)
    prev = be_ref[jnp.maximum(i - 1, 0)]
    fresh = jnp.logical_or(i == 0, be_ref[i] != prev)

    @pl.when(jnp.logical_and(fresh, i < nu_ref[0]))
    def _():
        wg_s[...] = wg_ref[0].astype(BF16)
        wu_s[...] = wu_ref[0].astype(BF16)
        wd_s[...] = wd_ref[0].astype(BF16)

    @pl.when(i < nu_ref[0])
    def _():
        rows = x_ref.shape[0] // ROW_TILES
        xa, xb = _unpack_words(_load_packed(x_ref, rows))
        xa = xa.astype(BF16)
        xb = xb.astype(BF16)
        g = _dot(xa, wg_s[0:HALF, :]) + _dot(xb, wg_s[HALF:, :])
        u = _dot(xa, wu_s[0:HALF, :]) + _dot(xb, wu_s[HALF:, :])
        a = (g * jax.nn.sigmoid(g) * u).astype(BF16)
        y = _dot(a, wd_s[...])
        _store_packed(y_ref, _pack_words(y), rows)


def _experts(blk_e, nused, xs, w_gate, w_up, w_down, blk):
    n_blk = blk_e.shape[0]
    ne, d, de = w_gate.shape
    xmap = lambda i, be, nu: (jnp.minimum(i, nu[0] - 1), 0)
    wmap = lambda i, be, nu: (be[i], 0, 0)
    gs = pltpu.PrefetchScalarGridSpec(
        num_scalar_prefetch=2,
        grid=(n_blk,),
        in_specs=[pl.BlockSpec((blk * ROW_TILES, LANES), xmap),
                  pl.BlockSpec((1, d, de), wmap), pl.BlockSpec((1, d, de), wmap),
                  pl.BlockSpec((1, de, d), wmap)],
        out_specs=pl.BlockSpec((blk * ROW_TILES, LANES), xmap),
        scratch_shapes=[pltpu.VMEM((d, de), BF16), pltpu.VMEM((d, de), BF16), pltpu.VMEM((de, d), BF16)],
    )
    return pl.pallas_call(
        _experts_kernel, grid_spec=gs,
        out_shape=jax.ShapeDtypeStruct(xs.shape, U32),
        compiler_params=_cparams(("arbitrary",)),
        name="experts",
    )(blk_e, nused, xs, w_gate, w_up, w_down)


def _combine_kernel(pos_ref, wt_ref, ys_ref, h_ref, x1_ref, wsg_ref, wsu_ref, wsd_ref,
                    gt_ref, gpost_ref, o_ref, gbuf, sem):
    td = pos_ref.shape[1]

    def issue(t, c):
        for k in range(TOP_K):
            _row_copy(ys_ref, pos_ref[k, t], gbuf.at[k], t, sem).start(priority=k % 2)
        return c

    lax.fori_loop(0, td, issue, 0, unroll=4)

    hb = h_ref[...]
    g = _dot(hb, wsg_ref[...])
    u = _dot(hb, wsu_ref[...])
    y = _dot((g * jax.nn.sigmoid(g) * u).astype(BF16), wsd_ref[...])

    wpad = jnp.concatenate([wt_ref[...], jnp.zeros((LANES - TOP_K, td), F32)], axis=0)
    wcol = wpad.T

    for k in range(TOP_K):
        pltpu.make_async_copy(ys_ref.at[pl.ds(0, td * ROW_TILES)], gbuf.at[k], sem).wait()
    lo = jnp.zeros((td, HALF), F32)
    hi = jnp.zeros((td, HALF), F32)
    for k in range(TOP_K):
        a, b = _unpack_words(_load_packed(gbuf.at[k], td))
        wk = wcol[:, k:k + 1]
        lo = lo + wk * a
        hi = hi + wk * b
    y = y + jnp.concatenate([lo, hi], axis=1)
    o_ref[...] = x1_ref[...] + gt_ref[...] * (_rms(y) * gpost_ref[...])


def _combine(pos, wts, ys, h2, x1, wsg, wsu, wsd, gt, gpost):
    k, t = pos.shape
    d = x1.shape[1]
    td = min(TD, t)
    assert td == LANES
    row = lambda n: pl.BlockSpec((td, n), lambda i: (i, 0))
    vec = _const_spec((1, d))
    return pl.pallas_call(
        _combine_kernel,
        grid=(t // td,),
        in_specs=[pl.BlockSpec((k, td), lambda i: (0, i), memory_space=pltpu.SMEM),
                  pl.BlockSpec((k, td), lambda i: (0, i)),
                  pl.BlockSpec(memory_space=pl.ANY),
                  row(d), row(d), _const_spec(wsg.shape), _const_spec(wsu.shape), _const_spec(wsd.shape),
                  vec, vec],
        out_specs=row(d),
        out_shape=jax.ShapeDtypeStruct((t, d), F32),
        scratch_shapes=[pltpu.VMEM((k, td * ROW_TILES, LANES), U32), pltpu.SemaphoreType.DMA],
        compiler_params=_cparams(("arbitrary",)),
        name="combine",
    )(pos, wts, ys, h2, x1, wsg, wsu, wsd, gt, gpost)


def _rope_tables(seq, dim):
    pos = jnp.arange(seq, dtype=F32)
    inv = ROPE_THETA ** (-jnp.arange(0, dim, 2, dtype=F32) / dim)
    ang = pos[:, None] * inv[None, :]
    return jnp.cos(ang), jnp.sin(ang)


def _lane_tables(seq, dim, period):
    cos, sin = _rope_tables(seq, dim)
    half = dim // 2
    r = jnp.arange(LANES) % period
    idx = r % half
    c = jnp.where(r < dim, cos[:, idx], 1.0)
    sa = jnp.where(r < half, -sin[:, idx], 0.0)
    sb = jnp.where((r >= half) & (r < dim), sin[:, idx], 0.0)
    return c, sa, sb


def _moe(h2, h2p, x1, lg_t, b_router, w_gate, w_up, w_down, wsg, wsu, wsd, gt_f, g_post_ffn):
    t = h2.shape[0]
    blk = EXP_BLK
    m_pad = t * TOP_K + N_EXPERTS * blk
    eidx, wts, rank, cnt = _route(lg_t, b_router.reshape(N_EXPERTS, 1))
    counts = cnt[:, 0].astype(I32)
    padded = (counts + blk - 1) // blk * blk
    pend = jnp.cumsum(padded)
    pstart = pend - padded
    n_blk = m_pad // blk
    blk_start = jnp.arange(n_blk, dtype=I32) * blk
    blk_e = jnp.minimum(jnp.sum(pend[None, :] <= blk_start[:, None], axis=1), N_EXPERTS - 1).astype(I32)
    nused = (pend[-1:] // blk).astype(I32)
    pos = _positions(pstart.astype(I32), eidx, rank)
    xs = _dispatch(pos, h2p, m_pad)
    ys = _experts(blk_e, nused, xs, w_gate, w_up, w_down, blk)
    return _combine(pos, wts, ys, h2, x1, wsg, wsu, wsd, gt_f, g_post_ffn)


def _layer(x, c, w_ada, b_ada, g_pre_mix, w_in, g_q_lat, w_uq, g_kv_lat, w_ukv,
           lq1, lk1, lq2, lk2, g_diff_sub, w_out, g_post_mix, g_pre_ffn, w_router, b_router,
           w_gate, w_up, w_down, ws_gate, ws_up, ws_down, g_post_ffn):
    s, d = x.shape
    row = lambda a: a.reshape(1, -1)

    mod = _ada(c, w_ada, row(b_ada))
    sh_a, sc_a, gt_a, sh_f, sc_f, gt_f = [mod[:, i * d:(i + 1) * d] for i in range(6)]

    o1 = MLA_Q_RANK + MLA_KV_RANK
    o2 = o1 + MLA_ROPE
    o3 = o2 + 2 * DIFF_W
    w_in_r = jnp.concatenate(
        [w_in[:, :o1], w_in[:, o2:o3], w_in[:, o1:o2], jnp.zeros((d, LANES - MLA_ROPE), w_in.dtype)],
        axis=1).astype(BF16)
    lat, kpe, dq, dk, dvt = _inproj(x, row(g_pre_mix), sc_a, sh_a, w_in_r, w_in[:, o3:].T.astype(BF16),
                                    _lane_tables(s, DIFF_ROT, DIFF_HEAD_DIM))

    wq = w_uq.reshape(MLA_Q_RANK, MLA_HEADS, MLA_NOPE + MLA_ROPE)
    wq = jnp.pad(wq, ((0, 0), (0, 0), (0, MLA_QK_PAD - MLA_NOPE - MLA_ROPE)))
    wuq_r = wq.reshape(MLA_Q_RANK, MLA_HEADS * MLA_QK_PAD).astype(BF16)
    wkv = w_ukv.reshape(MLA_KV_RANK, MLA_HEADS, MLA_NOPE + MLA_V)
    wuk_r = wkv[:, :, :MLA_NOPE].reshape(MLA_KV_RANK, -1).astype(BF16)
    wv_t = wkv[:, :, MLA_NOPE:].reshape(MLA_KV_RANK, -1).T.astype(BF16)
    qc, kc, vt = _mla_prep(lat, kpe, row(g_q_lat), row(g_kv_lat), wuq_r, wuk_r, wv_t,
                           _lane_tables(s, MLA_ROPE, MLA_ROPE))

    o_mla = _mla_attn(qc, kc, vt)
    o_diff = _diff_attn(dq, dk, dvt, row(lq1), row(lk1), row(lq2), row(lk2), g_diff_sub.reshape(-1, 1))

    x1, h2, h2p, lg_t = _outproj(o_mla, o_diff, x, w_out.astype(BF16), gt_a, row(g_post_mix),
                                 row(g_pre_ffn), sc_f, sh_f, w_router.T)
    return _moe(h2, h2p, x1, lg_t, b_router, w_gate, w_up, w_down,
                ws_gate.astype(BF16), ws_up.astype(BF16), ws_down.astype(BF16), gt_f, row(g_post_ffn))


def kernel(x, c, w_ada, b_ada, g_pre_mix, w_in, g_q_lat, w_uq, g_kv_lat, w_ukv, lambda_q1, lambda_k1, lambda_q2, lambda_k2, g_diff_sub, w_out, g_post_mix, g_pre_ffn, w_router, b_router, w_gate, w_up, w_down, ws_gate, ws_up, ws_down, g_post_ffn):
    assert x.shape[0] == 1 and w_ada.shape[0] == 1
    out = _layer(x[0], c, w_ada[0], b_ada[0], g_pre_mix[0], w_in[0], g_q_lat[0], w_uq[0],
                 g_kv_lat[0], w_ukv[0], lambda_q1[0], lambda_k1[0], lambda_q2[0], lambda_k2[0],
                 g_diff_sub[0], w_out[0], g_post_mix[0], g_pre_ffn[0], w_router[0], b_router[0],
                 w_gate[0], w_up[0], w_down[0], ws_gate[0], ws_up[0], ws_down[0], g_post_ffn[0])
    return out[None]
```

```python
import functools
import math

import jax
import jax.numpy as jnp
import numpy as np
from jax import lax
from jax.experimental import pallas as pl
from jax.experimental.pallas import tpu as pltpu

F32 = jnp.float32
BF16 = jnp.bfloat16
U32 = jnp.uint32
I32 = jnp.int32

D_MODEL = 2048
CHUNK = 64
ROPE_THETA = 500000.0
EPS = 1e-6
LOG2E = 1.4426950408889634

MLA_HEADS = 8
MLA_Q_RANK = 768
MLA_KV_RANK = 512
MLA_NOPE = 128
MLA_ROPE = 64
MLA_V = 128
MLA_QK_PAD = 256

DIFF_HEADS = 8
DIFF_HEAD_DIM = 64
DIFF_ROT = DIFF_HEAD_DIM // 4
DIFF_W = DIFF_HEADS * 2 * DIFF_HEAD_DIM

N_EXPERTS = 64
TOP_K = 8
N_GROUPS = 8
GROUP_SIZE = N_EXPERTS // N_GROUPS
TOPK_GROUPS = 4
D_EXPERT = 512
ROUTED_SCALE = 2.5
LAMBDA_INIT = 0.8 - 0.6 * math.exp(-0.3 * 0)

LANES = 128
SUBLANES = 8
HALF = D_MODEL // 2
ROW_TILES = HALF // LANES

TM_PROJ = 256
ATT_BQ = 4096
ATT_BK = 1024
ATT_KV_PER_Q = ATT_BQ // ATT_BK
UNIT_W = 512
V_EXT = MLA_V + 16
EXP_BLK = 256
TD = 128
TR = 512
TN_ADA = 1024

VMEM_LIMIT = 56 * 1024 * 1024

assert ROW_TILES == SUBLANES


def _cparams(sem, vmem=VMEM_LIMIT, flags=None):
    return pltpu.CompilerParams(dimension_semantics=sem, vmem_limit_bytes=vmem, flags=flags)


def _dot(a, b):
    return jnp.dot(a, b, preferred_element_type=F32)


def _dot_nt(a, b):
    return lax.dot_general(a, b, (((1,), (1,)), ((), ())), preferred_element_type=F32)


def _rms(x):
    return x * lax.rsqrt(jnp.mean(x * x, axis=-1, keepdims=True) + EPS)


def _split_bf16(x):
    hi = x.astype(BF16)
    lo = (x - hi.astype(F32)).astype(BF16)
    return hi, lo


def _const_spec(shape):
    nd = len(shape)
    return pl.BlockSpec(shape, lambda *a: (0,) * nd, pipeline_mode=pl.Buffered(1))


def _ada_kernel(c_ref, w_ref, b_ref, o_ref):
    c = c_ref[...]
    a = c * jax.nn.sigmoid(c)
    a8 = jnp.broadcast_to(a, (SUBLANES, a.shape[1]))
    a_hi, a_lo = _split_bf16(a8)
    w_hi, w_lo = _split_bf16(w_ref[...])
    r = _dot(a_hi, w_hi) + _dot(a_lo, w_hi) + _dot(a_hi, w_lo)
    o_ref[...] = r[0:1] + b_ref[...]


def _ada(c, w, b):
    d, n = w.shape
    return pl.pallas_call(
        _ada_kernel,
        grid=(n // TN_ADA,),
        in_specs=[pl.BlockSpec((1, d), lambda j: (0, 0)),
                  pl.BlockSpec((d, TN_ADA), lambda j: (0, j)),
                  pl.BlockSpec((1, TN_ADA), lambda j: (0, j))],
        out_specs=pl.BlockSpec((1, TN_ADA), lambda j: (0, j)),
        out_shape=jax.ShapeDtypeStruct((1, n), F32),
        compiler_params=_cparams(("arbitrary",)),
        name="ada_mod",
    )(c, w, b)


def _tile_lanes(t, reps):
    return jnp.concatenate([t] * reps, axis=1)


def _rope_lanes(x, c, sa, sb, half):
    n = x.shape[1]
    return x * c + pltpu.roll(x, n - half, 1) * sa + pltpu.roll(x, half, 1) * sb


OFF_LAT = MLA_Q_RANK + MLA_KV_RANK
OFF_DQ = OFF_LAT
OFF_DK = OFF_DQ + DIFF_W
OFF_KPE = OFF_DK + DIFF_W
D_IN_PAD = OFF_KPE + LANES


def _inproj_kernel(x_ref, g_ref, sc_ref, sh_ref, w_ref, wvt_ref, c_ref, sa_ref, sb_ref,
                   lat_ref, kpe_ref, dq_ref, dk_ref, dvt_ref):
    x = x_ref[...]
    h = _rms(x) * g_ref[...] * (1.0 + sc_ref[...]) + sh_ref[...]
    hb = h.astype(BF16)
    reps = DIFF_W // LANES
    c = _tile_lanes(c_ref[...], reps)
    sa = _tile_lanes(sa_ref[...], reps)
    sb = _tile_lanes(sb_ref[...], reps)
    half = DIFF_ROT // 2
    lat_ref[...] = _dot(hb, w_ref[:, 0:OFF_LAT])
    q = _dot(hb, w_ref[:, OFF_DQ:OFF_DK])
    dq_ref[...] = (_rope_lanes(q, c, sa, sb, half) * (DIFF_HEAD_DIM ** -0.5 * LOG2E)).astype(BF16)
    k = _dot(hb, w_ref[:, OFF_DK:OFF_KPE])
    dk_ref[...] = _rope_lanes(k, c, sa, sb, half).astype(BF16)
    kpe_ref[...] = _dot(hb, w_ref[:, OFF_KPE:D_IN_PAD])
    _store_vt_ext(dvt_ref, _dot_nt(wvt_ref[...], hb), DIFF_HEADS)


def _inproj(x, g, sc, sh, w_r, wv_t, tabs):
    s, d = x.shape
    tm = min(TM_PROJ, s)
    row = lambda n: pl.BlockSpec((tm, n), lambda i: (i, 0))
    return pl.pallas_call(
        _inproj_kernel,
        grid=(s // tm,),
        in_specs=[row(d), _const_spec((1, d)), _const_spec((1, d)), _const_spec((1, d)),
                  _const_spec(w_r.shape), _const_spec(wv_t.shape), row(LANES), row(LANES), row(LANES)],
        out_specs=[row(OFF_LAT), row(LANES), row(DIFF_W), row(DIFF_W),
                   pl.BlockSpec((DIFF_HEADS * V_EXT, tm), lambda i: (0, i))],
        out_shape=[jax.ShapeDtypeStruct((s, OFF_LAT), F32),
                   jax.ShapeDtypeStruct((s, LANES), F32),
                   jax.ShapeDtypeStruct((s, DIFF_W), BF16),
                   jax.ShapeDtypeStruct((s, DIFF_W), BF16),
                   jax.ShapeDtypeStruct((DIFF_HEADS * V_EXT, s), BF16)],
        compiler_params=_cparams(("arbitrary",)),
        name="in_proj",
    )(x, g, sc, sh, w_r, wv_t, *tabs)


def _mla_prep_kernel(lat_ref, kpe_ref, gq_ref, gkv_ref, wuq_ref, wuk_ref, wvt_ref,
                     c_ref, sa_ref, sb_ref, q_ref, k_ref, vt_ref):
    lat = lat_ref[...]
    qn = (_rms(lat[:, :MLA_Q_RANK]) * gq_ref[...]).astype(BF16)
    kvn = (_rms(lat[:, MLA_Q_RANK:]) * gkv_ref[...]).astype(BF16)
    q = _dot(qn, wuq_ref[...])
    kn = _dot(kvn, wuk_ref[...])
    c, sa, sb = c_ref[...], sa_ref[...], sb_ref[...]
    half = MLA_ROPE // 2
    qs = (MLA_NOPE + MLA_ROPE) ** -0.5 * LOG2E
    kpe = _rope_lanes(kpe_ref[...], c, sa, sb, half).astype(BF16)
    for h in range(MLA_HEADS):
        o = h * MLA_QK_PAD
        q_ref[:, o:o + LANES] = (q[:, o:o + LANES] * qs).astype(BF16)
        q_ref[:, o + LANES:o + 2 * LANES] = (
            _rope_lanes(q[:, o + LANES:o + 2 * LANES], c, sa, sb, half) * qs).astype(BF16)
        k_ref[:, o:o + LANES] = kn[:, h * MLA_NOPE:(h + 1) * MLA_NOPE].astype(BF16)
        k_ref[:, o + LANES:o + 2 * LANES] = kpe
    _store_vt_ext(vt_ref, _dot_nt(wvt_ref[...], kvn), MLA_HEADS)


def _mla_prep(lat, kpe, gq, gkv, wuq_r, wuk_r, wv_t, tabs):
    s = lat.shape[0]
    tm = min(TM_PROJ, s)
    row = lambda n: pl.BlockSpec((tm, n), lambda i: (i, 0))
    hq = MLA_HEADS * MLA_QK_PAD
    hv = MLA_HEADS * V_EXT
    return pl.pallas_call(
        _mla_prep_kernel,
        grid=(s // tm,),
        in_specs=[row(OFF_LAT), row(LANES), _const_spec(gq.shape), _const_spec(gkv.shape),
                  _const_spec(wuq_r.shape), _const_spec(wuk_r.shape), _const_spec(wv_t.shape),
                  row(LANES), row(LANES), row(LANES)],
        out_specs=[row(hq), row(hq), pl.BlockSpec((hv, tm), lambda i: (0, i))],
        out_shape=[jax.ShapeDtypeStruct((s, hq), BF16),
                   jax.ShapeDtypeStruct((s, hq), BF16),
                   jax.ShapeDtypeStruct((hv, s), BF16)],
        compiler_params=_cparams(("arbitrary",)),
        name="mla_prep",
    )(lat, kpe, gq, gkv, wuq_r, wuk_r, wv_t, *tabs)


def _pair_tables(nq):
    n = [ATT_KV_PER_Q * (i + 1) for i in range(nq)]
    qi = np.concatenate([np.full(c, i) for i, c in enumerate(n)]).astype(np.int32)
    kj = np.concatenate([np.arange(c) for c in n]).astype(np.int32)
    return jnp.asarray(qi), jnp.asarray(kj)


def _ones_rows(n):
    rows = V_EXT - MLA_V
    return (lax.broadcasted_iota(I32, (rows, n), 0) == 0).astype(BF16)


def _store_vt_ext(vt_ref, vt, heads):
    ones = _ones_rows(vt.shape[1])
    for h in range(heads):
        vt_ref[h * V_EXT:h * V_EXT + MLA_V, :] = vt[h * MLA_V:(h + 1) * MLA_V, :].astype(BF16)
        vt_ref[h * V_EXT + MLA_V:(h + 1) * V_EXT, :] = ones


def _attn_step(q_refs, k_ref, vt_ref, s_bufs, m_refs, acc_refs, rel):
    w = UNIT_W
    nk = k_ref.shape[0]
    per_kv = nk // w
    specs = []
    for u in range(q_refs[0].shape[0] // w):
        d = None if rel is None else u - rel * per_kv
        if d is None or d >= per_kv:
            specs.append((u, nk, None))
        elif d >= 0:
            specs.append((u, (d + 1) * w, d * w))
    units = [(si, sp) for sp in specs for si in range(len(q_refs))]

    def scores(n):
        si, (u, rows, off) = units[n]
        s = _dot_nt(k_ref[0:rows, :], q_refs[si][u * w:(u + 1) * w, :])
        if off is not None:
            kc = lax.broadcasted_iota(I32, (rows, w), 0) // CHUNK
            qc = (lax.broadcasted_iota(I32, (rows, w), 1) + off) // CHUNK
            s = jnp.where(kc <= qc, s, -jnp.inf)
        s_bufs[n % 2][0:rows, :] = s
        return jnp.max(s, axis=0, keepdims=True)

    cmax_next = scores(0)
    for n, (si, (u, rows, _)) in enumerate(units):
        cmax = cmax_next
        if n + 1 < len(units):
            cmax_next = scores(n + 1)
        m_ref, acc_ref = m_refs[si][u], acc_refs[si][u]
        m_prev = m_ref[...]
        m_new = jnp.maximum(m_prev, cmax)
        m_ref[...] = m_new
        acc_ref[...] = jnp.exp2(m_prev - m_new) * acc_ref[...]
        p = jnp.exp2(s_bufs[n % 2][0:rows, :] - m_new).astype(BF16)
        acc_ref[...] += _dot(vt_ref[:, 0:rows], p)


def _init_stats(ms, accs):
    for m, a in zip(ms, accs):
        m[...] = jnp.full(m.shape, -jnp.inf, F32)
        a[...] = jnp.zeros(a.shape, F32)


def _normalized(acc_ref):
    acc = acc_ref[...]
    return acc[0:MLA_V, :] / acc[MLA_V:MLA_V + 1, :]


def _attn_phases(qi_ref, kj_ref, init, step, finish):
    t = pl.program_id(1)
    rel = kj_ref[t] - ATT_KV_PER_Q * qi_ref[t]

    @pl.when(kj_ref[t] == 0)
    def _():
        init()

    @pl.when(rel < 0)
    def _():
        step(None)

    for r in range(ATT_KV_PER_Q):
        @pl.when(rel == r)
        def _(r=r):
            step(r)
            if r == ATT_KV_PER_Q - 1:
                finish()


def _unit_scratch(n_streams):
    nu = ATT_BQ // UNIT_W
    return ([pltpu.VMEM((ATT_BK, UNIT_W), F32)] * 2
            + [pltpu.VMEM((1, UNIT_W), F32)] * (nu * n_streams)
            + [pltpu.VMEM((V_EXT, UNIT_W), F32)] * (nu * n_streams))


def _split_units(scr, n_streams):
    nu = ATT_BQ // UNIT_W
    ms = [list(scr[s * nu:(s + 1) * nu]) for s in range(n_streams)]
    accs = [list(scr[(n_streams + s) * nu:(n_streams + s + 1) * nu]) for s in range(n_streams)]
    return ms, accs


def _mla_attn_kernel(qi_ref, kj_ref, q_ref, k_ref, vt_ref, o_ref, *scr):
    s_bufs, (ms, accs) = scr[:2], _split_units(scr[2:], 1)

    def finish():
        for u, a in enumerate(accs[0]):
            o_ref[u * UNIT_W:(u + 1) * UNIT_W, :] = _normalized(a).T.astype(o_ref.dtype)

    _attn_phases(qi_ref, kj_ref,
                 lambda: _init_stats(ms[0], accs[0]),
                 lambda rel: _attn_step([q_ref], k_ref, vt_ref, s_bufs, ms, accs, rel),
                 finish)


def _mla_attn(qc, kc, vt):
    s = qc.shape[0]
    assert s % ATT_BQ == 0
    qi, kj = _pair_tables(s // ATT_BQ)
    gs = pltpu.PrefetchScalarGridSpec(
        num_scalar_prefetch=2,
        grid=(MLA_HEADS, qi.shape[0]),
        in_specs=[pl.BlockSpec((ATT_BQ, MLA_QK_PAD), lambda h, t, qi, kj: (qi[t], h)),
                  pl.BlockSpec((ATT_BK, MLA_QK_PAD), lambda h, t, qi, kj: (kj[t], h)),
                  pl.BlockSpec((V_EXT, ATT_BK), lambda h, t, qi, kj: (h, kj[t]))],
        out_specs=pl.BlockSpec((ATT_BQ, MLA_V), lambda h, t, qi, kj: (qi[t], h)),
        scratch_shapes=_unit_scratch(1),
    )
    return pl.pallas_call(
        _mla_attn_kernel,
        grid_spec=gs,
        out_shape=jax.ShapeDtypeStruct((s, MLA_HEADS * MLA_V), BF16),
        compiler_params=_cparams(("arbitrary", "arbitrary")),
        name="mla_attn",
    )(qi, kj, qc, kc, vt)


def _diff_attn_kernel(qi_ref, kj_ref, q_ref, k_ref, vt_ref, lq1_ref, lk1_ref, lq2_ref, lk2_ref,
                      g_ref, o_ref, q0_scr, q1_scr, *scr):
    s_bufs, (ms, accs) = scr[:2], _split_units(scr[2:], 2)

    def init():
        q = q_ref[...]
        lane = lax.broadcasted_iota(I32, q.shape, 1)
        zero = jnp.zeros_like(q)
        q0_scr[...] = jnp.where(lane < DIFF_HEAD_DIM, q, zero)
        q1_scr[...] = jnp.where(lane >= DIFF_HEAD_DIM, q, zero)
        _init_stats(ms[0] + ms[1], accs[0] + accs[1])

    def finish():
        lam = (jnp.exp(jnp.sum(lq1_ref[...] * lk1_ref[...], axis=1, keepdims=True))
               - jnp.exp(jnp.sum(lq2_ref[...] * lk2_ref[...], axis=1, keepdims=True))
               + LAMBDA_INIT)
        for u, (a0, a1) in enumerate(zip(accs[0], accs[1])):
            o = _normalized(a0) - lam * _normalized(a1)
            o = o * lax.rsqrt(jnp.mean(o * o, axis=0, keepdims=True) + EPS)
            o_ref[u * UNIT_W:(u + 1) * UNIT_W, :] = (
                o * g_ref[...] * (1.0 - LAMBDA_INIT)).T.astype(o_ref.dtype)

    _attn_phases(qi_ref, kj_ref, init,
                 lambda rel: _attn_step([q0_scr, q1_scr], k_ref, vt_ref, s_bufs, ms, accs, rel),
                 finish)


def _diff_attn(dq, dk, dvt, lq1, lk1, lq2, lk2, g_col):
    s = dq.shape[0]
    assert s % ATT_BQ == 0
    qi, kj = _pair_tables(s // ATT_BQ)
    hw = 2 * DIFF_HEAD_DIM
    small = lambda a: pl.BlockSpec(a.shape, lambda h, t, qi, kj: (0, 0))
    gs = pltpu.PrefetchScalarGridSpec(
        num_scalar_prefetch=2,
        grid=(DIFF_HEADS, qi.shape[0]),
        in_specs=[pl.BlockSpec((ATT_BQ, hw), lambda h, t, qi, kj: (qi[t], h)),
                  pl.BlockSpec((ATT_BK, hw), lambda h, t, qi, kj: (kj[t], h)),
                  pl.BlockSpec((V_EXT, ATT_BK), lambda h, t, qi, kj: (h, kj[t])),
                  small(lq1), small(lk1), small(lq2), small(lk2), small(g_col)],
        out_specs=pl.BlockSpec((ATT_BQ, hw), lambda h, t, qi, kj: (qi[t], h)),
        scratch_shapes=[pltpu.VMEM((ATT_BQ, hw), BF16), pltpu.VMEM((ATT_BQ, hw), BF16)] + _unit_scratch(2),
    )
    return pl.pallas_call(
        _diff_attn_kernel,
        grid_spec=gs,
        out_shape=jax.ShapeDtypeStruct((s, DIFF_W), BF16),
        compiler_params=_cparams(("arbitrary", "arbitrary")),
        name="diff_attn",
    )(qi, kj, dq, dk, dvt, lq1, lk1, lq2, lk2, g_col)


def _pack_words(y):
    lo = pltpu.bitcast(y[:, :HALF].astype(BF16).astype(F32), U32) >> 16
    hi = pltpu.bitcast(y[:, HALF:].astype(BF16).astype(F32), U32) & jnp.uint32(0xFFFF0000)
    return lo | hi


def _store_packed(ref, words, rows):
    for s in range(ROW_TILES):
        ref[pl.ds(s, rows, stride=ROW_TILES), :] = words[:, s * LANES:(s + 1) * LANES]


def _load_packed(ref, rows):
    return jnp.concatenate(
        [ref[pl.ds(s, rows, stride=ROW_TILES), :] for s in range(ROW_TILES)], axis=1)


def _unpack_words(w):
    lo = pltpu.bitcast(w << 16, F32)
    hi = pltpu.bitcast(w & jnp.uint32(0xFFFF0000), F32)
    return lo, hi


def _outproj_kernel(om_ref, od_ref, x_ref, w_ref, gt_ref, gpost_ref, gpre_ref, sc_ref, sh_ref,
                    wr_ref, x1_ref, h2_ref, h2p_ref, lg_ref):
    y = _dot(jnp.concatenate([om_ref[...], od_ref[...]], axis=1), w_ref[...])
    x1 = x_ref[...] + gt_ref[...] * (_rms(y) * gpost_ref[...])
    x1_ref[...] = x1
    h2 = _rms(x1) * gpre_ref[...] * (1.0 + sc_ref[...]) + sh_ref[...]
    h_hi, h_lo = _split_bf16(h2)
    h2_ref[...] = h_hi
    _store_packed(h2p_ref, _pack_words(h2), h2.shape[0])
    w_hi, w_lo = _split_bf16(wr_ref[...])
    ne = w_hi.shape[0]
    a = _dot_nt(jnp.concatenate([w_hi, w_lo], axis=0), h_hi)
    lg_ref[...] = a[0:ne] + a[ne:] + _dot_nt(w_hi, h_lo)


def _outproj(om, od, x, w_out, gt, gpost, gpre, sc, sh, wr_t):
    s, d = x.shape
    tm = min(TM_PROJ, s)
    row = lambda n: pl.BlockSpec((tm, n), lambda i: (i, 0))
    vec = _const_spec((1, d))
    return pl.pallas_call(
        _outproj_kernel,
        grid=(s // tm,),
        in_specs=[row(om.shape[1]), row(od.shape[1]), row(d), _const_spec(w_out.shape),
                  vec, vec, vec, vec, vec, _const_spec(wr_t.shape)],
        out_specs=[row(d), row(d),
                   pl.BlockSpec((tm * ROW_TILES, LANES), lambda i: (i, 0)),
                   pl.BlockSpec((N_EXPERTS, tm), lambda i: (0, i))],
        out_shape=[jax.ShapeDtypeStruct((s, d), F32),
                   jax.ShapeDtypeStruct((s, d), BF16),
                   jax.ShapeDtypeStruct((s * ROW_TILES, LANES), U32),
                   jax.ShapeDtypeStruct((N_EXPERTS, s), F32)],
        compiler_params=_cparams(("arbitrary",)),
        name="out_proj",
    )(om, od, x, w_out, gt, gpost, gpre, sc, sh, wr_t)


def _route_kernel(lg_ref, b_ref, eidx_ref, w_ref, rank_ref, cnt_ref, carry):
    step = pl.program_id(0)

    @pl.when(step == 0)
    def _():
        carry[...] = jnp.zeros(carry.shape, F32)

    tr = lg_ref.shape[1]
    ninf = jnp.float32(-jnp.inf)
    ie = lax.broadcasted_iota(I32, (GROUP_SIZE, tr), 0)
    rmax = lambda a: jnp.max(a, axis=0, keepdims=True)
    rmin = lambda a: jnp.min(a, axis=0, keepdims=True)
    rsum = lambda a: jnp.sum(a, axis=0, keepdims=True)

    sc, bi, gscore = [], [], []
    for g in range(N_GROUPS):
        lg = lg_ref[g * GROUP_SIZE:(g + 1) * GROUP_SIZE, :]
        s = jax.nn.sigmoid(lg)
        b = s + b_ref[g * GROUP_SIZE:(g + 1) * GROUP_SIZE, :]
        m1 = rmax(b)
        i1 = rmin(jnp.where(b == m1, ie, GROUP_SIZE))
        m2 = rmax(jnp.where(ie == i1, ninf, b))
        sc.append(s)
        bi.append(b)
        gscore.append(m1 + m2)

    gsel = [jnp.zeros((1, tr), I32) for _ in range(N_GROUPS)]
    for _ in range(TOPK_GROUPS):
        gm = functools.reduce(jnp.maximum, gscore)
        gi = functools.reduce(
            jnp.minimum, [jnp.where(gscore[g] == gm, g, N_GROUPS) for g in range(N_GROUPS)])
        for g in range(N_GROUPS):
            hit = gi == g
            gsel[g] = jnp.where(hit, 1, gsel[g])
            gscore[g] = jnp.where(hit, ninf, gscore[g])

    masked = [jnp.where(jnp.broadcast_to(gsel[g], (GROUP_SIZE, tr)) > 0, bi[g], ninf)
              for g in range(N_GROUPS)]
    sel = [jnp.zeros((GROUP_SIZE, tr), F32) for _ in range(N_GROUPS)]
    idxs, svals = [], []
    for _ in range(TOP_K):
        m = functools.reduce(jnp.maximum, [rmax(x) for x in masked])
        idx = functools.reduce(
            jnp.minimum,
            [rmin(jnp.where(masked[g] == m, ie + g * GROUP_SIZE, N_EXPERTS)) for g in range(N_GROUPS)])
        sv = jnp.zeros((1, tr), F32)
        for g in range(N_GROUPS):
            hit = (ie + g * GROUP_SIZE) == idx
            sv = sv + rsum(jnp.where(hit, sc[g], 0.0))
            masked[g] = jnp.where(hit, ninf, masked[g])
            sel[g] = jnp.where(hit, 1.0, sel[g])
        idxs.append(idx)
        svals.append(sv)
    tot = functools.reduce(lambda a, b: a + b, svals)

    sel2 = jnp.concatenate(sel, axis=0)
    before = (lax.broadcasted_iota(I32, (tr, tr), 0) < lax.broadcasted_iota(I32, (tr, tr), 1))
    rank2 = _dot(sel2.astype(BF16), before.astype(BF16)) + carry[:, 0:1]
    for k in range(TOP_K):
        rk = jnp.zeros((1, tr), F32)
        for g in range(N_GROUPS):
            hit = (ie + g * GROUP_SIZE) == idxs[k]
            rk = rk + rsum(jnp.where(hit, rank2[g * GROUP_SIZE:(g + 1) * GROUP_SIZE, :], 0.0))
        eidx_ref[k:k + 1, :] = idxs[k]
        w_ref[k:k + 1, :] = svals[k] / tot * ROUTED_SCALE
        rank_ref[k:k + 1, :] = rk.astype(I32)
    carry[...] = carry[...] + jnp.sum(sel2, axis=1, keepdims=True)
    cnt_ref[...] = carry[...]


def _route(lg_t, b_col):
    e, t = lg_t.shape
    tr = min(TR, t)
    tok = lambda: pl.BlockSpec((TOP_K, tr), lambda i: (0, i))
    return pl.pallas_call(
        _route_kernel,
        grid=(t // tr,),
        in_specs=[pl.BlockSpec((e, tr), lambda i: (0, i)), pl.BlockSpec((e, 1), lambda i: (0, 0))],
        out_specs=[tok(), tok(), tok(), pl.BlockSpec((e, LANES), lambda i: (0, 0))],
        out_shape=[jax.ShapeDtypeStruct((TOP_K, t), I32),
                   jax.ShapeDtypeStruct((TOP_K, t), F32),
                   jax.ShapeDtypeStruct((TOP_K, t), I32),
                   jax.ShapeDtypeStruct((e, LANES), F32)],
        scratch_shapes=[pltpu.VMEM((e, LANES), F32)],
        compiler_params=_cparams(("arbitrary",)),
        name="route",
    )(lg_t, b_col)


def _pos_kernel(pstart_ref, eidx_ref, rank_ref, pos_ref):
    e = eidx_ref[...]
    pos = rank_ref[...]
    for x in range(N_EXPERTS):
        pos = pos + jnp.where(e == x, pstart_ref[x], 0)
    pos_ref[...] = pos


def _positions(pstart, eidx, rank):
    k, t = eidx.shape
    tr = min(TR, t)
    gs = pltpu.PrefetchScalarGridSpec(
        num_scalar_prefetch=1,
        grid=(t // tr,),
        in_specs=[pl.BlockSpec((k, tr), lambda i, p: (0, i)), pl.BlockSpec((k, tr), lambda i, p: (0, i))],
        out_specs=pl.BlockSpec((k, tr), lambda i, p: (0, i)),
    )
    return pl.pallas_call(
        _pos_kernel, grid_spec=gs,
        out_shape=jax.ShapeDtypeStruct((k, t), I32),
        compiler_params=_cparams(("arbitrary",)),
        name="positions",
    )(pstart, eidx, rank)


def _row_copy(src, src_row, dst, dst_row, sem):
    return pltpu.make_async_copy(src.at[pl.ds(src_row * ROW_TILES, ROW_TILES)],
                                 dst.at[pl.ds(dst_row * ROW_TILES, ROW_TILES)], sem)


def _dispatch_kernel(pos_ref, h_ref, xs_ref, sem):
    td = pos_ref.shape[1]

    def issue(t, c):
        for k in range(TOP_K):
            _row_copy(h_ref, t, xs_ref, pos_ref[k, t], sem).start(priority=k % 2)
        return c

    lax.fori_loop(0, td, issue, 0, unroll=4)
    for k in range(TOP_K):
        pltpu.make_async_copy(h_ref, xs_ref.at[pl.ds(0, td * ROW_TILES)], sem).wait()


def _dispatch(pos, h2p, m_pad):
    k, t = pos.shape
    td = min(TD, t)
    return pl.pallas_call(
        _dispatch_kernel,
        grid=(t // td,),
        in_specs=[pl.BlockSpec((k, td), lambda i: (0, i), memory_space=pltpu.SMEM),
                  pl.BlockSpec((td * ROW_TILES, LANES), lambda i: (i, 0))],
        out_specs=pl.BlockSpec(memory_space=pl.ANY),
        out_shape=jax.ShapeDtypeStruct((m_pad * ROW_TILES, LANES), U32),
        scratch_shapes=[pltpu.SemaphoreType.DMA],
        compiler_params=_cparams(("arbitrary",)),
        name="dispatch",
    )(pos, h2p)


def _weight_copies(w_hbm, w_buf, sem, e, s):
    return [pltpu.make_async_copy(h.at[e], b.at[s], sem.at[s]) for h, b in zip(w_hbm, w_buf)]


def _experts_kernel(be_ref, nu_ref, fresh_ref, nxt_ref, slot_ref, x_ref, wg_hbm, wu_hbm, wd_hbm, y_ref,
                    wg_buf, wu_buf, wd_buf, wg_s, wu_s, wd_s, sem):
    i = pl.program_id(0)
    used = i < nu_ref[0]
    w_hbm = (wg_hbm, wu_hbm, wd_hbm)
    w_buf = (wg_buf, wu_buf, wd_buf)

    @pl.when(jnp.logical_and(used, fresh_ref[i] == 1))
    def _():
        s = slot_ref[i]

        @pl.when(i == 0)
        def _():
            for c in _weight_copies(w_hbm, w_buf, sem, be_ref[0], 0):
                c.start()

        for c in _weight_copies(w_hbm, w_buf, sem, be_ref[i], s):
            c.wait()

        @pl.when(nxt_ref[i] >= 0)
        def _():
            for c in _weight_copies(w_hbm, w_buf, sem, nxt_ref[i], 1 - s):
                c.start()

        wg_s[...] = wg_buf[s].astype(BF16)
        wu_s[...] = wu_buf[s].astype(BF16)
        wd_s[...] = wd_buf[s].astype(BF16)

    @pl.when(used)
    def _():
        rows = x_ref.shape[0] // ROW_TILES
        xa, xb = _unpack_words(_load_packed(x_ref, rows))
        xa = xa.astype(BF16)
        xb = xb.astype(BF16)
        g = _dot(xa, wg_s[0:HALF, :]) + _dot(xb, wg_s[HALF:, :])
        u = _dot(xa, wu_s[0:HALF, :]) + _dot(xb, wu_s[HALF:, :])
        a = (g * jax.nn.sigmoid(g) * u).astype(BF16)
        y = _dot(a, wd_s[...])
        _store_packed(y_ref, _pack_words(y), rows)


def _experts(blk_e, nused, fresh, nxt_e, slot, xs, w_gate, w_up, w_down, blk):
    n_blk = blk_e.shape[0]
    ne, d, de = w_gate.shape
    xmap = lambda i, be, nu, fr, nx, sl: (jnp.minimum(i, nu[0] - 1), 0)
    hbm = pl.BlockSpec(memory_space=pl.ANY)
    gs = pltpu.PrefetchScalarGridSpec(
        num_scalar_prefetch=5,
        grid=(n_blk,),
        in_specs=[pl.BlockSpec((blk * ROW_TILES, LANES), xmap), hbm, hbm, hbm],
        out_specs=pl.BlockSpec((blk * ROW_TILES, LANES), xmap),
        scratch_shapes=[pltpu.VMEM((2, d, de), F32), pltpu.VMEM((2, d, de), F32), pltpu.VMEM((2, de, d), F32),
                        pltpu.VMEM((d, de), BF16), pltpu.VMEM((d, de), BF16), pltpu.VMEM((de, d), BF16),
                        pltpu.SemaphoreType.DMA((2,))],
    )
    return pl.pallas_call(
        _experts_kernel, grid_spec=gs,
        out_shape=jax.ShapeDtypeStruct(xs.shape, U32),
        compiler_params=_cparams(("arbitrary",)),
        name="experts",
    )(blk_e, nused, fresh, nxt_e, slot, xs, w_gate, w_up, w_down)


def _combine_kernel(pos_ref, wt_ref, ys_ref, h_ref, x1_ref, wsg_ref, wsu_ref, wsd_ref,
                    gt_ref, gpost_ref, o_ref, gbuf, sem):
    td = pos_ref.shape[1]

    def issue(t, c):
        for k in range(TOP_K):
            _row_copy(ys_ref, pos_ref[k, t], gbuf.at[k], t, sem).start(priority=k % 2)
        return c

    lax.fori_loop(0, td, issue, 0, unroll=4)

    hb = h_ref[...]
    g = _dot(hb, wsg_ref[...])
    u = _dot(hb, wsu_ref[...])
    y = _dot((g * jax.nn.sigmoid(g) * u).astype(BF16), wsd_ref[...])

    wpad = jnp.concatenate([wt_ref[...], jnp.zeros((LANES - TOP_K, td), F32)], axis=0)
    wcol = wpad.T

    for k in range(TOP_K):
        pltpu.make_async_copy(ys_ref.at[pl.ds(0, td * ROW_TILES)], gbuf.at[k], sem).wait()
    lo = jnp.zeros((td, HALF), F32)
    hi = jnp.zeros((td, HALF), F32)
    for k in range(TOP_K):
        a, b = _unpack_words(_load_packed(gbuf.at[k], td))
        wk = wcol[:, k:k + 1]
        lo = lo + wk * a
        hi = hi + wk * b
    y = y + jnp.concatenate([lo, hi], axis=1)
    o_ref[...] = x1_ref[...] + gt_ref[...] * (_rms(y) * gpost_ref[...])


def _combine(pos, wts, ys, h2, x1, wsg, wsu, wsd, gt, gpost):
    k, t = pos.shape
    d = x1.shape[1]
    td = min(TD, t)
    assert td == LANES
    row = lambda n: pl.BlockSpec((td, n), lambda i: (i, 0))
    vec = _const_spec((1, d))
    return pl.pallas_call(
        _combine_kernel,
        grid=(t // td,),
        in_specs=[pl.BlockSpec((k, td), lambda i: (0, i), memory_space=pltpu.SMEM),
                  pl.BlockSpec((k, td), lambda i: (0, i)),
                  pl.BlockSpec(memory_space=pl.ANY),
                  row(d), row(d), _const_spec(wsg.shape), _const_spec(wsu.shape), _const_spec(wsd.shape),
                  vec, vec],
        out_specs=row(d),
        out_shape=jax.ShapeDtypeStruct((t, d), F32),
        scratch_shapes=[pltpu.VMEM((k, td * ROW_TILES, LANES), U32), pltpu.SemaphoreType.DMA],
        compiler_params=_cparams(("arbitrary",)),
        name="combine",
    )(pos, wts, ys, h2, x1, wsg, wsu, wsd, gt, gpost)


def _rope_tables(seq, dim):
    pos = jnp.arange(seq, dtype=F32)
    inv = ROPE_THETA ** (-jnp.arange(0, dim, 2, dtype=F32) / dim)
    ang = pos[:, None] * inv[None, :]
    return jnp.cos(ang), jnp.sin(ang)


def _lane_tables(seq, dim, period):
    cos, sin = _rope_tables(seq, dim)
    half = dim // 2
    r = jnp.arange(LANES) % period
    idx = r % half
    c = jnp.where(r < dim, cos[:, idx], 1.0)
    sa = jnp.where(r < half, -sin[:, idx], 0.0)
    sb = jnp.where((r >= half) & (r < dim), sin[:, idx], 0.0)
    return c, sa, sb


def _moe(h2, h2p, x1, lg_t, b_router, w_gate, w_up, w_down, wsg, wsu, wsd, gt_f, g_post_ffn):
    t = h2.shape[0]
    blk = EXP_BLK
    m_pad = t * TOP_K + N_EXPERTS * blk
    eidx, wts, rank, cnt = _route(lg_t, b_router.reshape(N_EXPERTS, 1))
    counts = cnt[:, 0].astype(I32)
    padded = (counts + blk - 1) // blk * blk
    pend = jnp.cumsum(padded)
    pstart = pend - padded
    n_blk = m_pad // blk
    blk_start = jnp.arange(n_blk, dtype=I32) * blk
    blk_e = jnp.minimum(jnp.sum(pend[None, :] <= blk_start[:, None], axis=1), N_EXPERTS - 1).astype(I32)
    nused = (pend[-1:] // blk).astype(I32)
    pos = _positions(pstart.astype(I32), eidx, rank)
    xs = _dispatch(pos, h2p, m_pad)
    idx = jnp.arange(n_blk, dtype=I32)
    fresh = (idx < nused[0]) & (blk_e != jnp.concatenate([jnp.full((1,), -1, I32), blk_e[:-1]]))
    slot = ((jnp.cumsum(fresh.astype(I32)) - 1) % 2).astype(I32)
    first_at_or_after = lax.cummin(jnp.where(fresh, idx, n_blk)[::-1])[::-1]
    nxt_idx = jnp.concatenate([first_at_or_after[1:], jnp.full((1,), n_blk, I32)])
    nxt_e = jnp.where(nxt_idx < n_blk, blk_e[jnp.minimum(nxt_idx, n_blk - 1)], -1).astype(I32)
    ys = _experts(blk_e, nused, fresh.astype(I32), nxt_e, slot, xs, w_gate, w_up, w_down, blk)
    return _combine(pos, wts, ys, h2, x1, wsg, wsu, wsd, gt_f, g_post_ffn)


def _layer(x, c, w_ada, b_ada, g_pre_mix, w_in, g_q_lat, w_uq, g_kv_lat, w_ukv,
           lq1, lk1, lq2, lk2, g_diff_sub, w_out, g_post_mix, g_pre_ffn, w_router, b_router,
           w_gate, w_up, w_down, ws_gate, ws_up, ws_down, g_post_ffn):
    s, d = x.shape
    row = lambda a: a.reshape(1, -1)

    mod = _ada(c, w_ada, row(b_ada))
    sh_a, sc_a, gt_a, sh_f, sc_f, gt_f = [mod[:, i * d:(i + 1) * d] for i in range(6)]

    o1 = MLA_Q_RANK + MLA_KV_RANK
    o2 = o1 + MLA_ROPE
    o3 = o2 + 2 * DIFF_W
    w_in_r = jnp.concatenate(
        [w_in[:, :o1], w_in[:, o2:o3], w_in[:, o1:o2], jnp.zeros((d, LANES - MLA_ROPE), w_in.dtype)],
        axis=1).astype(BF16)
    lat, kpe, dq, dk, dvt = _inproj(x, row(g_pre_mix), sc_a, sh_a, w_in_r, w_in[:, o3:].T.astype(BF16),
                                    _lane_tables(s, DIFF_ROT, DIFF_HEAD_DIM))

    wq = w_uq.reshape(MLA_Q_RANK, MLA_HEADS, MLA_NOPE + MLA_ROPE)
    wq = jnp.pad(wq, ((0, 0), (0, 0), (0, MLA_QK_PAD - MLA_NOPE - MLA_ROPE)))
    wuq_r = wq.reshape(MLA_Q_RANK, MLA_HEADS * MLA_QK_PAD).astype(BF16)
    wkv = w_ukv.reshape(MLA_KV_RANK, MLA_HEADS, MLA_NOPE + MLA_V)
    wuk_r = wkv[:, :, :MLA_NOPE].reshape(MLA_KV_RANK, -1).astype(BF16)
    wv_t = wkv[:, :, MLA_NOPE:].reshape(MLA_KV_RANK, -1).T.astype(BF16)
    qc, kc, vt = _mla_prep(lat, kpe, row(g_q_lat), row(g_kv_lat), wuq_r, wuk_r, wv_t,
                           _lane_tables(s, MLA_ROPE, MLA_ROPE))

    o_mla = _mla_attn(qc, kc, vt)
    o_diff = _diff_attn(dq, dk, dvt, row(lq1), row(lk1), row(lq2), row(lk2), g_diff_sub.reshape(-1, 1))

    x1, h2, h2p, lg_t = _outproj(o_mla, o_diff, x, w_out.astype(BF16), gt_a, row(g_post_mix),
                                 row(g_pre_ffn), sc_f, sh_f, w_router.T)
    return _moe(h2, h2p, x1, lg_t, b_router, w_gate, w_up, w_down,
                ws_gate.astype(BF16), ws_up.astype(BF16), ws_down.astype(BF16), gt_f, row(g_post_ffn))


def kernel(x, c, w_ada, b_ada, g_pre_mix, w_in, g_q_lat, w_uq, g_kv_lat, w_ukv, lambda_q1, lambda_k1, lambda_q2, lambda_k2, g_diff_sub, w_out, g_post_mix, g_pre_ffn, w_router, b_router, w_gate, w_up, w_down, ws_gate, ws_up, ws_down, g_post_ffn):
    assert x.shape[0] == 1 and w_ada.shape[0] == 1
    out = _layer(x[0], c, w_ada[0], b_ada[0], g_pre_mix[0], w_in[0], g_q_lat[0], w_uq[0],
                 g_kv_lat[0], w_ukv[0], lambda_q1[0], lambda_k1[0], lambda_q2[0], lambda_k2[0],
                 g_diff_sub[0], w_out[0], g_post_mix[0], g_pre_ffn[0], w_router[0], b_router[0],
                 w_gate[0], w_up[0], w_down[0], ws_gate[0], ws_up[0], ws_down[0], g_post_ffn[0])
    return out[None]
```

```python
import functools
import math

import jax
import jax.numpy as jnp
import numpy as np
from jax import lax
from jax.experimental import pallas as pl
from jax.experimental.pallas import tpu as pltpu

F32 = jnp.float32
BF16 = jnp.bfloat16
U32 = jnp.uint32
I32 = jnp.int32

D_MODEL = 2048
CHUNK = 64
ROPE_THETA = 500000.0
EPS = 1e-6
LOG2E = 1.4426950408889634

MLA_HEADS = 8
MLA_Q_RANK = 768
MLA_KV_RANK = 512
MLA_NOPE = 128
MLA_ROPE = 64
MLA_V = 128
MLA_QK_PAD = 256

DIFF_HEADS = 8
DIFF_HEAD_DIM = 64
DIFF_ROT = DIFF_HEAD_DIM // 4
DIFF_W = DIFF_HEADS * 2 * DIFF_HEAD_DIM

N_EXPERTS = 64
TOP_K = 8
N_GROUPS = 8
GROUP_SIZE = N_EXPERTS // N_GROUPS
TOPK_GROUPS = 4
D_EXPERT = 512
ROUTED_SCALE = 2.5
LAMBDA_INIT = 0.8 - 0.6 * math.exp(-0.3 * 0)

LANES = 128
SUBLANES = 8
HALF = D_MODEL // 2
ROW_TILES = HALF // LANES

TM_PROJ = 256
ATT_BQ = 4096
ATT_BK = 1024
ATT_KV_PER_Q = ATT_BQ // ATT_BK
UNIT_W = 512
V_EXT = MLA_V + 16
EXP_BLK = 256
TD = 128
TR = 512
TN_ADA = 1024

VMEM_LIMIT = 56 * 1024 * 1024

assert ROW_TILES == SUBLANES


def _cparams(sem, vmem=VMEM_LIMIT, flags=None):
    return pltpu.CompilerParams(dimension_semantics=sem, vmem_limit_bytes=vmem, flags=flags)


def _dot(a, b):
    return jnp.dot(a, b, preferred_element_type=F32)


def _dot_nt(a, b):
    return lax.dot_general(a, b, (((1,), (1,)), ((), ())), preferred_element_type=F32)


def _rms(x):
    return x * lax.rsqrt(jnp.mean(x * x, axis=-1, keepdims=True) + EPS)


def _split_bf16(x):
    hi = x.astype(BF16)
    lo = (x - hi.astype(F32)).astype(BF16)
    return hi, lo


def _const_spec(shape):
    nd = len(shape)
    return pl.BlockSpec(shape, lambda *a: (0,) * nd, pipeline_mode=pl.Buffered(1))


def _ada_kernel(c_ref, w_ref, b_ref, o_ref):
    c = c_ref[...]
    a = c * jax.nn.sigmoid(c)
    a8 = jnp.broadcast_to(a, (SUBLANES, a.shape[1]))
    a_hi, a_lo = _split_bf16(a8)
    w_hi, w_lo = _split_bf16(w_ref[...])
    r = _dot(a_hi, w_hi) + _dot(a_lo, w_hi) + _dot(a_hi, w_lo)
    o_ref[...] = r[0:1] + b_ref[...]


def _ada(c, w, b):
    d, n = w.shape
    return pl.pallas_call(
        _ada_kernel,
        grid=(n // TN_ADA,),
        in_specs=[pl.BlockSpec((1, d), lambda j: (0, 0)),
                  pl.BlockSpec((d, TN_ADA), lambda j: (0, j)),
                  pl.BlockSpec((1, TN_ADA), lambda j: (0, j))],
        out_specs=pl.BlockSpec((1, TN_ADA), lambda j: (0, j)),
        out_shape=jax.ShapeDtypeStruct((1, n), F32),
        compiler_params=_cparams(("arbitrary",)),
        name="ada_mod",
    )(c, w, b)


def _tile_lanes(t, reps):
    return jnp.concatenate([t] * reps, axis=1)


def _rope_lanes(x, c, sa, sb, half):
    n = x.shape[1]
    return x * c + pltpu.roll(x, n - half, 1) * sa + pltpu.roll(x, half, 1) * sb


OFF_LAT = MLA_Q_RANK + MLA_KV_RANK


def _inproj_kernel(x_ref, g_ref, sc_ref, sh_ref, wlat_ref, wqk_ref, wkpe_ref, wvt_ref, c_ref, sa_ref, sb_ref,
                   lat_ref, kpe_ref, dq_ref, dk_ref, dvt_ref):
    x = x_ref[...]
    h = _rms(x) * g_ref[...] * (1.0 + sc_ref[...]) + sh_ref[...]
    hb = h.astype(BF16)
    reps = DIFF_W // LANES
    c = _tile_lanes(c_ref[...], reps)
    sa = _tile_lanes(sa_ref[...], reps)
    sb = _tile_lanes(sb_ref[...], reps)
    half = DIFF_ROT // 2
    lat_ref[...] = _dot(hb, wlat_ref[...])
    q = _dot(hb, wqk_ref[:, 0:DIFF_W])
    dq_ref[...] = (_rope_lanes(q, c, sa, sb, half) * (DIFF_HEAD_DIM ** -0.5 * LOG2E)).astype(BF16)
    k = _dot(hb, wqk_ref[:, DIFF_W:])
    dk_ref[...] = _rope_lanes(k, c, sa, sb, half).astype(BF16)
    kpe_ref[...] = _dot(hb, wkpe_ref[...])
    _store_vt_ext(dvt_ref, _dot_nt(wvt_ref[...], hb), DIFF_HEADS)


def _inproj(x, g, sc, sh, w_lat, w_qk, w_kpe, wv_t, tabs):
    s, d = x.shape
    tm = min(TM_PROJ, s)
    row = lambda n: pl.BlockSpec((tm, n), lambda i: (i, 0))
    return pl.pallas_call(
        _inproj_kernel,
        grid=(s // tm,),
        in_specs=[row(d), _const_spec((1, d)), _const_spec((1, d)), _const_spec((1, d)),
                  _const_spec(w_lat.shape), _const_spec(w_qk.shape), _const_spec(w_kpe.shape),
                  _const_spec(wv_t.shape), row(LANES), row(LANES), row(LANES)],
        out_specs=[row(OFF_LAT), row(LANES), row(DIFF_W), row(DIFF_W),
                   pl.BlockSpec((DIFF_HEADS * V_EXT, tm), lambda i: (0, i))],
        out_shape=[jax.ShapeDtypeStruct((s, OFF_LAT), F32),
                   jax.ShapeDtypeStruct((s, LANES), F32),
                   jax.ShapeDtypeStruct((s, DIFF_W), BF16),
                   jax.ShapeDtypeStruct((s, DIFF_W), BF16),
                   jax.ShapeDtypeStruct((DIFF_HEADS * V_EXT, s), BF16)],
        compiler_params=_cparams(("arbitrary",)),
        name="in_proj",
    )(x, g, sc, sh, w_lat, w_qk, w_kpe, wv_t, *tabs)


def _mla_prep_kernel(lat_ref, kpe_ref, gq_ref, gkv_ref, wuq_ref, wuk_ref, wvt_ref,
                     c_ref, sa_ref, sb_ref, q_ref, k_ref, vt_ref):
    lat = lat_ref[...]
    qn = (_rms(lat[:, :MLA_Q_RANK]) * gq_ref[...]).astype(BF16)
    kvn = (_rms(lat[:, MLA_Q_RANK:]) * gkv_ref[...]).astype(BF16)
    q = _dot(qn, wuq_ref[...])
    kn = _dot(kvn, wuk_ref[...])
    c, sa, sb = c_ref[...], sa_ref[...], sb_ref[...]
    half = MLA_ROPE // 2
    qs = (MLA_NOPE + MLA_ROPE) ** -0.5 * LOG2E
    kpe = _rope_lanes(kpe_ref[...], c, sa, sb, half).astype(BF16)
    for h in range(MLA_HEADS):
        o = h * MLA_QK_PAD
        q_ref[:, o:o + LANES] = (q[:, o:o + LANES] * qs).astype(BF16)
        q_ref[:, o + LANES:o + 2 * LANES] = (
            _rope_lanes(q[:, o + LANES:o + 2 * LANES], c, sa, sb, half) * qs).astype(BF16)
        k_ref[:, o:o + LANES] = kn[:, h * MLA_NOPE:(h + 1) * MLA_NOPE].astype(BF16)
        k_ref[:, o + LANES:o + 2 * LANES] = kpe
    _store_vt_ext(vt_ref, _dot_nt(wvt_ref[...], kvn), MLA_HEADS)


def _mla_prep(lat, kpe, gq, gkv, wuq_r, wuk_r, wv_t, tabs):
    s = lat.shape[0]
    tm = min(TM_PROJ, s)
    row = lambda n: pl.BlockSpec((tm, n), lambda i: (i, 0))
    hq = MLA_HEADS * MLA_QK_PAD
    hv = MLA_HEADS * V_EXT
    return pl.pallas_call(
        _mla_prep_kernel,
        grid=(s // tm,),
        in_specs=[row(OFF_LAT), row(LANES), _const_spec(gq.shape), _const_spec(gkv.shape),
                  _const_spec(wuq_r.shape), _const_spec(wuk_r.shape), _const_spec(wv_t.shape),
                  row(LANES), row(LANES), row(LANES)],
        out_specs=[row(hq), row(hq), pl.BlockSpec((hv, tm), lambda i: (0, i))],
        out_shape=[jax.ShapeDtypeStruct((s, hq), BF16),
                   jax.ShapeDtypeStruct((s, hq), BF16),
                   jax.ShapeDtypeStruct((hv, s), BF16)],
        compiler_params=_cparams(("arbitrary",)),
        name="mla_prep",
    )(lat, kpe, gq, gkv, wuq_r, wuk_r, wv_t, *tabs)


def _pair_tables(nq):
    n = [ATT_KV_PER_Q * (i + 1) for i in range(nq)]
    qi = np.concatenate([np.full(c, i) for i, c in enumerate(n)]).astype(np.int32)
    kj = np.concatenate([np.arange(c) for c in n]).astype(np.int32)
    return jnp.asarray(qi), jnp.asarray(kj)


def _ones_rows(n):
    rows = V_EXT - MLA_V
    return (lax.broadcasted_iota(I32, (rows, n), 0) == 0).astype(BF16)


def _store_vt_ext(vt_ref, vt, heads):
    ones = _ones_rows(vt.shape[1])
    for h in range(heads):
        vt_ref[h * V_EXT:h * V_EXT + MLA_V, :] = vt[h * MLA_V:(h + 1) * MLA_V, :].astype(BF16)
        vt_ref[h * V_EXT + MLA_V:(h + 1) * V_EXT, :] = ones


def _attn_step(q_refs, k_ref, vt_ref, s_bufs, m_refs, acc_refs, rel):
    w = UNIT_W
    nk = k_ref.shape[0]
    per_kv = nk // w
    specs = []
    for u in range(q_refs[0].shape[0] // w):
        d = None if rel is None else u - rel * per_kv
        if d is None or d >= per_kv:
            specs.append((u, nk, None))
        elif d >= 0:
            specs.append((u, (d + 1) * w, d * w))
    units = [(si, sp) for sp in specs for si in range(len(q_refs))]

    def scores(n):
        si, (u, rows, off) = units[n]
        s = _dot_nt(k_ref[0:rows, :], q_refs[si][u * w:(u + 1) * w, :])
        if off is not None:
            kc = lax.broadcasted_iota(I32, (rows, w), 0) // CHUNK
            qc = (lax.broadcasted_iota(I32, (rows, w), 1) + off) // CHUNK
            s = jnp.where(kc <= qc, s, -jnp.inf)
        s_bufs[n % 2][0:rows, :] = s
        return jnp.max(s, axis=0, keepdims=True)

    cmax_next = scores(0)
    for n, (si, (u, rows, _)) in enumerate(units):
        cmax = cmax_next
        if n + 1 < len(units):
            cmax_next = scores(n + 1)
        m_ref, acc_ref = m_refs[si][u], acc_refs[si][u]
        m_prev = m_ref[...]
        m_new = jnp.maximum(m_prev, cmax)
        m_ref[...] = m_new
        acc_ref[...] = jnp.exp2(m_prev - m_new) * acc_ref[...]
        p = jnp.exp2(s_bufs[n % 2][0:rows, :] - m_new).astype(BF16)
        acc_ref[...] += _dot(vt_ref[:, 0:rows], p)


def _init_stats(ms, accs):
    for m, a in zip(ms, accs):
        m[...] = jnp.full(m.shape, -jnp.inf, F32)
        a[...] = jnp.zeros(a.shape, F32)


def _normalized(acc_ref):
    acc = acc_ref[...]
    return acc[0:MLA_V, :] / acc[MLA_V:MLA_V + 1, :]


def _attn_phases(qi_ref, kj_ref, init, step, finish):
    t = pl.program_id(1)
    rel = kj_ref[t] - ATT_KV_PER_Q * qi_ref[t]

    @pl.when(kj_ref[t] == 0)
    def _():
        init()

    @pl.when(rel < 0)
    def _():
        step(None)

    for r in range(ATT_KV_PER_Q):
        @pl.when(rel == r)
        def _(r=r):
            step(r)
            if r == ATT_KV_PER_Q - 1:
                finish()


def _unit_scratch(n_streams):
    nu = ATT_BQ // UNIT_W
    return ([pltpu.VMEM((ATT_BK, UNIT_W), F32)] * 2
            + [pltpu.VMEM((1, UNIT_W), F32)] * (nu * n_streams)
            + [pltpu.VMEM((V_EXT, UNIT_W), F32)] * (nu * n_streams))


def _split_units(scr, n_streams):
    nu = ATT_BQ // UNIT_W
    ms = [list(scr[s * nu:(s + 1) * nu]) for s in range(n_streams)]
    accs = [list(scr[(n_streams + s) * nu:(n_streams + s + 1) * nu]) for s in range(n_streams)]
    return ms, accs


def _mla_attn_kernel(qi_ref, kj_ref, q_ref, k_ref, vt_ref, o_ref, *scr):
    s_bufs, (ms, accs) = scr[:2], _split_units(scr[2:], 1)

    def finish():
        for u, a in enumerate(accs[0]):
            o_ref[u * UNIT_W:(u + 1) * UNIT_W, :] = _normalized(a).T.astype(o_ref.dtype)

    _attn_phases(qi_ref, kj_ref,
                 lambda: _init_stats(ms[0], accs[0]),
                 lambda rel: _attn_step([q_ref], k_ref, vt_ref, s_bufs, ms, accs, rel),
                 finish)


def _mla_attn(qc, kc, vt):
    s = qc.shape[0]
    assert s % ATT_BQ == 0
    qi, kj = _pair_tables(s // ATT_BQ)
    gs = pltpu.PrefetchScalarGridSpec(
        num_scalar_prefetch=2,
        grid=(MLA_HEADS, qi.shape[0]),
        in_specs=[pl.BlockSpec((ATT_BQ, MLA_QK_PAD), lambda h, t, qi, kj: (qi[t], h)),
                  pl.BlockSpec((ATT_BK, MLA_QK_PAD), lambda h, t, qi, kj: (kj[t], h)),
                  pl.BlockSpec((V_EXT, ATT_BK), lambda h, t, qi, kj: (h, kj[t]))],
        out_specs=pl.BlockSpec((ATT_BQ, MLA_V), lambda h, t, qi, kj: (qi[t], h)),
        scratch_shapes=_unit_scratch(1),
    )
    return pl.pallas_call(
        _mla_attn_kernel,
        grid_spec=gs,
        out_shape=jax.ShapeDtypeStruct((s, MLA_HEADS * MLA_V), BF16),
        compiler_params=_cparams(("arbitrary", "arbitrary")),
        name="mla_attn",
    )(qi, kj, qc, kc, vt)


def _diff_attn_kernel(qi_ref, kj_ref, q_ref, k_ref, vt_ref, lq1_ref, lk1_ref, lq2_ref, lk2_ref,
                      g_ref, o_ref, q0_scr, q1_scr, *scr):
    s_bufs, (ms, accs) = scr[:2], _split_units(scr[2:], 2)

    def init():
        q = q_ref[...]
        lane = lax.broadcasted_iota(I32, q.shape, 1)
        zero = jnp.zeros_like(q)
        q0_scr[...] = jnp.where(lane < DIFF_HEAD_DIM, q, zero)
        q1_scr[...] = jnp.where(lane >= DIFF_HEAD_DIM, q, zero)
        _init_stats(ms[0] + ms[1], accs[0] + accs[1])

    def finish():
        lam = (jnp.exp(jnp.sum(lq1_ref[...] * lk1_ref[...], axis=1, keepdims=True))
               - jnp.exp(jnp.sum(lq2_ref[...] * lk2_ref[...], axis=1, keepdims=True))
               + LAMBDA_INIT)
        for u, (a0, a1) in enumerate(zip(accs[0], accs[1])):
            o = _normalized(a0) - lam * _normalized(a1)
            o = o * lax.rsqrt(jnp.mean(o * o, axis=0, keepdims=True) + EPS)
            o_ref[u * UNIT_W:(u + 1) * UNIT_W, :] = (
                o * g_ref[...] * (1.0 - LAMBDA_INIT)).T.astype(o_ref.dtype)

    _attn_phases(qi_ref, kj_ref, init,
                 lambda rel: _attn_step([q0_scr, q1_scr], k_ref, vt_ref, s_bufs, ms, accs, rel),
                 finish)


def _diff_attn(dq, dk, dvt, lq1, lk1, lq2, lk2, g_col):
    s = dq.shape[0]
    assert s % ATT_BQ == 0
    qi, kj = _pair_tables(s // ATT_BQ)
    hw = 2 * DIFF_HEAD_DIM
    small = lambda a: pl.BlockSpec(a.shape, lambda h, t, qi, kj: (0, 0))
    gs = pltpu.PrefetchScalarGridSpec(
        num_scalar_prefetch=2,
        grid=(DIFF_HEADS, qi.shape[0]),
        in_specs=[pl.BlockSpec((ATT_BQ, hw), lambda h, t, qi, kj: (qi[t], h)),
                  pl.BlockSpec((ATT_BK, hw), lambda h, t, qi, kj: (kj[t], h)),
                  pl.BlockSpec((V_EXT, ATT_BK), lambda h, t, qi, kj: (h, kj[t])),
                  small(lq1), small(lk1), small(lq2), small(lk2), small(g_col)],
        out_specs=pl.BlockSpec((ATT_BQ, hw), lambda h, t, qi, kj: (qi[t], h)),
        scratch_shapes=[pltpu.VMEM((ATT_BQ, hw), BF16), pltpu.VMEM((ATT_BQ, hw), BF16)] + _unit_scratch(2),
    )
    return pl.pallas_call(
        _diff_attn_kernel,
        grid_spec=gs,
        out_shape=jax.ShapeDtypeStruct((s, DIFF_W), BF16),
        compiler_params=_cparams(("arbitrary", "arbitrary")),
        name="diff_attn",
    )(qi, kj, dq, dk, dvt, lq1, lk1, lq2, lk2, g_col)


def _pack_words(y):
    lo = pltpu.bitcast(y[:, :HALF].astype(BF16).astype(F32), U32) >> 16
    hi = pltpu.bitcast(y[:, HALF:].astype(BF16).astype(F32), U32) & jnp.uint32(0xFFFF0000)
    return lo | hi


def _store_packed(ref, words, rows):
    for s in range(ROW_TILES):
        ref[pl.ds(s, rows, stride=ROW_TILES), :] = words[:, s * LANES:(s + 1) * LANES]


def _load_packed(ref, rows):
    return jnp.concatenate(
        [ref[pl.ds(s, rows, stride=ROW_TILES), :] for s in range(ROW_TILES)], axis=1)


def _unpack_words(w):
    lo = pltpu.bitcast(w << 16, F32)
    hi = pltpu.bitcast(w & jnp.uint32(0xFFFF0000), F32)
    return lo, hi


def _outproj_kernel(om_ref, od_ref, x_ref, w_ref, gt_ref, gpost_ref, gpre_ref, sc_ref, sh_ref,
                    wr_ref, x1_ref, h2_ref, h2p_ref, lg_ref):
    y = _dot(jnp.concatenate([om_ref[...], od_ref[...]], axis=1), w_ref[...])
    x1 = x_ref[...] + gt_ref[...] * (_rms(y) * gpost_ref[...])
    x1_ref[...] = x1
    h2 = _rms(x1) * gpre_ref[...] * (1.0 + sc_ref[...]) + sh_ref[...]
    h_hi, h_lo = _split_bf16(h2)
    h2_ref[...] = h_hi
    _store_packed(h2p_ref, _pack_words(h2), h2.shape[0])
    w_hi, w_lo = _split_bf16(wr_ref[...])
    ne = w_hi.shape[0]
    a = _dot_nt(jnp.concatenate([w_hi, w_lo], axis=0), h_hi)
    lg_ref[...] = a[0:ne] + a[ne:] + _dot_nt(w_hi, h_lo)


def _outproj(om, od, x, w_out, gt, gpost, gpre, sc, sh, wr_t):
    s, d = x.shape
    tm = min(TM_PROJ, s)
    row = lambda n: pl.BlockSpec((tm, n), lambda i: (i, 0))
    vec = _const_spec((1, d))
    return pl.pallas_call(
        _outproj_kernel,
        grid=(s // tm,),
        in_specs=[row(om.shape[1]), row(od.shape[1]), row(d), _const_spec(w_out.shape),
                  vec, vec, vec, vec, vec, _const_spec(wr_t.shape)],
        out_specs=[row(d), row(d),
                   pl.BlockSpec((tm * ROW_TILES, LANES), lambda i: (i, 0)),
                   pl.BlockSpec((N_EXPERTS, tm), lambda i: (0, i))],
        out_shape=[jax.ShapeDtypeStruct((s, d), F32),
                   jax.ShapeDtypeStruct((s, d), BF16),
                   jax.ShapeDtypeStruct((s * ROW_TILES, LANES), U32),
                   jax.ShapeDtypeStruct((N_EXPERTS, s), F32)],
        compiler_params=_cparams(("arbitrary",)),
        name="out_proj",
    )(om, od, x, w_out, gt, gpost, gpre, sc, sh, wr_t)


def _route_kernel(lg_ref, b_ref, eidx_ref, w_ref, rank_ref, cnt_ref, carry):
    step = pl.program_id(0)

    @pl.when(step == 0)
    def _():
        carry[...] = jnp.zeros(carry.shape, F32)

    tr = lg_ref.shape[1]
    ninf = jnp.float32(-jnp.inf)
    ie = lax.broadcasted_iota(I32, (GROUP_SIZE, tr), 0)
    rmax = lambda a: jnp.max(a, axis=0, keepdims=True)
    rmin = lambda a: jnp.min(a, axis=0, keepdims=True)
    rsum = lambda a: jnp.sum(a, axis=0, keepdims=True)

    sc, bi, gscore = [], [], []
    for g in range(N_GROUPS):
        lg = lg_ref[g * GROUP_SIZE:(g + 1) * GROUP_SIZE, :]
        s = jax.nn.sigmoid(lg)
        b = s + b_ref[g * GROUP_SIZE:(g + 1) * GROUP_SIZE, :]
        m1 = rmax(b)
        i1 = rmin(jnp.where(b == m1, ie, GROUP_SIZE))
        m2 = rmax(jnp.where(ie == i1, ninf, b))
        sc.append(s)
        bi.append(b)
        gscore.append(m1 + m2)

    gsel = [jnp.zeros((1, tr), I32) for _ in range(N_GROUPS)]
    for _ in range(TOPK_GROUPS):
        gm = functools.reduce(jnp.maximum, gscore)
        gi = functools.reduce(
            jnp.minimum, [jnp.where(gscore[g] == gm, g, N_GROUPS) for g in range(N_GROUPS)])
        for g in range(N_GROUPS):
            hit = gi == g
            gsel[g] = jnp.where(hit, 1, gsel[g])
            gscore[g] = jnp.where(hit, ninf, gscore[g])

    masked = [jnp.where(jnp.broadcast_to(gsel[g], (GROUP_SIZE, tr)) > 0, bi[g], ninf)
              for g in range(N_GROUPS)]
    sel = [jnp.zeros((GROUP_SIZE, tr), F32) for _ in range(N_GROUPS)]
    idxs, svals = [], []
    for _ in range(TOP_K):
        m = functools.reduce(jnp.maximum, [rmax(x) for x in masked])
        idx = functools.reduce(
            jnp.minimum,
            [rmin(jnp.where(masked[g] == m, ie + g * GROUP_SIZE, N_EXPERTS)) for g in range(N_GROUPS)])
        sv = jnp.zeros((1, tr), F32)
        for g in range(N_GROUPS):
            hit = (ie + g * GROUP_SIZE) == idx
            sv = sv + rsum(jnp.where(hit, sc[g], 0.0))
            masked[g] = jnp.where(hit, ninf, masked[g])
            sel[g] = jnp.where(hit, 1.0, sel[g])
        idxs.append(idx)
        svals.append(sv)
    tot = functools.reduce(lambda a, b: a + b, svals)

    sel2 = jnp.concatenate(sel, axis=0)
    before = (lax.broadcasted_iota(I32, (tr, tr), 0) < lax.broadcasted_iota(I32, (tr, tr), 1))
    rank2 = _dot(sel2.astype(BF16), before.astype(BF16)) + carry[:, 0:1]
    for k in range(TOP_K):
        rk = jnp.zeros((1, tr), F32)
        for g in range(N_GROUPS):
            hit = (ie + g * GROUP_SIZE) == idxs[k]
            rk = rk + rsum(jnp.where(hit, rank2[g * GROUP_SIZE:(g + 1) * GROUP_SIZE, :], 0.0))
        eidx_ref[k:k + 1, :] = idxs[k]
        w_ref[k:k + 1, :] = svals[k] / tot * ROUTED_SCALE
        rank_ref[k:k + 1, :] = rk.astype(I32)
    carry[...] = carry[...] + jnp.sum(sel2, axis=1, keepdims=True)
    cnt_ref[...] = carry[...]


def _route(lg_t, b_col):
    e, t = lg_t.shape
    tr = min(TR, t)
    tok = lambda: pl.BlockSpec((TOP_K, tr), lambda i: (0, i))
    return pl.pallas_call(
        _route_kernel,
        grid=(t // tr,),
        in_specs=[pl.BlockSpec((e, tr), lambda i: (0, i)), pl.BlockSpec((e, 1), lambda i: (0, 0))],
        out_specs=[tok(), tok(), tok(), pl.BlockSpec((e, LANES), lambda i: (0, 0))],
        out_shape=[jax.ShapeDtypeStruct((TOP_K, t), I32),
                   jax.ShapeDtypeStruct((TOP_K, t), F32),
                   jax.ShapeDtypeStruct((TOP_K, t), I32),
                   jax.ShapeDtypeStruct((e, LANES), F32)],
        scratch_shapes=[pltpu.VMEM((e, LANES), F32)],
        compiler_params=_cparams(("arbitrary",)),
        name="route",
    )(lg_t, b_col)


def _pos_kernel(pstart_ref, eidx_ref, rank_ref, pos_ref):
    e = eidx_ref[...]
    pos = rank_ref[...]
    for x in range(N_EXPERTS):
        pos = pos + jnp.where(e == x, pstart_ref[x], 0)
    pos_ref[...] = pos


def _positions(pstart, eidx, rank):
    k, t = eidx.shape
    tr = min(TR, t)
    gs = pltpu.PrefetchScalarGridSpec(
        num_scalar_prefetch=1,
        grid=(t // tr,),
        in_specs=[pl.BlockSpec((k, tr), lambda i, p: (0, i)), pl.BlockSpec((k, tr), lambda i, p: (0, i))],
        out_specs=pl.BlockSpec((k, tr), lambda i, p: (0, i)),
    )
    return pl.pallas_call(
        _pos_kernel, grid_spec=gs,
        out_shape=jax.ShapeDtypeStruct((k, t), I32),
        compiler_params=_cparams(("arbitrary",)),
        name="positions",
    )(pstart, eidx, rank)


def _row_copy(src, src_row, dst, dst_row, sem):
    return pltpu.make_async_copy(src.at[pl.ds(src_row * ROW_TILES, ROW_TILES)],
                                 dst.at[pl.ds(dst_row * ROW_TILES, ROW_TILES)], sem)


FILL_CHUNKS = (128, 64, 32, 16, 8, 4, 2, 1)


def _zero_fill(fs_ref, fl_ref, tail_ref, zbuf, xs_ref, sem, wait):
    def chunk(row, n):
        return pltpu.make_async_copy(zbuf.at[pl.ds(0, n * ROW_TILES)],
                                     xs_ref.at[pl.ds(row * ROW_TILES, n * ROW_TILES)], sem)

    def go(cp):
        if wait:
            cp.wait()
        else:
            cp.start()

    def per_expert(e, c):
        pad = fl_ref[e]
        row = fs_ref[e]
        for n in FILL_CHUNKS:
            @pl.when((pad & n) != 0)
            def _(row=row, n=n):
                go(chunk(row, n))
            row = row + (pad & n)
        return c

    lax.fori_loop(0, N_EXPERTS, per_expert, 0)

    def per_tail(b, c):
        for part in range(EXP_BLK // FILL_CHUNKS[0]):
            go(chunk(b * EXP_BLK + part * FILL_CHUNKS[0], FILL_CHUNKS[0]))
        return c

    lax.fori_loop(tail_ref[0], tail_ref[1], per_tail, 0)


def _dispatch_kernel(fs_ref, fl_ref, tail_ref, pos_ref, h_ref, xs_ref, zbuf, sem, fill_sem):
    td = pos_ref.shape[1]
    step = pl.program_id(0)

    @pl.when(step == 0)
    def _():
        zbuf[...] = jnp.zeros(zbuf.shape, zbuf.dtype)
        _zero_fill(fs_ref, fl_ref, tail_ref, zbuf, xs_ref, fill_sem, wait=False)

    def issue(t, c):
        for k in range(TOP_K):
            _row_copy(h_ref, t, xs_ref, pos_ref[k, t], sem).start(priority=k % 2)
        return c

    lax.fori_loop(0, td, issue, 0, unroll=4)
    for k in range(TOP_K):
        pltpu.make_async_copy(h_ref, xs_ref.at[pl.ds(0, td * ROW_TILES)], sem).wait()

    @pl.when(step == pl.num_programs(0) - 1)
    def _():
        _zero_fill(fs_ref, fl_ref, tail_ref, zbuf, xs_ref, fill_sem, wait=True)


def _dispatch(fill_start, fill_len, tail, pos, h2p, m_pad):
    k, t = pos.shape
    td = min(TD, t)
    gs = pltpu.PrefetchScalarGridSpec(
        num_scalar_prefetch=3,
        grid=(t // td,),
        in_specs=[pl.BlockSpec((k, td), lambda i, *_: (0, i), memory_space=pltpu.SMEM),
                  pl.BlockSpec((td * ROW_TILES, LANES), lambda i, *_: (i, 0))],
        out_specs=pl.BlockSpec(memory_space=pl.ANY),
        scratch_shapes=[pltpu.VMEM((FILL_CHUNKS[0] * ROW_TILES, LANES), U32),
                        pltpu.SemaphoreType.DMA, pltpu.SemaphoreType.DMA],
    )
    return pl.pallas_call(
        _dispatch_kernel,
        grid_spec=gs,
        out_shape=jax.ShapeDtypeStruct((m_pad * ROW_TILES, LANES), U32),
        compiler_params=_cparams(("arbitrary",)),
        name="dispatch",
    )(fill_start, fill_len, tail, pos, h2p)


def _weight_copies(w_hbm, w_buf, sem, e, s):
    return [pltpu.make_async_copy(h.at[e], b.at[s], sem.at[s]) for h, b in zip(w_hbm, w_buf)]


def _experts_kernel(be_ref, nu_ref, fresh_ref, nxt_ref, slot_ref, x_ref, wg_hbm, wu_hbm, wd_hbm, y_ref,
                    wg_buf, wu_buf, wd_buf, wg_s, wu_s, wd_s, sem):
    i = pl.program_id(0)
    used = i < nu_ref[0]
    w_hbm = (wg_hbm, wu_hbm, wd_hbm)
    w_buf = (wg_buf, wu_buf, wd_buf)

    @pl.when(jnp.logical_and(used, fresh_ref[i] == 1))
    def _():
        s = slot_ref[i]

        @pl.when(i == 0)
        def _():
            for c in _weight_copies(w_hbm, w_buf, sem, be_ref[0], 0):
                c.start()

        for c in _weight_copies(w_hbm, w_buf, sem, be_ref[i], s):
            c.wait()

        @pl.when(nxt_ref[i] >= 0)
        def _():
            for c in _weight_copies(w_hbm, w_buf, sem, nxt_ref[i], 1 - s):
                c.start()

        wg_s[...] = wg_buf[s].astype(BF16)
        wu_s[...] = wu_buf[s].astype(BF16)
        wd_s[...] = wd_buf[s].astype(BF16)

    @pl.when(jnp.logical_not(used))
    def _():
        y_ref[...] = jnp.zeros(y_ref.shape, y_ref.dtype)

    @pl.when(used)
    def _():
        rows = x_ref.shape[0] // ROW_TILES
        xa, xb = _unpack_words(_load_packed(x_ref, rows))
        xa = xa.astype(BF16)
        xb = xb.astype(BF16)
        g = _dot(xa, wg_s[0:HALF, :]) + _dot(xb, wg_s[HALF:, :])
        u = _dot(xa, wu_s[0:HALF, :]) + _dot(xb, wu_s[HALF:, :])
        a = (g * jax.nn.sigmoid(g) * u).astype(BF16)
        y = _dot(a, wd_s[...])
        _store_packed(y_ref, _pack_words(y), rows)


def _experts(blk_e, nused, fresh, nxt_e, slot, xs, w_gate, w_up, w_down, blk):
    n_blk = blk_e.shape[0]
    ne, d, de = w_gate.shape
    xmap = lambda i, be, nu, fr, nx, sl: (jnp.minimum(i, nu[0] - 1), 0)
    hbm = pl.BlockSpec(memory_space=pl.ANY)
    gs = pltpu.PrefetchScalarGridSpec(
        num_scalar_prefetch=5,
        grid=(n_blk,),
        in_specs=[pl.BlockSpec((blk * ROW_TILES, LANES), xmap), hbm, hbm, hbm],
        out_specs=pl.BlockSpec((blk * ROW_TILES, LANES), lambda i, *_: (i, 0)),
        scratch_shapes=[pltpu.VMEM((2, d, de), F32), pltpu.VMEM((2, d, de), F32), pltpu.VMEM((2, de, d), F32),
                        pltpu.VMEM((d, de), BF16), pltpu.VMEM((d, de), BF16), pltpu.VMEM((de, d), BF16),
                        pltpu.SemaphoreType.DMA((2,))],
    )
    return pl.pallas_call(
        _experts_kernel, grid_spec=gs,
        out_shape=jax.ShapeDtypeStruct(xs.shape, U32),
        compiler_params=_cparams(("arbitrary",)),
        name="experts",
    )(blk_e, nused, fresh, nxt_e, slot, xs, w_gate, w_up, w_down)


def _combine_kernel(pos_ref, wt_ref, ys_ref, h_ref, x1_ref, wsg_ref, wsu_ref, wsd_ref,
                    gt_ref, gpost_ref, o_ref, gbuf, sem):
    td = pos_ref.shape[1]

    def issue(t, c):
        for k in range(TOP_K):
            _row_copy(ys_ref, pos_ref[k, t], gbuf.at[k], t, sem).start(priority=k % 2)
        return c

    lax.fori_loop(0, td, issue, 0, unroll=4)

    hb = h_ref[...]
    g = _dot(hb, wsg_ref[...])
    u = _dot(hb, wsu_ref[...])
    y = _dot((g * jax.nn.sigmoid(g) * u).astype(BF16), wsd_ref[...])

    wpad = jnp.concatenate([wt_ref[...], jnp.zeros((LANES - TOP_K, td), F32)], axis=0)
    wcol = wpad.T

    for k in range(TOP_K):
        pltpu.make_async_copy(ys_ref.at[pl.ds(0, td * ROW_TILES)], gbuf.at[k], sem).wait()
    lo = jnp.zeros((td, HALF), F32)
    hi = jnp.zeros((td, HALF), F32)
    for k in range(TOP_K):
        a, b = _unpack_words(_load_packed(gbuf.at[k], td))
        wk = wcol[:, k:k + 1]
        lo = lo + wk * a
        hi = hi + wk * b
    y = y + jnp.concatenate([lo, hi], axis=1)
    o_ref[...] = x1_ref[...] + gt_ref[...] * (_rms(y) * gpost_ref[...])


def _combine(pos, wts, ys, h2, x1, wsg, wsu, wsd, gt, gpost):
    k, t = pos.shape
    d = x1.shape[1]
    td = min(TD, t)
    assert td == LANES
    row = lambda n: pl.BlockSpec((td, n), lambda i: (i, 0))
    vec = _const_spec((1, d))
    return pl.pallas_call(
        _combine_kernel,
        grid=(t // td,),
        in_specs=[pl.BlockSpec((k, td), lambda i: (0, i), memory_space=pltpu.SMEM),
                  pl.BlockSpec((k, td), lambda i: (0, i)),
                  pl.BlockSpec(memory_space=pl.ANY),
                  row(d), row(d), _const_spec(wsg.shape), _const_spec(wsu.shape), _const_spec(wsd.shape),
                  vec, vec],
        out_specs=row(d),
        out_shape=jax.ShapeDtypeStruct((t, d), F32),
        scratch_shapes=[pltpu.VMEM((k, td * ROW_TILES, LANES), U32), pltpu.SemaphoreType.DMA],
        compiler_params=_cparams(("arbitrary",)),
        name="combine",
    )(pos, wts, ys, h2, x1, wsg, wsu, wsd, gt, gpost)


def _rope_tables(seq, dim):
    pos = jnp.arange(seq, dtype=F32)
    inv = ROPE_THETA ** (-jnp.arange(0, dim, 2, dtype=F32) / dim)
    ang = pos[:, None] * inv[None, :]
    return jnp.cos(ang), jnp.sin(ang)


def _lane_tables(seq, dim, period):
    cos, sin = _rope_tables(seq, dim)
    half = dim // 2
    cos = jnp.tile(cos, (1, LANES // half))
    sin = jnp.tile(sin, (1, LANES // half))
    r = jnp.arange(LANES) % period
    c = jnp.where(r < dim, cos, 1.0)
    sa = jnp.where(r < half, -sin, 0.0)
    sb = jnp.where((r >= half) & (r < dim), sin, 0.0)
    return c, sa, sb


def _moe(h2, h2p, x1, lg_t, b_router, w_gate, w_up, w_down, wsg, wsu, wsd, gt_f, g_post_ffn):
    t = h2.shape[0]
    blk = EXP_BLK
    m_pad = t * TOP_K + N_EXPERTS * blk
    eidx, wts, rank, cnt = _route(lg_t, b_router.reshape(N_EXPERTS, 1))
    counts = cnt[:, 0].astype(I32)
    padded = (counts + blk - 1) // blk * blk
    pend = jnp.cumsum(padded)
    pstart = pend - padded
    n_blk = m_pad // blk
    blk_start = jnp.arange(n_blk, dtype=I32) * blk
    blk_e = jnp.minimum(jnp.sum(pend[None, :] <= blk_start[:, None], axis=1), N_EXPERTS - 1).astype(I32)
    nused = (pend[-1:] // blk).astype(I32)
    pos = _positions(pstart.astype(I32), eidx, rank)
    tail = jnp.concatenate([nused, jnp.full((1,), n_blk, I32)])
    xs = _dispatch((pstart + counts).astype(I32), (padded - counts).astype(I32), tail, pos, h2p, m_pad)
    idx = jnp.arange(n_blk, dtype=I32)
    fresh = (idx < nused[0]) & (blk_e != jnp.concatenate([jnp.full((1,), -1, I32), blk_e[:-1]]))
    slot = ((jnp.cumsum(fresh.astype(I32)) - 1) % 2).astype(I32)
    first_at_or_after = lax.cummin(jnp.where(fresh, idx, n_blk)[::-1])[::-1]
    nxt_idx = jnp.concatenate([first_at_or_after[1:], jnp.full((1,), n_blk, I32)])
    nxt_e = jnp.where(nxt_idx < n_blk, blk_e[jnp.minimum(nxt_idx, n_blk - 1)], -1).astype(I32)
    ys = _experts(blk_e, nused, fresh.astype(I32), nxt_e, slot, xs, w_gate, w_up, w_down, blk)
    return _combine(pos, wts, ys, h2, x1, wsg, wsu, wsd, gt_f, g_post_ffn)


def _layer(x, c, w_ada, b_ada, g_pre_mix, w_in, g_q_lat, w_uq, g_kv_lat, w_ukv,
           lq1, lk1, lq2, lk2, g_diff_sub, w_out, g_post_mix, g_pre_ffn, w_router, b_router,
           w_gate, w_up, w_down, ws_gate, ws_up, ws_down, g_post_ffn):
    s, d = x.shape
    row = lambda a: a.reshape(1, -1)

    mod = _ada(c, w_ada, row(b_ada))
    sh_a, sc_a, gt_a, sh_f, sc_f, gt_f = [mod[:, i * d:(i + 1) * d] for i in range(6)]

    o1 = MLA_Q_RANK + MLA_KV_RANK
    o2 = o1 + MLA_ROPE
    o3 = o2 + 2 * DIFF_W
    w_kpe = jnp.pad(w_in[:, o1:o2], ((0, 0), (0, LANES - MLA_ROPE))).astype(BF16)
    wv_t = lax.optimization_barrier(w_in[:, o3:]).T.astype(BF16)
    lat, kpe, dq, dk, dvt = _inproj(x, row(g_pre_mix), sc_a, sh_a, w_in[:, :o1].astype(BF16),
                                    w_in[:, o2:o3].astype(BF16), w_kpe, wv_t,
                                    _lane_tables(s, DIFF_ROT, DIFF_HEAD_DIM))

    wq = w_uq.reshape(MLA_Q_RANK, MLA_HEADS, MLA_NOPE + MLA_ROPE)
    wq = jnp.pad(wq, ((0, 0), (0, 0), (0, MLA_QK_PAD - MLA_NOPE - MLA_ROPE)))
    wuq_r = wq.reshape(MLA_Q_RANK, MLA_HEADS * MLA_QK_PAD).astype(BF16)
    wkv = w_ukv.reshape(MLA_KV_RANK, MLA_HEADS, MLA_NOPE + MLA_V)
    wuk_r = wkv[:, :, :MLA_NOPE].reshape(MLA_KV_RANK, -1).astype(BF16)
    wv_t = wkv[:, :, MLA_NOPE:].reshape(MLA_KV_RANK, -1).T.astype(BF16)
    qc, kc, vt = _mla_prep(lat, kpe, row(g_q_lat), row(g_kv_lat), wuq_r, wuk_r, wv_t,
                           _lane_tables(s, MLA_ROPE, MLA_ROPE))

    o_mla = _mla_attn(qc, kc, vt)
    o_diff = _diff_attn(dq, dk, dvt, row(lq1), row(lk1), row(lq2), row(lk2), g_diff_sub.reshape(-1, 1))

    x1, h2, h2p, lg_t = _outproj(o_mla, o_diff, x, w_out.astype(BF16), gt_a, row(g_post_mix),
                                 row(g_pre_ffn), sc_f, sh_f, w_router.T)
    return _moe(h2, h2p, x1, lg_t, b_router, w_gate, w_up, w_down,
                ws_gate.astype(BF16), ws_up.astype(BF16), ws_down.astype(BF16), gt_f, row(g_post_ffn))


def kernel(x, c, w_ada, b_ada, g_pre_mix, w_in, g_q_lat, w_uq, g_kv_lat, w_ukv, lambda_q1, lambda_k1, lambda_q2, lambda_k2, g_diff_sub, w_out, g_post_mix, g_pre_ffn, w_router, b_router, w_gate, w_up, w_down, ws_gate, ws_up, ws_down, g_post_ffn):
    assert x.shape[0] == 1 and w_ada.shape[0] == 1
    out = _layer(x[0], c, w_ada[0], b_ada[0], g_pre_mix[0], w_in[0], g_q_lat[0], w_uq[0],
                 g_kv_lat[0], w_ukv[0], lambda_q1[0], lambda_k1[0], lambda_q2[0], lambda_k2[0],
                 g_diff_sub[0], w_out[0], g_post_mix[0], g_pre_ffn[0], w_router[0], b_router[0],
                 w_gate[0], w_up[0], w_down[0], ws_gate[0], ws_up[0], ws_down[0], g_post_ffn[0])
    return out[None]
```

```python
import functools
import math

import jax
import jax.numpy as jnp
import numpy as np
from jax import lax
from jax.experimental import pallas as pl
from jax.experimental.pallas import tpu as pltpu

F32 = jnp.float32
BF16 = jnp.bfloat16
U32 = jnp.uint32
I32 = jnp.int32

D_MODEL = 2048
CHUNK = 64
ROPE_THETA = 500000.0
EPS = 1e-6
LOG2E = 1.4426950408889634

MLA_HEADS = 8
MLA_Q_RANK = 768
MLA_KV_RANK = 512
MLA_NOPE = 128
MLA_ROPE = 64
MLA_V = 128
MLA_QK_PAD = 256

DIFF_HEADS = 8
DIFF_HEAD_DIM = 64
DIFF_ROT = DIFF_HEAD_DIM // 4
DIFF_W = DIFF_HEADS * 2 * DIFF_HEAD_DIM

N_EXPERTS = 64
TOP_K = 8
N_GROUPS = 8
GROUP_SIZE = N_EXPERTS // N_GROUPS
TOPK_GROUPS = 4
D_EXPERT = 512
ROUTED_SCALE = 2.5
LAMBDA_INIT = 0.8 - 0.6 * math.exp(-0.3 * 0)

LANES = 128
SUBLANES = 8
HALF = D_MODEL // 2
ROW_TILES = HALF // LANES

TM_PROJ = 256
TM_OUT = 2 * TM_PROJ
ATT_BQ = 4096
ATT_BK = 1024
ATT_KV_PER_Q = ATT_BQ // ATT_BK
UNIT_W = 512
V_EXT = MLA_V + 16
EXP_BLK = 256
BLKS_PER_STEP = 2
TD = 128
TR = 512
TN_ADA = 1024

VMEM_LIMIT = 56 * 1024 * 1024

assert ROW_TILES == SUBLANES


def _cparams(sem, vmem=VMEM_LIMIT, flags=None):
    return pltpu.CompilerParams(dimension_semantics=sem, vmem_limit_bytes=vmem, flags=flags)


def _dot(a, b):
    return jnp.dot(a, b, preferred_element_type=F32)


def _dot_nt(a, b):
    return lax.dot_general(a, b, (((1,), (1,)), ((), ())), preferred_element_type=F32)


def _rms(x):
    return x * lax.rsqrt(jnp.mean(x * x, axis=-1, keepdims=True) + EPS)


def _split_bf16(x):
    hi = x.astype(BF16)
    lo = (x - hi.astype(F32)).astype(BF16)
    return hi, lo


def _const_spec(shape):
    nd = len(shape)
    return pl.BlockSpec(shape, lambda *a: (0,) * nd, pipeline_mode=pl.Buffered(1))


def _ada_kernel(c_ref, w_ref, b_ref, o_ref):
    c = c_ref[...]
    a = c * jax.nn.sigmoid(c)
    a8 = jnp.broadcast_to(a, (SUBLANES, a.shape[1]))
    a_hi, a_lo = _split_bf16(a8)
    w_hi, w_lo = _split_bf16(w_ref[...])
    r = _dot(a_hi, w_hi) + _dot(a_lo, w_hi) + _dot(a_hi, w_lo)
    o_ref[...] = r[0:1] + b_ref[...]


def _ada(c, w, b):
    d, n = w.shape
    return pl.pallas_call(
        _ada_kernel,
        grid=(n // TN_ADA,),
        in_specs=[pl.BlockSpec((1, d), lambda j: (0, 0)),
                  pl.BlockSpec((d, TN_ADA), lambda j: (0, j)),
                  pl.BlockSpec((1, TN_ADA), lambda j: (0, j))],
        out_specs=pl.BlockSpec((1, TN_ADA), lambda j: (0, j)),
        out_shape=jax.ShapeDtypeStruct((1, n), F32),
        compiler_params=_cparams(("arbitrary",)),
        name="ada_mod",
    )(c, w, b)


def _tile_lanes(t, reps):
    return jnp.concatenate([t] * reps, axis=1)


def _rope_lanes(x, c, sa, sb, half):
    n = x.shape[1]
    return x * c + pltpu.roll(x, n - half, 1) * sa + pltpu.roll(x, half, 1) * sb


OFF_LAT = MLA_Q_RANK + MLA_KV_RANK


def _inproj_kernel(x_ref, g_ref, sc_ref, sh_ref, wlat_ref, wqk_ref, wkpe_ref, wvt_ref, c_ref, sa_ref, sb_ref,
                   lat_ref, kpe_ref, dq_ref, dk_ref, dvt_ref):
    x = x_ref[...]
    h = _rms(x) * g_ref[...] * (1.0 + sc_ref[...]) + sh_ref[...]
    hb = h.astype(BF16)
    reps = DIFF_W // LANES
    c = _tile_lanes(c_ref[...], reps)
    sa = _tile_lanes(sa_ref[...], reps)
    sb = _tile_lanes(sb_ref[...], reps)
    half = DIFF_ROT // 2
    lat_ref[...] = _dot(hb, wlat_ref[...])
    q = _dot(hb, wqk_ref[:, 0:DIFF_W])
    dq_ref[...] = (_rope_lanes(q, c, sa, sb, half) * (DIFF_HEAD_DIM ** -0.5 * LOG2E)).astype(BF16)
    k = _dot(hb, wqk_ref[:, DIFF_W:])
    dk_ref[...] = _rope_lanes(k, c, sa, sb, half).astype(BF16)
    kpe_ref[...] = _dot(hb, wkpe_ref[...])
    _store_vt_ext(dvt_ref, _dot_nt(wvt_ref[...], hb), DIFF_HEADS)


def _inproj(x, g, sc, sh, w_lat, w_qk, w_kpe, wv_t, tabs):
    s, d = x.shape
    tm = min(TM_PROJ, s)
    row = lambda n: pl.BlockSpec((tm, n), lambda i: (i, 0))
    return pl.pallas_call(
        _inproj_kernel,
        grid=(s // tm,),
        in_specs=[row(d), _const_spec((1, d)), _const_spec((1, d)), _const_spec((1, d)),
                  _const_spec(w_lat.shape), _const_spec(w_qk.shape), _const_spec(w_kpe.shape),
                  _const_spec(wv_t.shape), row(LANES), row(LANES), row(LANES)],
        out_specs=[row(OFF_LAT), row(LANES), row(DIFF_W), row(DIFF_W),
                   pl.BlockSpec((DIFF_HEADS * V_EXT, tm), lambda i: (0, i))],
        out_shape=[jax.ShapeDtypeStruct((s, OFF_LAT), F32),
                   jax.ShapeDtypeStruct((s, LANES), F32),
                   jax.ShapeDtypeStruct((s, DIFF_W), BF16),
                   jax.ShapeDtypeStruct((s, DIFF_W), BF16),
                   jax.ShapeDtypeStruct((DIFF_HEADS * V_EXT, s), BF16)],
        compiler_params=_cparams(("arbitrary",)),
        name="in_proj",
    )(x, g, sc, sh, w_lat, w_qk, w_kpe, wv_t, *tabs)


def _mla_prep_kernel(lat_ref, kpe_ref, gq_ref, gkv_ref, wuq_ref, wuk_ref, wvt_ref,
                     c_ref, sa_ref, sb_ref, q_ref, k_ref, vt_ref):
    lat = lat_ref[...]
    qn = (_rms(lat[:, :MLA_Q_RANK]) * gq_ref[...]).astype(BF16)
    kvn = (_rms(lat[:, MLA_Q_RANK:]) * gkv_ref[...]).astype(BF16)
    q = _dot(qn, wuq_ref[...])
    kn = _dot(kvn, wuk_ref[...])
    c, sa, sb = c_ref[...], sa_ref[...], sb_ref[...]
    half = MLA_ROPE // 2
    qs = (MLA_NOPE + MLA_ROPE) ** -0.5 * LOG2E
    kpe = _rope_lanes(kpe_ref[...], c, sa, sb, half).astype(BF16)
    for h in range(MLA_HEADS):
        o = h * MLA_QK_PAD
        q_ref[:, o:o + LANES] = (q[:, o:o + LANES] * qs).astype(BF16)
        q_ref[:, o + LANES:o + 2 * LANES] = (
            _rope_lanes(q[:, o + LANES:o + 2 * LANES], c, sa, sb, half) * qs).astype(BF16)
        k_ref[:, o:o + LANES] = kn[:, h * MLA_NOPE:(h + 1) * MLA_NOPE].astype(BF16)
        k_ref[:, o + LANES:o + 2 * LANES] = kpe
    _store_vt_ext(vt_ref, _dot_nt(wvt_ref[...], kvn), MLA_HEADS)


def _mla_prep(lat, kpe, gq, gkv, wuq_r, wuk_r, wv_t, tabs):
    s = lat.shape[0]
    tm = min(TM_PROJ, s)
    row = lambda n: pl.BlockSpec((tm, n), lambda i: (i, 0))
    hq = MLA_HEADS * MLA_QK_PAD
    hv = MLA_HEADS * V_EXT
    return pl.pallas_call(
        _mla_prep_kernel,
        grid=(s // tm,),
        in_specs=[row(OFF_LAT), row(LANES), _const_spec(gq.shape), _const_spec(gkv.shape),
                  _const_spec(wuq_r.shape), _const_spec(wuk_r.shape), _const_spec(wv_t.shape),
                  row(LANES), row(LANES), row(LANES)],
        out_specs=[row(hq), row(hq), pl.BlockSpec((hv, tm), lambda i: (0, i))],
        out_shape=[jax.ShapeDtypeStruct((s, hq), BF16),
                   jax.ShapeDtypeStruct((s, hq), BF16),
                   jax.ShapeDtypeStruct((hv, s), BF16)],
        compiler_params=_cparams(("arbitrary",)),
        name="mla_prep",
    )(lat, kpe, gq, gkv, wuq_r, wuk_r, wv_t, *tabs)


def _pair_tables(nq):
    n = [ATT_KV_PER_Q * (i + 1) for i in range(nq)]
    qi = np.concatenate([np.full(c, i) for i, c in enumerate(n)]).astype(np.int32)
    kj = np.concatenate([np.arange(c) for c in n]).astype(np.int32)
    return jnp.asarray(qi), jnp.asarray(kj)


def _ones_rows(n):
    rows = V_EXT - MLA_V
    return (lax.broadcasted_iota(I32, (rows, n), 0) == 0).astype(BF16)


def _store_vt_ext(vt_ref, vt, heads):
    ones = _ones_rows(vt.shape[1])
    for h in range(heads):
        vt_ref[h * V_EXT:h * V_EXT + MLA_V, :] = vt[h * MLA_V:(h + 1) * MLA_V, :].astype(BF16)
        vt_ref[h * V_EXT + MLA_V:(h + 1) * V_EXT, :] = ones


def _attn_step(q_refs, k_ref, vt_ref, s_bufs, m_refs, acc_refs, rel):
    w = UNIT_W
    nk = k_ref.shape[0]
    per_kv = nk // w
    specs = []
    for u in range(q_refs[0].shape[0] // w):
        d = None if rel is None else u - rel * per_kv
        if d is None or d >= per_kv:
            specs.append((u, nk, None))
        elif d >= 0:
            specs.append((u, (d + 1) * w, d * w))
    units = [(si, sp) for sp in specs for si in range(len(q_refs))]

    def scores(n):
        si, (u, rows, off) = units[n]
        s = _dot_nt(k_ref[0:rows, :], q_refs[si][u * w:(u + 1) * w, :])
        if off is not None:
            kc = lax.broadcasted_iota(I32, (rows, w), 0) // CHUNK
            qc = (lax.broadcasted_iota(I32, (rows, w), 1) + off) // CHUNK
            s = jnp.where(kc <= qc, s, -jnp.inf)
        s_bufs[n % 2][0:rows, :] = s
        return jnp.max(s, axis=0, keepdims=True)

    cmax_next = scores(0)
    for n, (si, (u, rows, _)) in enumerate(units):
        cmax = cmax_next
        if n + 1 < len(units):
            cmax_next = scores(n + 1)
        m_ref, acc_ref = m_refs[si][u], acc_refs[si][u]
        m_prev = m_ref[...]
        m_new = jnp.maximum(m_prev, cmax)
        m_ref[...] = m_new
        acc_ref[...] = jnp.exp2(m_prev - m_new) * acc_ref[...]
        p = jnp.exp2(s_bufs[n % 2][0:rows, :] - m_new).astype(BF16)
        acc_ref[...] += _dot(vt_ref[:, 0:rows], p)


def _init_stats(ms, accs):
    for m, a in zip(ms, accs):
        m[...] = jnp.full(m.shape, -jnp.inf, F32)
        a[...] = jnp.zeros(a.shape, F32)


def _normalized(acc_ref):
    acc = acc_ref[...]
    return acc[0:MLA_V, :] / acc[MLA_V:MLA_V + 1, :]


def _attn_phases(qi_ref, kj_ref, init, step, finish):
    t = pl.program_id(1)
    rel = kj_ref[t] - ATT_KV_PER_Q * qi_ref[t]

    @pl.when(kj_ref[t] == 0)
    def _():
        init()

    @pl.when(rel < 0)
    def _():
        step(None)

    for r in range(ATT_KV_PER_Q):
        @pl.when(rel == r)
        def _(r=r):
            step(r)
            if r == ATT_KV_PER_Q - 1:
                finish()


def _unit_scratch(n_streams):
    nu = ATT_BQ // UNIT_W
    return ([pltpu.VMEM((ATT_BK, UNIT_W), F32)] * 2
            + [pltpu.VMEM((1, UNIT_W), F32)] * (nu * n_streams)
            + [pltpu.VMEM((V_EXT, UNIT_W), F32)] * (nu * n_streams))


def _split_units(scr, n_streams):
    nu = ATT_BQ // UNIT_W
    ms = [list(scr[s * nu:(s + 1) * nu]) for s in range(n_streams)]
    accs = [list(scr[(n_streams + s) * nu:(n_streams + s + 1) * nu]) for s in range(n_streams)]
    return ms, accs


def _mla_attn_kernel(qi_ref, kj_ref, q_ref, k_ref, vt_ref, o_ref, *scr):
    s_bufs, (ms, accs) = scr[:2], _split_units(scr[2:], 1)

    def finish():
        for u, a in enumerate(accs[0]):
            o_ref[u * UNIT_W:(u + 1) * UNIT_W, :] = _normalized(a).T.astype(o_ref.dtype)

    _attn_phases(qi_ref, kj_ref,
                 lambda: _init_stats(ms[0], accs[0]),
                 lambda rel: _attn_step([q_ref], k_ref, vt_ref, s_bufs, ms, accs, rel),
                 finish)


def _mla_attn(qc, kc, vt):
    s = qc.shape[0]
    assert s % ATT_BQ == 0
    qi, kj = _pair_tables(s // ATT_BQ)
    gs = pltpu.PrefetchScalarGridSpec(
        num_scalar_prefetch=2,
        grid=(MLA_HEADS, qi.shape[0]),
        in_specs=[pl.BlockSpec((ATT_BQ, MLA_QK_PAD), lambda h, t, qi, kj: (qi[t], h)),
                  pl.BlockSpec((ATT_BK, MLA_QK_PAD), lambda h, t, qi, kj: (kj[t], h)),
                  pl.BlockSpec((V_EXT, ATT_BK), lambda h, t, qi, kj: (h, kj[t]))],
        out_specs=pl.BlockSpec((ATT_BQ, MLA_V), lambda h, t, qi, kj: (qi[t], h)),
        scratch_shapes=_unit_scratch(1),
    )
    return pl.pallas_call(
        _mla_attn_kernel,
        grid_spec=gs,
        out_shape=jax.ShapeDtypeStruct((s, MLA_HEADS * MLA_V), BF16),
        compiler_params=_cparams(("arbitrary", "arbitrary")),
        name="mla_attn",
    )(qi, kj, qc, kc, vt)


def _diff_attn_kernel(qi_ref, kj_ref, q_ref, k_ref, vt_ref, lq1_ref, lk1_ref, lq2_ref, lk2_ref,
                      g_ref, o_ref, q0_scr, q1_scr, *scr):
    s_bufs, (ms, accs) = scr[:2], _split_units(scr[2:], 2)

    def init():
        q = q_ref[...]
        lane = lax.broadcasted_iota(I32, q.shape, 1)
        zero = jnp.zeros_like(q)
        q0_scr[...] = jnp.where(lane < DIFF_HEAD_DIM, q, zero)
        q1_scr[...] = jnp.where(lane >= DIFF_HEAD_DIM, q, zero)
        _init_stats(ms[0] + ms[1], accs[0] + accs[1])

    def finish():
        lam = (jnp.exp(jnp.sum(lq1_ref[...] * lk1_ref[...], axis=1, keepdims=True))
               - jnp.exp(jnp.sum(lq2_ref[...] * lk2_ref[...], axis=1, keepdims=True))
               + LAMBDA_INIT)
        for u, (a0, a1) in enumerate(zip(accs[0], accs[1])):
            o = _normalized(a0) - lam * _normalized(a1)
            o = o * lax.rsqrt(jnp.mean(o * o, axis=0, keepdims=True) + EPS)
            o_ref[u * UNIT_W:(u + 1) * UNIT_W, :] = (
                o * g_ref[...] * (1.0 - LAMBDA_INIT)).T.astype(o_ref.dtype)

    _attn_phases(qi_ref, kj_ref, init,
                 lambda rel: _attn_step([q0_scr, q1_scr], k_ref, vt_ref, s_bufs, ms, accs, rel),
                 finish)


def _diff_attn(dq, dk, dvt, lq1, lk1, lq2, lk2, g_col):
    s = dq.shape[0]
    assert s % ATT_BQ == 0
    qi, kj = _pair_tables(s // ATT_BQ)
    hw = 2 * DIFF_HEAD_DIM
    small = lambda a: pl.BlockSpec(a.shape, lambda h, t, qi, kj: (0, 0))
    gs = pltpu.PrefetchScalarGridSpec(
        num_scalar_prefetch=2,
        grid=(DIFF_HEADS, qi.shape[0]),
        in_specs=[pl.BlockSpec((ATT_BQ, hw), lambda h, t, qi, kj: (qi[t], h)),
                  pl.BlockSpec((ATT_BK, hw), lambda h, t, qi, kj: (kj[t], h)),
                  pl.BlockSpec((V_EXT, ATT_BK), lambda h, t, qi, kj: (h, kj[t])),
                  small(lq1), small(lk1), small(lq2), small(lk2), small(g_col)],
        out_specs=pl.BlockSpec((ATT_BQ, hw), lambda h, t, qi, kj: (qi[t], h)),
        scratch_shapes=[pltpu.VMEM((ATT_BQ, hw), BF16), pltpu.VMEM((ATT_BQ, hw), BF16)] + _unit_scratch(2),
    )
    return pl.pallas_call(
        _diff_attn_kernel,
        grid_spec=gs,
        out_shape=jax.ShapeDtypeStruct((s, DIFF_W), BF16),
        compiler_params=_cparams(("arbitrary", "arbitrary")),
        name="diff_attn",
    )(qi, kj, dq, dk, dvt, lq1, lk1, lq2, lk2, g_col)


def _pack_words(y):
    lo = pltpu.bitcast(y[:, :HALF].astype(BF16).astype(F32), U32) >> 16
    hi = pltpu.bitcast(y[:, HALF:].astype(BF16).astype(F32), U32) & jnp.uint32(0xFFFF0000)
    return lo | hi


def _store_packed(ref, words, rows):
    for s in range(ROW_TILES):
        ref[pl.ds(s, rows, stride=ROW_TILES), :] = words[:, s * LANES:(s + 1) * LANES]


def _load_packed(ref, rows):
    return jnp.concatenate(
        [ref[pl.ds(s, rows, stride=ROW_TILES), :] for s in range(ROW_TILES)], axis=1)


def _unpack_words(w):
    lo = pltpu.bitcast(w << 16, F32)
    hi = pltpu.bitcast(w & jnp.uint32(0xFFFF0000), F32)
    return lo, hi


def _outproj_kernel(om_ref, od_ref, x_ref, w_ref, gt_ref, gpost_ref, gpre_ref, sc_ref, sh_ref,
                    wr_ref, x1_ref, h2_ref, h2p_ref, lg_ref):
    n_sub = x_ref.shape[0] // TM_PROJ
    sub = lambda r, j: r.at[pl.ds(j * TM_PROJ, TM_PROJ)]
    w_hi, w_lo = _split_bf16(wr_ref[...])
    ne = w_hi.shape[0]
    w_hl = jnp.concatenate([w_hi, w_lo], axis=0)
    ys = [_dot(jnp.concatenate([sub(om_ref, j)[...], sub(od_ref, j)[...]], axis=1), w_ref[...])
          for j in range(n_sub)]
    for j, y in enumerate(ys):
        x1 = sub(x_ref, j)[...] + gt_ref[...] * (_rms(y) * gpost_ref[...])
        sub(x1_ref, j)[...] = x1
        h2 = _rms(x1) * gpre_ref[...] * (1.0 + sc_ref[...]) + sh_ref[...]
        h_hi, h_lo = _split_bf16(h2)
        sub(h2_ref, j)[...] = h_hi
        _store_packed(h2p_ref.at[pl.ds(j * TM_PROJ * ROW_TILES, TM_PROJ * ROW_TILES)], _pack_words(h2), TM_PROJ)
        a = _dot_nt(w_hl, h_hi)
        lg_ref[:, j * TM_PROJ:(j + 1) * TM_PROJ] = a[0:ne] + a[ne:] + _dot_nt(w_hi, h_lo)


def _outproj(om, od, x, w_out, gt, gpost, gpre, sc, sh, wr_t):
    s, d = x.shape
    tm = min(TM_OUT, s)
    row = lambda n: pl.BlockSpec((tm, n), lambda i: (i, 0))
    vec = _const_spec((1, d))
    return pl.pallas_call(
        _outproj_kernel,
        grid=(s // tm,),
        in_specs=[row(om.shape[1]), row(od.shape[1]), row(d), _const_spec(w_out.shape),
                  vec, vec, vec, vec, vec, _const_spec(wr_t.shape)],
        out_specs=[row(d), row(d),
                   pl.BlockSpec((tm * ROW_TILES, LANES), lambda i: (i, 0)),
                   pl.BlockSpec((N_EXPERTS, tm), lambda i: (0, i))],
        out_shape=[jax.ShapeDtypeStruct((s, d), F32),
                   jax.ShapeDtypeStruct((s, d), BF16),
                   jax.ShapeDtypeStruct((s * ROW_TILES, LANES), U32),
                   jax.ShapeDtypeStruct((N_EXPERTS, s), F32)],
        compiler_params=_cparams(("arbitrary",)),
        name="out_proj",
    )(om, od, x, w_out, gt, gpost, gpre, sc, sh, wr_t)


def _route_kernel(lg_ref, b_ref, eidx_ref, w_ref, rank_ref, cnt_ref, carry):
    step = pl.program_id(0)

    @pl.when(step == 0)
    def _():
        carry[...] = jnp.zeros(carry.shape, F32)

    tr = lg_ref.shape[1]
    ninf = jnp.float32(-jnp.inf)
    ie = lax.broadcasted_iota(I32, (GROUP_SIZE, tr), 0)
    rmax = lambda a: jnp.max(a, axis=0, keepdims=True)
    rmin = lambda a: jnp.min(a, axis=0, keepdims=True)
    rsum = lambda a: jnp.sum(a, axis=0, keepdims=True)

    sc, bi, gscore = [], [], []
    for g in range(N_GROUPS):
        lg = lg_ref[g * GROUP_SIZE:(g + 1) * GROUP_SIZE, :]
        s = jax.nn.sigmoid(lg)
        b = s + b_ref[g * GROUP_SIZE:(g + 1) * GROUP_SIZE, :]
        m1 = rmax(b)
        i1 = rmin(jnp.where(b == m1, ie, GROUP_SIZE))
        m2 = rmax(jnp.where(ie == i1, ninf, b))
        sc.append(s)
        bi.append(b)
        gscore.append(m1 + m2)

    gsel = [jnp.zeros((1, tr), I32) for _ in range(N_GROUPS)]
    for _ in range(TOPK_GROUPS):
        gm = functools.reduce(jnp.maximum, gscore)
        gi = functools.reduce(
            jnp.minimum, [jnp.where(gscore[g] == gm, g, N_GROUPS) for g in range(N_GROUPS)])
        for g in range(N_GROUPS):
            hit = gi == g
            gsel[g] = jnp.where(hit, 1, gsel[g])
            gscore[g] = jnp.where(hit, ninf, gscore[g])

    masked = [jnp.where(jnp.broadcast_to(gsel[g], (GROUP_SIZE, tr)) > 0, bi[g], ninf)
              for g in range(N_GROUPS)]
    sel = [jnp.zeros((GROUP_SIZE, tr), F32) for _ in range(N_GROUPS)]
    idxs, svals = [], []
    for _ in range(TOP_K):
        m = functools.reduce(jnp.maximum, [rmax(x) for x in masked])
        idx = functools.reduce(
            jnp.minimum,
            [rmin(jnp.where(masked[g] == m, ie + g * GROUP_SIZE, N_EXPERTS)) for g in range(N_GROUPS)])
        sv = jnp.zeros((1, tr), F32)
        for g in range(N_GROUPS):
            hit = (ie + g * GROUP_SIZE) == idx
            sv = sv + rsum(jnp.where(hit, sc[g], 0.0))
            masked[g] = jnp.where(hit, ninf, masked[g])
            sel[g] = jnp.where(hit, 1.0, sel[g])
        idxs.append(idx)
        svals.append(sv)
    tot = functools.reduce(lambda a, b: a + b, svals)

    sel2 = jnp.concatenate(sel, axis=0)
    before = (lax.broadcasted_iota(I32, (tr, tr), 0) < lax.broadcasted_iota(I32, (tr, tr), 1))
    rank2 = _dot(sel2.astype(BF16), before.astype(BF16)) + carry[:, 0:1]
    for k in range(TOP_K):
        rk = jnp.zeros((1, tr), F32)
        for g in range(N_GROUPS):
            hit = (ie + g * GROUP_SIZE) == idxs[k]
            rk = rk + rsum(jnp.where(hit, rank2[g * GROUP_SIZE:(g + 1) * GROUP_SIZE, :], 0.0))
        eidx_ref[k:k + 1, :] = idxs[k]
        w_ref[k:k + 1, :] = svals[k] / tot * ROUTED_SCALE
        rank_ref[k:k + 1, :] = rk.astype(I32)
    carry[...] = carry[...] + jnp.sum(sel2, axis=1, keepdims=True)
    cnt_ref[...] = carry[...]


def _route(lg_t, b_col):
    e, t = lg_t.shape
    tr = min(TR, t)
    tok = lambda: pl.BlockSpec((TOP_K, tr), lambda i: (0, i))
    return pl.pallas_call(
        _route_kernel,
        grid=(t // tr,),
        in_specs=[pl.BlockSpec((e, tr), lambda i: (0, i)), pl.BlockSpec((e, 1), lambda i: (0, 0))],
        out_specs=[tok(), tok(), tok(), pl.BlockSpec((e, LANES), lambda i: (0, 0))],
        out_shape=[jax.ShapeDtypeStruct((TOP_K, t), I32),
                   jax.ShapeDtypeStruct((TOP_K, t), F32),
                   jax.ShapeDtypeStruct((TOP_K, t), I32),
                   jax.ShapeDtypeStruct((e, LANES), F32)],
        scratch_shapes=[pltpu.VMEM((e, LANES), F32)],
        compiler_params=_cparams(("arbitrary",)),
        name="route",
    )(lg_t, b_col)


def _pos_kernel(pstart_ref, eidx_ref, rank_ref, pos_ref):
    e = eidx_ref[...]
    pos = rank_ref[...]
    for x in range(N_EXPERTS):
        pos = pos + jnp.where(e == x, pstart_ref[x], 0)
    pos_ref[...] = pos


def _positions(pstart, eidx, rank):
    k, t = eidx.shape
    tr = min(TR, t)
    gs = pltpu.PrefetchScalarGridSpec(
        num_scalar_prefetch=1,
        grid=(t // tr,),
        in_specs=[pl.BlockSpec((k, tr), lambda i, p: (0, i)), pl.BlockSpec((k, tr), lambda i, p: (0, i))],
        out_specs=pl.BlockSpec((k, tr), lambda i, p: (0, i)),
    )
    return pl.pallas_call(
        _pos_kernel, grid_spec=gs,
        out_shape=jax.ShapeDtypeStruct((k, t), I32),
        compiler_params=_cparams(("arbitrary",)),
        name="positions",
    )(pstart, eidx, rank)


def _row_copy(src, src_row, dst, dst_row, sem):
    return pltpu.make_async_copy(src.at[pl.ds(src_row * ROW_TILES, ROW_TILES)],
                                 dst.at[pl.ds(dst_row * ROW_TILES, ROW_TILES)], sem)


FILL_CHUNKS = (128, 64, 32, 16, 8, 4, 2, 1)


def _zero_fill(fs_ref, fl_ref, tail_ref, zbuf, xs_ref, sem, wait):
    def chunk(row, n):
        return pltpu.make_async_copy(zbuf.at[pl.ds(0, n * ROW_TILES)],
                                     xs_ref.at[pl.ds(row * ROW_TILES, n * ROW_TILES)], sem)

    def go(cp):
        if wait:
            cp.wait()
        else:
            cp.start()

    def per_expert(e, c):
        pad = fl_ref[e]
        row = fs_ref[e]
        for n in FILL_CHUNKS:
            @pl.when((pad & n) != 0)
            def _(row=row, n=n):
                go(chunk(row, n))
            row = row + (pad & n)
        return c

    lax.fori_loop(0, N_EXPERTS, per_expert, 0)

    def per_tail(b, c):
        for part in range(EXP_BLK // FILL_CHUNKS[0]):
            go(chunk(b * EXP_BLK + part * FILL_CHUNKS[0], FILL_CHUNKS[0]))
        return c

    lax.fori_loop(tail_ref[0], tail_ref[1], per_tail, 0)


def _dispatch_kernel(fs_ref, fl_ref, tail_ref, pos_ref, h_ref, xs_ref, zbuf, sem, fill_sem):
    td = pos_ref.shape[1]
    step = pl.program_id(0)

    @pl.when(step == 0)
    def _():
        zbuf[...] = jnp.zeros(zbuf.shape, zbuf.dtype)
        _zero_fill(fs_ref, fl_ref, tail_ref, zbuf, xs_ref, fill_sem, wait=False)

    def issue(t, c):
        for k in range(TOP_K):
            _row_copy(h_ref, t, xs_ref, pos_ref[k, t], sem).start(priority=k % 2)
        return c

    lax.fori_loop(0, td, issue, 0, unroll=4)
    for k in range(TOP_K):
        pltpu.make_async_copy(h_ref, xs_ref.at[pl.ds(0, td * ROW_TILES)], sem).wait()

    @pl.when(step == pl.num_programs(0) - 1)
    def _():
        _zero_fill(fs_ref, fl_ref, tail_ref, zbuf, xs_ref, fill_sem, wait=True)


def _dispatch(fill_start, fill_len, tail, pos, h2p, m_pad):
    k, t = pos.shape
    td = min(TD, t)
    gs = pltpu.PrefetchScalarGridSpec(
        num_scalar_prefetch=3,
        grid=(t // td,),
        in_specs=[pl.BlockSpec((k, td), lambda i, *_: (0, i), memory_space=pltpu.SMEM),
                  pl.BlockSpec((td * ROW_TILES, LANES), lambda i, *_: (i, 0))],
        out_specs=pl.BlockSpec(memory_space=pl.ANY),
        scratch_shapes=[pltpu.VMEM((FILL_CHUNKS[0] * ROW_TILES, LANES), U32),
                        pltpu.SemaphoreType.DMA, pltpu.SemaphoreType.DMA],
    )
    return pl.pallas_call(
        _dispatch_kernel,
        grid_spec=gs,
        out_shape=jax.ShapeDtypeStruct((m_pad * ROW_TILES, LANES), U32),
        compiler_params=_cparams(("arbitrary",)),
        name="dispatch",
    )(fill_start, fill_len, tail, pos, h2p)


def _weight_copies(w_hbm, w_buf, sem, e, s):
    return [pltpu.make_async_copy(h.at[e], b.at[s], sem.at[s]) for h, b in zip(w_hbm, w_buf)]


def _experts_kernel(be_ref, nu_ref, fresh_ref, nxt_ref, slot_ref, x_ref, wg_hbm, wu_hbm, wd_hbm, y_ref,
                    wg_buf, wu_buf, wd_buf, wg_s, wu_s, wd_s, sem):
    step = pl.program_id(0)
    b0 = step * BLKS_PER_STEP
    blk_rows = x_ref.shape[0] // BLKS_PER_STEP
    w_hbm = (wg_hbm, wu_hbm, wd_hbm)
    w_buf = (wg_buf, wu_buf, wd_buf)

    def maybe_weights(b):
        @pl.when(fresh_ref[b] == 1)
        def _():
            s = slot_ref[b]

            @pl.when(b == 0)
            def _():
                for c in _weight_copies(w_hbm, w_buf, sem, be_ref[0], 0):
                    c.start()

            for c in _weight_copies(w_hbm, w_buf, sem, be_ref[b], s):
                c.wait()

            @pl.when(nxt_ref[b] >= 0)
            def _():
                for c in _weight_copies(w_hbm, w_buf, sem, nxt_ref[b], 1 - s):
                    c.start()

            wg_s[...] = wg_buf[s].astype(BF16)
            wu_s[...] = wu_buf[s].astype(BF16)
            wd_s[...] = wd_buf[s].astype(BF16)

    def swiglu(first, n):
        x_blk = x_ref.at[pl.ds(first * blk_rows, n * blk_rows)]
        y_blk = y_ref.at[pl.ds(first * blk_rows, n * blk_rows)]
        rows = n * blk_rows // ROW_TILES
        xa, xb = _unpack_words(_load_packed(x_blk, rows))
        xa = xa.astype(BF16)
        xb = xb.astype(BF16)
        g = _dot(xa, wg_s[0:HALF, :]) + _dot(xb, wg_s[HALF:, :])
        u = _dot(xa, wu_s[0:HALF, :]) + _dot(xb, wu_s[HALF:, :])
        a = (g * jax.nn.sigmoid(g) * u).astype(BF16)
        _store_packed(y_blk, _pack_words(_dot(a, wd_s[...])), rows)

    def zero(first, n):
        y_ref[pl.ds(first * blk_rows, n * blk_rows), :] = jnp.zeros((n * blk_rows, LANES), y_ref.dtype)

    n_used = jnp.clip(nu_ref[0] - b0, 0, BLKS_PER_STEP)
    same = jnp.logical_and(n_used == BLKS_PER_STEP, be_ref[b0] == be_ref[b0 + 1])

    @pl.when(same)
    def _():
        maybe_weights(b0)
        swiglu(0, BLKS_PER_STEP)

    @pl.when(jnp.logical_and(jnp.logical_not(same), n_used > 0))
    def _():
        maybe_weights(b0)
        swiglu(0, 1)

        @pl.when(n_used > 1)
        def _():
            maybe_weights(b0 + 1)
            swiglu(1, 1)

        @pl.when(n_used == 1)
        def _():
            zero(1, 1)

    @pl.when(n_used == 0)
    def _():
        zero(0, BLKS_PER_STEP)


def _experts(blk_e, nused, fresh, nxt_e, slot, xs, w_gate, w_up, w_down, blk):
    n_blk = blk_e.shape[0]
    assert n_blk % BLKS_PER_STEP == 0 and BLKS_PER_STEP == 2
    ne, d, de = w_gate.shape
    rows = BLKS_PER_STEP * blk * ROW_TILES
    xmap = lambda i, be, nu, fr, nx, sl: (jnp.minimum(i, (nu[0] - 1) // BLKS_PER_STEP), 0)
    hbm = pl.BlockSpec(memory_space=pl.ANY)
    gs = pltpu.PrefetchScalarGridSpec(
        num_scalar_prefetch=5,
        grid=(n_blk // BLKS_PER_STEP,),
        in_specs=[pl.BlockSpec((rows, LANES), xmap), hbm, hbm, hbm],
        out_specs=pl.BlockSpec((rows, LANES), lambda i, *_: (i, 0)),
        scratch_shapes=[pltpu.VMEM((2, d, de), F32), pltpu.VMEM((2, d, de), F32), pltpu.VMEM((2, de, d), F32),
                        pltpu.VMEM((d, de), BF16), pltpu.VMEM((d, de), BF16), pltpu.VMEM((de, d), BF16),
                        pltpu.SemaphoreType.DMA((2,))],
    )
    return pl.pallas_call(
        _experts_kernel, grid_spec=gs,
        out_shape=jax.ShapeDtypeStruct(xs.shape, U32),
        compiler_params=_cparams(("arbitrary",)),
        name="experts",
    )(blk_e, nused, fresh, nxt_e, slot, xs, w_gate, w_up, w_down)


def _combine_kernel(pos_ref, wt_ref, ys_ref, h_ref, x1_ref, wsg_ref, wsu_ref, wsd_ref,
                    gt_ref, gpost_ref, o_ref, gbuf, sem):
    td = pos_ref.shape[1]

    def issue(t, c):
        for k in range(TOP_K):
            _row_copy(ys_ref, pos_ref[k, t], gbuf.at[k], t, sem).start(priority=k % 2)
        return c

    lax.fori_loop(0, td, issue, 0, unroll=4)

    hb = h_ref[...]
    g = _dot(hb, wsg_ref[...])
    u = _dot(hb, wsu_ref[...])
    y = _dot((g * jax.nn.sigmoid(g) * u).astype(BF16), wsd_ref[...])

    wpad = jnp.concatenate([wt_ref[...], jnp.zeros((LANES - TOP_K, td), F32)], axis=0)
    wcol = wpad.T

    for k in range(TOP_K):
        pltpu.make_async_copy(ys_ref.at[pl.ds(0, td * ROW_TILES)], gbuf.at[k], sem).wait()
    lo = jnp.zeros((td, HALF), F32)
    hi = jnp.zeros((td, HALF), F32)
    for k in range(TOP_K):
        a, b = _unpack_words(_load_packed(gbuf.at[k], td))
        wk = wcol[:, k:k + 1]
        lo = lo + wk * a
        hi = hi + wk * b
    y = y + jnp.concatenate([lo, hi], axis=1)
    o_ref[...] = x1_ref[...] + gt_ref[...] * (_rms(y) * gpost_ref[...])


def _combine(pos, wts, ys, h2, x1, wsg, wsu, wsd, gt, gpost):
    k, t = pos.shape
    d = x1.shape[1]
    td = min(TD, t)
    assert td == LANES
    row = lambda n: pl.BlockSpec((td, n), lambda i: (i, 0))
    vec = _const_spec((1, d))
    return pl.pallas_call(
        _combine_kernel,
        grid=(t // td,),
        in_specs=[pl.BlockSpec((k, td), lambda i: (0, i), memory_space=pltpu.SMEM),
                  pl.BlockSpec((k, td), lambda i: (0, i)),
                  pl.BlockSpec(memory_space=pl.ANY),
                  row(d), row(d), _const_spec(wsg.shape), _const_spec(wsu.shape), _const_spec(wsd.shape),
                  vec, vec],
        out_specs=row(d),
        out_shape=jax.ShapeDtypeStruct((t, d), F32),
        scratch_shapes=[pltpu.VMEM((k, td * ROW_TILES, LANES), U32), pltpu.SemaphoreType.DMA],
        compiler_params=_cparams(("arbitrary",)),
        name="combine",
    )(pos, wts, ys, h2, x1, wsg, wsu, wsd, gt, gpost)


def _rope_tables(seq, dim):
    pos = jnp.arange(seq, dtype=F32)
    inv = ROPE_THETA ** (-jnp.arange(0, dim, 2, dtype=F32) / dim)
    ang = pos[:, None] * inv[None, :]
    return jnp.cos(ang), jnp.sin(ang)


def _lane_tables(seq, dim, period):
    cos, sin = _rope_tables(seq, dim)
    half = dim // 2
    cos = jnp.tile(cos, (1, LANES // half))
    sin = jnp.tile(sin, (1, LANES // half))
    r = jnp.arange(LANES) % period
    c = jnp.where(r < dim, cos, 1.0)
    sa = jnp.where(r < half, -sin, 0.0)
    sb = jnp.where((r >= half) & (r < dim), sin, 0.0)
    return c, sa, sb


def _moe(h2, h2p, x1, lg_t, b_router, w_gate, w_up, w_down, wsg, wsu, wsd, gt_f, g_post_ffn):
    t = h2.shape[0]
    blk = EXP_BLK
    m_pad = t * TOP_K + N_EXPERTS * blk
    eidx, wts, rank, cnt = _route(lg_t, b_router.reshape(N_EXPERTS, 1))
    counts = cnt[:, 0].astype(I32)
    padded = (counts + blk - 1) // blk * blk
    pend = jnp.cumsum(padded)
    pstart = pend - padded
    n_blk = m_pad // blk
    blk_start = jnp.arange(n_blk, dtype=I32) * blk
    blk_e = jnp.minimum(jnp.sum(pend[None, :] <= blk_start[:, None], axis=1), N_EXPERTS - 1).astype(I32)
    nused = (pend[-1:] // blk).astype(I32)
    pos = _positions(pstart.astype(I32), eidx, rank)
    tail = jnp.concatenate([nused, jnp.full((1,), n_blk, I32)])
    xs = _dispatch((pstart + counts).astype(I32), (padded - counts).astype(I32), tail, pos, h2p, m_pad)
    idx = jnp.arange(n_blk, dtype=I32)
    fresh = (idx < nused[0]) & (blk_e != jnp.concatenate([jnp.full((1,), -1, I32), blk_e[:-1]]))
    slot = ((jnp.cumsum(fresh.astype(I32)) - 1) % 2).astype(I32)
    first_at_or_after = lax.cummin(jnp.where(fresh, idx, n_blk)[::-1])[::-1]
    nxt_idx = jnp.concatenate([first_at_or_after[1:], jnp.full((1,), n_blk, I32)])
    nxt_e = jnp.where(nxt_idx < n_blk, blk_e[jnp.minimum(nxt_idx, n_blk - 1)], -1).astype(I32)
    ys = _experts(blk_e, nused, fresh.astype(I32), nxt_e, slot, xs, w_gate, w_up, w_down, blk)
    return _combine(pos, wts, ys, h2, x1, wsg, wsu, wsd, gt_f, g_post_ffn)


def _layer(x, c, w_ada, b_ada, g_pre_mix, w_in, g_q_lat, w_uq, g_kv_lat, w_ukv,
           lq1, lk1, lq2, lk2, g_diff_sub, w_out, g_post_mix, g_pre_ffn, w_router, b_router,
           w_gate, w_up, w_down, ws_gate, ws_up, ws_down, g_post_ffn):
    s, d = x.shape
    row = lambda a: a.reshape(1, -1)

    mod = _ada(c, w_ada, row(b_ada))
    sh_a, sc_a, gt_a, sh_f, sc_f, gt_f = [mod[:, i * d:(i + 1) * d] for i in range(6)]

    o1 = MLA_Q_RANK + MLA_KV_RANK
    o2 = o1 + MLA_ROPE
    o3 = o2 + 2 * DIFF_W
    w_kpe = jnp.pad(w_in[:, o1:o2], ((0, 0), (0, LANES - MLA_ROPE))).astype(BF16)
    wv_t = lax.optimization_barrier(w_in[:, o3:]).T.astype(BF16)
    lat, kpe, dq, dk, dvt = _inproj(x, row(g_pre_mix), sc_a, sh_a, w_in[:, :o1].astype(BF16),
                                    w_in[:, o2:o3].astype(BF16), w_kpe, wv_t,
                                    _lane_tables(s, DIFF_ROT, DIFF_HEAD_DIM))

    wq = w_uq.reshape(MLA_Q_RANK, MLA_HEADS, MLA_NOPE + MLA_ROPE)
    wq = jnp.pad(wq, ((0, 0), (0, 0), (0, MLA_QK_PAD - MLA_NOPE - MLA_ROPE)))
    wuq_r = wq.reshape(MLA_Q_RANK, MLA_HEADS * MLA_QK_PAD).astype(BF16)
    wkv = w_ukv.reshape(MLA_KV_RANK, MLA_HEADS, MLA_NOPE + MLA_V)
    wuk_r = wkv[:, :, :MLA_NOPE].reshape(MLA_KV_RANK, -1).astype(BF16)
    wv_t = wkv[:, :, MLA_NOPE:].reshape(MLA_KV_RANK, -1).T.astype(BF16)
    qc, kc, vt = _mla_prep(lat, kpe, row(g_q_lat), row(g_kv_lat), wuq_r, wuk_r, wv_t,
                           _lane_tables(s, MLA_ROPE, MLA_ROPE))

    o_mla = _mla_attn(qc, kc, vt)
    o_diff = _diff_attn(dq, dk, dvt, row(lq1), row(lk1), row(lq2), row(lk2), g_diff_sub.reshape(-1, 1))

    x1, h2, h2p, lg_t = _outproj(o_mla, o_diff, x, w_out.astype(BF16), gt_a, row(g_post_mix),
                                 row(g_pre_ffn), sc_f, sh_f, w_router.T)
    return _moe(h2, h2p, x1, lg_t, b_router, w_gate, w_up, w_down,
                ws_gate.astype(BF16), ws_up.astype(BF16), ws_down.astype(BF16), gt_f, row(g_post_ffn))


def kernel(x, c, w_ada, b_ada, g_pre_mix, w_in, g_q_lat, w_uq, g_kv_lat, w_ukv, lambda_q1, lambda_k1, lambda_q2, lambda_k2, g_diff_sub, w_out, g_post_mix, g_pre_ffn, w_router, b_router, w_gate, w_up, w_down, ws_gate, ws_up, ws_down, g_post_ffn):
    assert x.shape[0] == 1 and w_ada.shape[0] == 1
    out = _layer(x[0], c, w_ada[0], b_ada[0], g_pre_mix[0], w_in[0], g_q_lat[0], w_uq[0],
                 g_kv_lat[0], w_ukv[0], lambda_q1[0], lambda_k1[0], lambda_q2[0], lambda_k2[0],
                 g_diff_sub[0], w_out[0], g_post_mix[0], g_pre_ffn[0], w_router[0], b_router[0],
                 w_gate[0], w_up[0], w_down[0], ws_gate[0], ws_up[0], ws_down[0], g_post_ffn[0])
    return out[None]
```

```python
import functools
import math

import jax
import jax.numpy as jnp
import numpy as np
from jax import lax
from jax.experimental import pallas as pl
from jax.experimental.pallas import tpu as pltpu

F32 = jnp.float32
BF16 = jnp.bfloat16
U32 = jnp.uint32
I32 = jnp.int32

D_MODEL = 2048
CHUNK = 64
ROPE_THETA = 500000.0
EPS = 1e-6
LOG2E = 1.4426950408889634

MLA_HEADS = 8
MLA_Q_RANK = 768
MLA_KV_RANK = 512
MLA_NOPE = 128
MLA_ROPE = 64
MLA_V = 128
MLA_QK_PAD = 256

DIFF_HEADS = 8
DIFF_HEAD_DIM = 64
DIFF_ROT = DIFF_HEAD_DIM // 4
DIFF_W = DIFF_HEADS * 2 * DIFF_HEAD_DIM

N_EXPERTS = 64
TOP_K = 8
N_GROUPS = 8
GROUP_SIZE = N_EXPERTS // N_GROUPS
TOPK_GROUPS = 4
D_EXPERT = 512
ROUTED_SCALE = 2.5
LAMBDA_INIT = 0.8 - 0.6 * math.exp(-0.3 * 0)

LANES = 128
SUBLANES = 8
HALF = D_MODEL // 2
ROW_TILES = HALF // LANES

TM_PROJ = 256
TM_OUT = 2 * TM_PROJ
ATT_BQ = 4096
ATT_BK = 2048
ATT_KV_PER_Q = ATT_BQ // ATT_BK
UNIT_W = 512
V_EXT = MLA_V + 16
EXP_BLK = 256
BLKS_PER_STEP = 2
TD = 256
TR = 512
TN_ADA = 1024

VMEM_LIMIT = 56 * 1024 * 1024

assert ROW_TILES == SUBLANES


def _cparams(sem, vmem=VMEM_LIMIT, flags=None):
    return pltpu.CompilerParams(dimension_semantics=sem, vmem_limit_bytes=vmem, flags=flags)


def _dot(a, b):
    return jnp.dot(a, b, preferred_element_type=F32)


def _dot_nt(a, b):
    return lax.dot_general(a, b, (((1,), (1,)), ((), ())), preferred_element_type=F32)


def _rms(x):
    return x * lax.rsqrt(jnp.mean(x * x, axis=-1, keepdims=True) + EPS)


def _split_bf16(x):
    hi = x.astype(BF16)
    lo = (x - hi.astype(F32)).astype(BF16)
    return hi, lo


def _const_spec(shape):
    nd = len(shape)
    return pl.BlockSpec(shape, lambda *a: (0,) * nd, pipeline_mode=pl.Buffered(1))


def _ada_kernel(c_ref, w_ref, b_ref, o_ref):
    c = c_ref[...]
    a = c * jax.nn.sigmoid(c)
    a8 = jnp.broadcast_to(a, (SUBLANES, a.shape[1]))
    a_hi, a_lo = _split_bf16(a8)
    w_hi, w_lo = _split_bf16(w_ref[...])
    r = _dot(a_hi, w_hi) + _dot(a_lo, w_hi) + _dot(a_hi, w_lo)
    o_ref[...] = r[0:1] + b_ref[...]


def _ada(c, w, b):
    d, n = w.shape
    return pl.pallas_call(
        _ada_kernel,
        grid=(n // TN_ADA,),
        in_specs=[pl.BlockSpec((1, d), lambda j: (0, 0)),
                  pl.BlockSpec((d, TN_ADA), lambda j: (0, j)),
                  pl.BlockSpec((1, TN_ADA), lambda j: (0, j))],
        out_specs=pl.BlockSpec((1, TN_ADA), lambda j: (0, j)),
        out_shape=jax.ShapeDtypeStruct((1, n), F32),
        compiler_params=_cparams(("arbitrary",)),
        name="ada_mod",
    )(c, w, b)


def _tile_lanes(t, reps):
    return jnp.concatenate([t] * reps, axis=1)


def _rope_lanes(x, c, sa, sb, half):
    n = x.shape[1]
    return x * c + pltpu.roll(x, n - half, 1) * sa + pltpu.roll(x, half, 1) * sb


def _mla_heads(lat, kpe_raw, gq_ref, gkv_ref, wuq_ref, wuk_ref, wvt_ref, c, sa, sb, q_ref, k_ref, vt_ref):
    qn = (_rms(lat[:, :MLA_Q_RANK]) * gq_ref[...]).astype(BF16)
    kvn = (_rms(lat[:, MLA_Q_RANK:]) * gkv_ref[...]).astype(BF16)
    q = _dot(qn, wuq_ref[...])
    kn = _dot(kvn, wuk_ref[...])
    half = MLA_ROPE // 2
    qs = (MLA_NOPE + MLA_ROPE) ** -0.5 * LOG2E
    kpe = _rope_lanes(kpe_raw, c, sa, sb, half).astype(BF16)
    for h in range(MLA_HEADS):
        o = h * MLA_QK_PAD
        q_ref[:, o:o + LANES] = (q[:, o:o + LANES] * qs).astype(BF16)
        q_ref[:, o + LANES:o + 2 * LANES] = (
            _rope_lanes(q[:, o + LANES:o + 2 * LANES], c, sa, sb, half) * qs).astype(BF16)
        k_ref[:, o:o + LANES] = kn[:, h * MLA_NOPE:(h + 1) * MLA_NOPE].astype(BF16)
        k_ref[:, o + LANES:o + 2 * LANES] = kpe
    _store_vt_ext(vt_ref, _dot_nt(wvt_ref[...], kvn), MLA_HEADS)


def _inproj_kernel(x_ref, g_ref, sc_ref, sh_ref, wlat_ref, wqk_ref, wkpe_ref, wdvt_ref,
                   dc_ref, dsa_ref, dsb_ref, gq_ref, gkv_ref, wuq_ref, wuk_ref, wmvt_ref,
                   mc_ref, msa_ref, msb_ref,
                   dq_ref, dk_ref, dvt_ref, q_ref, k_ref, vt_ref):
    x = x_ref[...]
    h = _rms(x) * g_ref[...] * (1.0 + sc_ref[...]) + sh_ref[...]
    hb = h.astype(BF16)
    reps = DIFF_W // LANES
    c = _tile_lanes(dc_ref[...], reps)
    sa = _tile_lanes(dsa_ref[...], reps)
    sb = _tile_lanes(dsb_ref[...], reps)
    half = DIFF_ROT // 2
    q = _dot(hb, wqk_ref[:, 0:DIFF_W])
    dq_ref[...] = (_rope_lanes(q, c, sa, sb, half) * (DIFF_HEAD_DIM ** -0.5 * LOG2E)).astype(BF16)
    k = _dot(hb, wqk_ref[:, DIFF_W:])
    dk_ref[...] = _rope_lanes(k, c, sa, sb, half).astype(BF16)
    _store_vt_ext(dvt_ref, _dot_nt(wdvt_ref[...], hb), DIFF_HEADS)
    _mla_heads(_dot(hb, wlat_ref[...]), _dot(hb, wkpe_ref[...]), gq_ref, gkv_ref, wuq_ref, wuk_ref,
               wmvt_ref, mc_ref[...], msa_ref[...], msb_ref[...], q_ref, k_ref, vt_ref)


def _inproj(x, g, sc, sh, w_lat, w_qk, w_kpe, wdv_t, dtabs, gq, gkv, wuq_r, wuk_r, wmv_t, mtabs):
    s, d = x.shape
    tm = min(TM_PROJ, s)
    row = lambda n: pl.BlockSpec((tm, n), lambda i: (i, 0))
    col = lambda n: pl.BlockSpec((n, tm), lambda i: (0, i))
    const = lambda a: _const_spec(a.shape)
    hq = MLA_HEADS * MLA_QK_PAD
    return pl.pallas_call(
        _inproj_kernel,
        grid=(s // tm,),
        in_specs=[row(d), const(g), const(sc), const(sh), const(w_lat), const(w_qk), const(w_kpe),
                  const(wdv_t), row(LANES), row(LANES), row(LANES),
                  const(gq), const(gkv), const(wuq_r), const(wuk_r), const(wmv_t),
                  row(LANES), row(LANES), row(LANES)],
        out_specs=[row(DIFF_W), row(DIFF_W), col(DIFF_HEADS * V_EXT), row(hq), row(hq), col(MLA_HEADS * V_EXT)],
        out_shape=[jax.ShapeDtypeStruct((s, DIFF_W), BF16),
                   jax.ShapeDtypeStruct((s, DIFF_W), BF16),
                   jax.ShapeDtypeStruct((DIFF_HEADS * V_EXT, s), BF16),
                   jax.ShapeDtypeStruct((s, hq), BF16),
                   jax.ShapeDtypeStruct((s, hq), BF16),
                   jax.ShapeDtypeStruct((MLA_HEADS * V_EXT, s), BF16)],
        compiler_params=_cparams(("arbitrary",)),
        name="in_proj",
    )(x, g, sc, sh, w_lat, w_qk, w_kpe, wdv_t, *dtabs, gq, gkv, wuq_r, wuk_r, wmv_t, *mtabs)


def _pair_tables(nq):
    n = [ATT_KV_PER_Q * (i + 1) for i in range(nq)]
    qi = np.concatenate([np.full(c, i) for i, c in enumerate(n)]).astype(np.int32)
    kj = np.concatenate([np.arange(c) for c in n]).astype(np.int32)
    return jnp.asarray(qi), jnp.asarray(kj)


def _ones_rows(n):
    rows = V_EXT - MLA_V
    return (lax.broadcasted_iota(I32, (rows, n), 0) == 0).astype(BF16)


def _store_vt_ext(vt_ref, vt, heads):
    ones = _ones_rows(vt.shape[1])
    for h in range(heads):
        vt_ref[h * V_EXT:h * V_EXT + MLA_V, :] = vt[h * MLA_V:(h + 1) * MLA_V, :].astype(BF16)
        vt_ref[h * V_EXT + MLA_V:(h + 1) * V_EXT, :] = ones


def _attn_step(q_refs, k_ref, vt_ref, s_bufs, m_refs, acc_refs, rel):
    w = UNIT_W
    nk = k_ref.shape[0]
    per_kv = nk // w
    specs = []
    for u in range(q_refs[0].shape[0] // w):
        d = None if rel is None else u - rel * per_kv
        if d is None or d >= per_kv:
            specs.append((u, nk, None))
        elif d >= 0:
            specs.append((u, (d + 1) * w, d * w))
    units = [(si, sp) for sp in specs for si in range(len(q_refs))]

    def scores(n):
        si, (u, rows, off) = units[n]
        s = _dot_nt(k_ref[0:rows, :], q_refs[si][u * w:(u + 1) * w, :])
        if off is not None:
            kc = lax.broadcasted_iota(I32, (rows, w), 0) // CHUNK
            qc = (lax.broadcasted_iota(I32, (rows, w), 1) + off) // CHUNK
            s = jnp.where(kc <= qc, s, -jnp.inf)
        s_bufs[n % 2][0:rows, :] = s
        return jnp.max(s, axis=0, keepdims=True)

    cmax_next = scores(0)
    for n, (si, (u, rows, _)) in enumerate(units):
        cmax = cmax_next
        if n + 1 < len(units):
            cmax_next = scores(n + 1)
        m_ref, acc_ref = m_refs[si][u], acc_refs[si][u]
        m_prev = m_ref[...]
        m_new = jnp.maximum(m_prev, cmax)
        m_ref[...] = m_new
        acc_ref[...] = jnp.exp2(m_prev - m_new) * acc_ref[...]
        p = jnp.exp2(s_bufs[n % 2][0:rows, :] - m_new).astype(BF16)
        acc_ref[...] += _dot(vt_ref[:, 0:rows], p)


def _init_stats(ms, accs):
    for m, a in zip(ms, accs):
        m[...] = jnp.full(m.shape, -jnp.inf, F32)
        a[...] = jnp.zeros(a.shape, F32)


def _normalized(acc_ref):
    acc = acc_ref[...]
    return acc[0:MLA_V, :] / acc[MLA_V:MLA_V + 1, :]


def _attn_phases(qi_ref, kj_ref, init, step, finish):
    t = pl.program_id(1)
    rel = kj_ref[t] - ATT_KV_PER_Q * qi_ref[t]

    @pl.when(kj_ref[t] == 0)
    def _():
        init()

    @pl.when(rel < 0)
    def _():
        step(None)

    for r in range(ATT_KV_PER_Q):
        @pl.when(rel == r)
        def _(r=r):
            step(r)
            if r == ATT_KV_PER_Q - 1:
                finish()


def _unit_scratch(n_streams):
    nu = ATT_BQ // UNIT_W
    return ([pltpu.VMEM((ATT_BK, UNIT_W), F32)] * 2
            + [pltpu.VMEM((1, UNIT_W), F32)] * (nu * n_streams)
            + [pltpu.VMEM((V_EXT, UNIT_W), F32)] * (nu * n_streams))


def _split_units(scr, n_streams):
    nu = ATT_BQ // UNIT_W
    ms = [list(scr[s * nu:(s + 1) * nu]) for s in range(n_streams)]
    accs = [list(scr[(n_streams + s) * nu:(n_streams + s + 1) * nu]) for s in range(n_streams)]
    return ms, accs


def _mla_attn_kernel(qi_ref, kj_ref, q_ref, k_ref, vt_ref, o_ref, *scr):
    s_bufs, (ms, accs) = scr[:2], _split_units(scr[2:], 1)

    def finish():
        for u, a in enumerate(accs[0]):
            o_ref[u * UNIT_W:(u + 1) * UNIT_W, :] = _normalized(a).T.astype(o_ref.dtype)

    _attn_phases(qi_ref, kj_ref,
                 lambda: _init_stats(ms[0], accs[0]),
                 lambda rel: _attn_step([q_ref], k_ref, vt_ref, s_bufs, ms, accs, rel),
                 finish)


def _mla_attn(qc, kc, vt):
    s = qc.shape[0]
    assert s % ATT_BQ == 0
    qi, kj = _pair_tables(s // ATT_BQ)
    gs = pltpu.PrefetchScalarGridSpec(
        num_scalar_prefetch=2,
        grid=(MLA_HEADS, qi.shape[0]),
        in_specs=[pl.BlockSpec((ATT_BQ, MLA_QK_PAD), lambda h, t, qi, kj: (qi[t], h)),
                  pl.BlockSpec((ATT_BK, MLA_QK_PAD), lambda h, t, qi, kj: (kj[t], h)),
                  pl.BlockSpec((V_EXT, ATT_BK), lambda h, t, qi, kj: (h, kj[t]))],
        out_specs=pl.BlockSpec((ATT_BQ, MLA_V), lambda h, t, qi, kj: (qi[t], h)),
        scratch_shapes=_unit_scratch(1),
    )
    return pl.pallas_call(
        _mla_attn_kernel,
        grid_spec=gs,
        out_shape=jax.ShapeDtypeStruct((s, MLA_HEADS * MLA_V), BF16),
        compiler_params=_cparams(("arbitrary", "arbitrary")),
        name="mla_attn",
    )(qi, kj, qc, kc, vt)


def _diff_attn_kernel(qi_ref, kj_ref, q_ref, k_ref, vt_ref, lq1_ref, lk1_ref, lq2_ref, lk2_ref,
                      g_ref, o_ref, q0_scr, q1_scr, *scr):
    s_bufs, (ms, accs) = scr[:2], _split_units(scr[2:], 2)

    def init():
        q = q_ref[...]
        lane = lax.broadcasted_iota(I32, q.shape, 1)
        zero = jnp.zeros_like(q)
        q0_scr[...] = jnp.where(lane < DIFF_HEAD_DIM, q, zero)
        q1_scr[...] = jnp.where(lane >= DIFF_HEAD_DIM, q, zero)
        _init_stats(ms[0] + ms[1], accs[0] + accs[1])

    def finish():
        lam = (jnp.exp(jnp.sum(lq1_ref[...] * lk1_ref[...], axis=1, keepdims=True))
               - jnp.exp(jnp.sum(lq2_ref[...] * lk2_ref[...], axis=1, keepdims=True))
               + LAMBDA_INIT)
        for u, (a0, a1) in enumerate(zip(accs[0], accs[1])):
            o = _normalized(a0) - lam * _normalized(a1)
            o = o * lax.rsqrt(jnp.mean(o * o, axis=0, keepdims=True) + EPS)
            o_ref[u * UNIT_W:(u + 1) * UNIT_W, :] = (
                o * g_ref[...] * (1.0 - LAMBDA_INIT)).T.astype(o_ref.dtype)

    _attn_phases(qi_ref, kj_ref, init,
                 lambda rel: _attn_step([q0_scr, q1_scr], k_ref, vt_ref, s_bufs, ms, accs, rel),
                 finish)


def _diff_attn(dq, dk, dvt, lq1, lk1, lq2, lk2, g_col):
    s = dq.shape[0]
    assert s % ATT_BQ == 0
    qi, kj = _pair_tables(s // ATT_BQ)
    hw = 2 * DIFF_HEAD_DIM
    small = lambda a: pl.BlockSpec(a.shape, lambda h, t, qi, kj: (0, 0))
    gs = pltpu.PrefetchScalarGridSpec(
        num_scalar_prefetch=2,
        grid=(DIFF_HEADS, qi.shape[0]),
        in_specs=[pl.BlockSpec((ATT_BQ, hw), lambda h, t, qi, kj: (qi[t], h)),
                  pl.BlockSpec((ATT_BK, hw), lambda h, t, qi, kj: (kj[t], h)),
                  pl.BlockSpec((V_EXT, ATT_BK), lambda h, t, qi, kj: (h, kj[t])),
                  small(lq1), small(lk1), small(lq2), small(lk2), small(g_col)],
        out_specs=pl.BlockSpec((ATT_BQ, hw), lambda h, t, qi, kj: (qi[t], h)),
        scratch_shapes=[pltpu.VMEM((ATT_BQ, hw), BF16), pltpu.VMEM((ATT_BQ, hw), BF16)] + _unit_scratch(2),
    )
    return pl.pallas_call(
        _diff_attn_kernel,
        grid_spec=gs,
        out_shape=jax.ShapeDtypeStruct((s, DIFF_W), BF16),
        compiler_params=_cparams(("arbitrary", "arbitrary")),
        name="diff_attn",
    )(qi, kj, dq, dk, dvt, lq1, lk1, lq2, lk2, g_col)


def _pack_words(y):
    lo = pltpu.bitcast(y[:, :HALF].astype(BF16).astype(F32), U32) >> 16
    hi = pltpu.bitcast(y[:, HALF:].astype(BF16).astype(F32), U32) & jnp.uint32(0xFFFF0000)
    return lo | hi


def _store_packed(ref, words, rows):
    for s in range(ROW_TILES):
        ref[pl.ds(s, rows, stride=ROW_TILES), :] = words[:, s * LANES:(s + 1) * LANES]


def _load_packed(ref, rows):
    return jnp.concatenate(
        [ref[pl.ds(s, rows, stride=ROW_TILES), :] for s in range(ROW_TILES)], axis=1)


def _unpack_words(w):
    lo = pltpu.bitcast(w << 16, F32)
    hi = pltpu.bitcast(w & jnp.uint32(0xFFFF0000), F32)
    return lo, hi


def _outproj_kernel(om_ref, od_ref, x_ref, w_ref, gt_ref, gpost_ref, gpre_ref, sc_ref, sh_ref,
                    wr_ref, x1_ref, h2_ref, h2p_ref, lg_ref):
    n_sub = x_ref.shape[0] // TM_PROJ
    sub = lambda r, j: r.at[pl.ds(j * TM_PROJ, TM_PROJ)]
    w_hi, w_lo = _split_bf16(wr_ref[...])
    ne = w_hi.shape[0]
    w_hl = jnp.concatenate([w_hi, w_lo], axis=0)
    ys = [_dot(jnp.concatenate([sub(om_ref, j)[...], sub(od_ref, j)[...]], axis=1), w_ref[...])
          for j in range(n_sub)]
    for j, y in enumerate(ys):
        x1 = sub(x_ref, j)[...] + gt_ref[...] * (_rms(y) * gpost_ref[...])
        sub(x1_ref, j)[...] = x1
        h2 = _rms(x1) * gpre_ref[...] * (1.0 + sc_ref[...]) + sh_ref[...]
        h_hi, h_lo = _split_bf16(h2)
        sub(h2_ref, j)[...] = h_hi
        _store_packed(h2p_ref.at[pl.ds(j * TM_PROJ * ROW_TILES, TM_PROJ * ROW_TILES)], _pack_words(h2), TM_PROJ)
        a = _dot_nt(w_hl, h_hi)
        lg_ref[:, j * TM_PROJ:(j + 1) * TM_PROJ] = a[0:ne] + a[ne:] + _dot_nt(w_hi, h_lo)


def _outproj(om, od, x, w_out, gt, gpost, gpre, sc, sh, wr_t):
    s, d = x.shape
    tm = min(TM_OUT, s)
    row = lambda n: pl.BlockSpec((tm, n), lambda i: (i, 0))
    vec = _const_spec((1, d))
    return pl.pallas_call(
        _outproj_kernel,
        grid=(s // tm,),
        in_specs=[row(om.shape[1]), row(od.shape[1]), row(d), _const_spec(w_out.shape),
                  vec, vec, vec, vec, vec, _const_spec(wr_t.shape)],
        out_specs=[row(d), row(d),
                   pl.BlockSpec((tm * ROW_TILES, LANES), lambda i: (i, 0)),
                   pl.BlockSpec((N_EXPERTS, tm), lambda i: (0, i))],
        out_shape=[jax.ShapeDtypeStruct((s, d), F32),
                   jax.ShapeDtypeStruct((s, d), BF16),
                   jax.ShapeDtypeStruct((s * ROW_TILES, LANES), U32),
                   jax.ShapeDtypeStruct((N_EXPERTS, s), F32)],
        compiler_params=_cparams(("arbitrary",)),
        name="out_proj",
    )(om, od, x, w_out, gt, gpost, gpre, sc, sh, wr_t)


def _route_kernel(lg_ref, b_ref, eidx_ref, w_ref, rank_ref, cnt_ref, carry):
    step = pl.program_id(0)

    @pl.when(step == 0)
    def _():
        carry[...] = jnp.zeros(carry.shape, F32)

    tr = lg_ref.shape[1]
    ninf = jnp.float32(-jnp.inf)
    ie = lax.broadcasted_iota(I32, (GROUP_SIZE, tr), 0)
    rmax = lambda a: jnp.max(a, axis=0, keepdims=True)
    rmin = lambda a: jnp.min(a, axis=0, keepdims=True)
    rsum = lambda a: jnp.sum(a, axis=0, keepdims=True)

    sc, bi, gscore = [], [], []
    for g in range(N_GROUPS):
        lg = lg_ref[g * GROUP_SIZE:(g + 1) * GROUP_SIZE, :]
        s = jax.nn.sigmoid(lg)
        b = s + b_ref[g * GROUP_SIZE:(g + 1) * GROUP_SIZE, :]
        m1 = rmax(b)
        i1 = rmin(jnp.where(b == m1, ie, GROUP_SIZE))
        m2 = rmax(jnp.where(ie == i1, ninf, b))
        sc.append(s)
        bi.append(b)
        gscore.append(m1 + m2)

    gsel = [jnp.zeros((1, tr), I32) for _ in range(N_GROUPS)]
    for _ in range(TOPK_GROUPS):
        gm = functools.reduce(jnp.maximum, gscore)
        gi = functools.reduce(
            jnp.minimum, [jnp.where(gscore[g] == gm, g, N_GROUPS) for g in range(N_GROUPS)])
        for g in range(N_GROUPS):
            hit = gi == g
            gsel[g] = jnp.where(hit, 1, gsel[g])
            gscore[g] = jnp.where(hit, ninf, gscore[g])

    masked = [jnp.where(jnp.broadcast_to(gsel[g], (GROUP_SIZE, tr)) > 0, bi[g], ninf)
              for g in range(N_GROUPS)]
    sel = [jnp.zeros((GROUP_SIZE, tr), F32) for _ in range(N_GROUPS)]
    idxs, svals = [], []
    for _ in range(TOP_K):
        m = functools.reduce(jnp.maximum, [rmax(x) for x in masked])
        idx = functools.reduce(
            jnp.minimum,
            [rmin(jnp.where(masked[g] == m, ie + g * GROUP_SIZE, N_EXPERTS)) for g in range(N_GROUPS)])
        sv = jnp.zeros((1, tr), F32)
        for g in range(N_GROUPS):
            hit = (ie + g * GROUP_SIZE) == idx
            sv = sv + rsum(jnp.where(hit, sc[g], 0.0))
            masked[g] = jnp.where(hit, ninf, masked[g])
            sel[g] = jnp.where(hit, 1.0, sel[g])
        idxs.append(idx)
        svals.append(sv)
    tot = functools.reduce(lambda a, b: a + b, svals)

    sel2 = jnp.concatenate(sel, axis=0)
    before = (lax.broadcasted_iota(I32, (tr, tr), 0) < lax.broadcasted_iota(I32, (tr, tr), 1))
    rank2 = _dot(sel2.astype(BF16), before.astype(BF16)) + carry[:, 0:1]
    for k in range(TOP_K):
        rk = jnp.zeros((1, tr), F32)
        for g in range(N_GROUPS):
            hit = (ie + g * GROUP_SIZE) == idxs[k]
            rk = rk + rsum(jnp.where(hit, rank2[g * GROUP_SIZE:(g + 1) * GROUP_SIZE, :], 0.0))
        eidx_ref[k:k + 1, :] = idxs[k]
        w_ref[k:k + 1, :] = svals[k] / tot * ROUTED_SCALE
        rank_ref[k:k + 1, :] = rk.astype(I32)
    carry[...] = carry[...] + jnp.sum(sel2, axis=1, keepdims=True)
    cnt_ref[...] = carry[...]


def _route(lg_t, b_col):
    e, t = lg_t.shape
    tr = min(TR, t)
    tok = lambda: pl.BlockSpec((TOP_K, tr), lambda i: (0, i))
    return pl.pallas_call(
        _route_kernel,
        grid=(t // tr,),
        in_specs=[pl.BlockSpec((e, tr), lambda i: (0, i)), pl.BlockSpec((e, 1), lambda i: (0, 0))],
        out_specs=[tok(), tok(), tok(), pl.BlockSpec((e, LANES), lambda i: (0, 0))],
        out_shape=[jax.ShapeDtypeStruct((TOP_K, t), I32),
                   jax.ShapeDtypeStruct((TOP_K, t), F32),
                   jax.ShapeDtypeStruct((TOP_K, t), I32),
                   jax.ShapeDtypeStruct((e, LANES), F32)],
        scratch_shapes=[pltpu.VMEM((e, LANES), F32)],
        compiler_params=_cparams(("arbitrary",)),
        name="route",
    )(lg_t, b_col)


def _pos_kernel(pstart_ref, eidx_ref, rank_ref, pos_ref):
    e = eidx_ref[...]
    pos = rank_ref[...]
    for x in range(N_EXPERTS):
        pos = pos + jnp.where(e == x, pstart_ref[x], 0)
    pos_ref[...] = pos


def _positions(pstart, eidx, rank):
    k, t = eidx.shape
    tr = min(TR, t)
    gs = pltpu.PrefetchScalarGridSpec(
        num_scalar_prefetch=1,
        grid=(t // tr,),
        in_specs=[pl.BlockSpec((k, tr), lambda i, p: (0, i)), pl.BlockSpec((k, tr), lambda i, p: (0, i))],
        out_specs=pl.BlockSpec((k, tr), lambda i, p: (0, i)),
    )
    return pl.pallas_call(
        _pos_kernel, grid_spec=gs,
        out_shape=jax.ShapeDtypeStruct((k, t), I32),
        compiler_params=_cparams(("arbitrary",)),
        name="positions",
    )(pstart, eidx, rank)


def _row_copy(src, src_row, dst, dst_row, sem):
    return pltpu.make_async_copy(src.at[pl.ds(src_row * ROW_TILES, ROW_TILES)],
                                 dst.at[pl.ds(dst_row * ROW_TILES, ROW_TILES)], sem)


FILL_CHUNKS = (128, 64, 32, 16, 8, 4, 2, 1)


def _zero_fill(fs_ref, fl_ref, tail_ref, zbuf, xs_ref, sem, wait):
    def chunk(row, n):
        return pltpu.make_async_copy(zbuf.at[pl.ds(0, n * ROW_TILES)],
                                     xs_ref.at[pl.ds(row * ROW_TILES, n * ROW_TILES)], sem)

    def go(cp):
        if wait:
            cp.wait()
        else:
            cp.start()

    def per_expert(e, c):
        pad = fl_ref[e]
        row = fs_ref[e]
        for n in FILL_CHUNKS:
            @pl.when((pad & n) != 0)
            def _(row=row, n=n):
                go(chunk(row, n))
            row = row + (pad & n)
        return c

    lax.fori_loop(0, N_EXPERTS, per_expert, 0)

    def per_tail(b, c):
        for part in range(EXP_BLK // FILL_CHUNKS[0]):
            go(chunk(b * EXP_BLK + part * FILL_CHUNKS[0], FILL_CHUNKS[0]))
        return c

    lax.fori_loop(tail_ref[0], tail_ref[1], per_tail, 0)


def _dispatch_kernel(fs_ref, fl_ref, tail_ref, pos_ref, h_ref, xs_ref, zbuf, sem, fill_sem):
    td = pos_ref.shape[1]
    step = pl.program_id(0)

    @pl.when(step == 0)
    def _():
        zbuf[...] = jnp.zeros(zbuf.shape, zbuf.dtype)
        _zero_fill(fs_ref, fl_ref, tail_ref, zbuf, xs_ref, fill_sem, wait=False)

    def issue(t, c):
        for k in range(TOP_K):
            _row_copy(h_ref, t, xs_ref, pos_ref[k, t], sem).start(priority=k % 2)
        return c

    lax.fori_loop(0, td, issue, 0, unroll=4)
    for k in range(TOP_K):
        pltpu.make_async_copy(h_ref, xs_ref.at[pl.ds(0, td * ROW_TILES)], sem).wait()

    @pl.when(step == pl.num_programs(0) - 1)
    def _():
        _zero_fill(fs_ref, fl_ref, tail_ref, zbuf, xs_ref, fill_sem, wait=True)


def _dispatch(fill_start, fill_len, tail, pos, h2p, m_pad):
    k, t = pos.shape
    td = min(TD, t)
    gs = pltpu.PrefetchScalarGridSpec(
        num_scalar_prefetch=3,
        grid=(t // td,),
        in_specs=[pl.BlockSpec((k, td), lambda i, *_: (0, i), memory_space=pltpu.SMEM),
                  pl.BlockSpec((td * ROW_TILES, LANES), lambda i, *_: (i, 0))],
        out_specs=pl.BlockSpec(memory_space=pl.ANY),
        scratch_shapes=[pltpu.VMEM((FILL_CHUNKS[0] * ROW_TILES, LANES), U32),
                        pltpu.SemaphoreType.DMA, pltpu.SemaphoreType.DMA],
    )
    return pl.pallas_call(
        _dispatch_kernel,
        grid_spec=gs,
        out_shape=jax.ShapeDtypeStruct((m_pad * ROW_TILES, LANES), U32),
        compiler_params=_cparams(("arbitrary",)),
        name="dispatch",
    )(fill_start, fill_len, tail, pos, h2p)


def _weight_copies(w_hbm, w_buf, sem, e, s):
    return [pltpu.make_async_copy(h.at[e], b.at[s], sem.at[s]) for h, b in zip(w_hbm, w_buf)]


def _experts_kernel(be_ref, nu_ref, fresh_ref, nxt_ref, slot_ref, x_ref, wg_hbm, wu_hbm, wd_hbm, y_ref,
                    wg_buf, wu_buf, wd_buf, wg_s, wu_s, wd_s, sem):
    step = pl.program_id(0)
    b0 = step * BLKS_PER_STEP
    blk_rows = x_ref.shape[0] // BLKS_PER_STEP
    w_hbm = (wg_hbm, wu_hbm, wd_hbm)
    w_buf = (wg_buf, wu_buf, wd_buf)

    def maybe_weights(b):
        @pl.when(fresh_ref[b] == 1)
        def _():
            s = slot_ref[b]

            @pl.when(b == 0)
            def _():
                for c in _weight_copies(w_hbm, w_buf, sem, be_ref[0], 0):
                    c.start()

            for c in _weight_copies(w_hbm, w_buf, sem, be_ref[b], s):
                c.wait()

            @pl.when(nxt_ref[b] >= 0)
            def _():
                for c in _weight_copies(w_hbm, w_buf, sem, nxt_ref[b], 1 - s):
                    c.start()

            wg_s[...] = wg_buf[s].astype(BF16)
            wu_s[...] = wu_buf[s].astype(BF16)
            wd_s[...] = wd_buf[s].astype(BF16)

    def swiglu(first, n):
        x_blk = x_ref.at[pl.ds(first * blk_rows, n * blk_rows)]
        y_blk = y_ref.at[pl.ds(first * blk_rows, n * blk_rows)]
        rows = n * blk_rows // ROW_TILES
        xa, xb = _unpack_words(_load_packed(x_blk, rows))
        xa = xa.astype(BF16)
        xb = xb.astype(BF16)
        g = _dot(xa, wg_s[0:HALF, :]) + _dot(xb, wg_s[HALF:, :])
        u = _dot(xa, wu_s[0:HALF, :]) + _dot(xb, wu_s[HALF:, :])
        a = (g * jax.nn.sigmoid(g) * u).astype(BF16)
        _store_packed(y_blk, _pack_words(_dot(a, wd_s[...])), rows)

    def zero(first, n):
        y_ref[pl.ds(first * blk_rows, n * blk_rows), :] = jnp.zeros((n * blk_rows, LANES), y_ref.dtype)

    n_used = jnp.clip(nu_ref[0] - b0, 0, BLKS_PER_STEP)
    same = jnp.logical_and(n_used == BLKS_PER_STEP, be_ref[b0] == be_ref[b0 + 1])

    @pl.when(same)
    def _():
        maybe_weights(b0)
        swiglu(0, BLKS_PER_STEP)

    @pl.when(jnp.logical_and(jnp.logical_not(same), n_used > 0))
    def _():
        maybe_weights(b0)
        swiglu(0, 1)

        @pl.when(n_used > 1)
        def _():
            maybe_weights(b0 + 1)
            swiglu(1, 1)

        @pl.when(n_used == 1)
        def _():
            zero(1, 1)

    @pl.when(n_used == 0)
    def _():
        zero(0, BLKS_PER_STEP)


def _experts(blk_e, nused, fresh, nxt_e, slot, xs, w_gate, w_up, w_down, blk):
    n_blk = blk_e.shape[0]
    assert n_blk % BLKS_PER_STEP == 0 and BLKS_PER_STEP == 2
    ne, d, de = w_gate.shape
    rows = BLKS_PER_STEP * blk * ROW_TILES
    xmap = lambda i, be, nu, fr, nx, sl: (jnp.minimum(i, (nu[0] - 1) // BLKS_PER_STEP), 0)
    hbm = pl.BlockSpec(memory_space=pl.ANY)
    gs = pltpu.PrefetchScalarGridSpec(
        num_scalar_prefetch=5,
        grid=(n_blk // BLKS_PER_STEP,),
        in_specs=[pl.BlockSpec((rows, LANES), xmap), hbm, hbm, hbm],
        out_specs=pl.BlockSpec((rows, LANES), lambda i, *_: (i, 0)),
        scratch_shapes=[pltpu.VMEM((2, d, de), F32), pltpu.VMEM((2, d, de), F32), pltpu.VMEM((2, de, d), F32),
                        pltpu.VMEM((d, de), BF16), pltpu.VMEM((d, de), BF16), pltpu.VMEM((de, d), BF16),
                        pltpu.SemaphoreType.DMA((2,))],
    )
    return pl.pallas_call(
        _experts_kernel, grid_spec=gs,
        out_shape=jax.ShapeDtypeStruct(xs.shape, U32),
        compiler_params=_cparams(("arbitrary",)),
        name="experts",
    )(blk_e, nused, fresh, nxt_e, slot, xs, w_gate, w_up, w_down)


def _combine_kernel(pos_ref, wt_ref, ys_ref, h_ref, x1_ref, wsg_ref, wsu_ref, wsd_ref,
                    gt_ref, gpost_ref, o_ref, gbuf, sem):
    td = pos_ref.shape[1]

    def issue(t, c):
        for k in range(TOP_K):
            _row_copy(ys_ref, pos_ref[k, t], gbuf.at[k], t, sem).start(priority=k % 2)
        return c

    lax.fori_loop(0, td, issue, 0, unroll=4)

    hb = h_ref[...]
    g = _dot(hb, wsg_ref[...])
    u = _dot(hb, wsu_ref[...])
    y = _dot((g * jax.nn.sigmoid(g) * u).astype(BF16), wsd_ref[...])

    wpad = jnp.concatenate([wt_ref[...], jnp.zeros((LANES - TOP_K, td), F32)], axis=0)
    wcol = jnp.concatenate([wpad[:, j * LANES:(j + 1) * LANES].T for j in range(td // LANES)], axis=0)

    for k in range(TOP_K):
        pltpu.make_async_copy(ys_ref.at[pl.ds(0, td * ROW_TILES)], gbuf.at[k], sem).wait()
    lo = jnp.zeros((td, HALF), F32)
    hi = jnp.zeros((td, HALF), F32)
    for k in range(TOP_K):
        a, b = _unpack_words(_load_packed(gbuf.at[k], td))
        wk = wcol[:, k:k + 1]
        lo = lo + wk * a
        hi = hi + wk * b
    y = y + jnp.concatenate([lo, hi], axis=1)
    o_ref[...] = x1_ref[...] + gt_ref[...] * (_rms(y) * gpost_ref[...])


def _combine(pos, wts, ys, h2, x1, wsg, wsu, wsd, gt, gpost):
    k, t = pos.shape
    d = x1.shape[1]
    td = min(TD, t)
    assert td % LANES == 0
    row = lambda n: pl.BlockSpec((td, n), lambda i: (i, 0))
    vec = _const_spec((1, d))
    return pl.pallas_call(
        _combine_kernel,
        grid=(t // td,),
        in_specs=[pl.BlockSpec((k, td), lambda i: (0, i), memory_space=pltpu.SMEM),
                  pl.BlockSpec((k, td), lambda i: (0, i)),
                  pl.BlockSpec(memory_space=pl.ANY),
                  row(d), row(d), _const_spec(wsg.shape), _const_spec(wsu.shape), _const_spec(wsd.shape),
                  vec, vec],
        out_specs=row(d),
        out_shape=jax.ShapeDtypeStruct((t, d), F32),
        scratch_shapes=[pltpu.VMEM((k, td * ROW_TILES, LANES), U32), pltpu.SemaphoreType.DMA],
        compiler_params=_cparams(("arbitrary",)),
        name="combine",
    )(pos, wts, ys, h2, x1, wsg, wsu, wsd, gt, gpost)


def _rope_tables(seq, dim):
    pos = jnp.arange(seq, dtype=F32)
    inv = ROPE_THETA ** (-jnp.arange(0, dim, 2, dtype=F32) / dim)
    ang = pos[:, None] * inv[None, :]
    return jnp.cos(ang), jnp.sin(ang)


def _lane_tables(seq, dim, period):
    cos, sin = _rope_tables(seq, dim)
    half = dim // 2
    cos = jnp.tile(cos, (1, LANES // half))
    sin = jnp.tile(sin, (1, LANES // half))
    r = jnp.arange(LANES) % period
    c = jnp.where(r < dim, cos, 1.0)
    sa = jnp.where(r < half, -sin, 0.0)
    sb = jnp.where((r >= half) & (r < dim), sin, 0.0)
    return c, sa, sb


def _moe(h2, h2p, x1, lg_t, b_router, w_gate, w_up, w_down, wsg, wsu, wsd, gt_f, g_post_ffn):
    t = h2.shape[0]
    blk = EXP_BLK
    m_pad = t * TOP_K + N_EXPERTS * blk
    eidx, wts, rank, cnt = _route(lg_t, b_router.reshape(N_EXPERTS, 1))
    counts = cnt[:, 0].astype(I32)
    padded = (counts + blk - 1) // blk * blk
    pend = jnp.cumsum(padded)
    pstart = pend - padded
    n_blk = m_pad // blk
    blk_start = jnp.arange(n_blk, dtype=I32) * blk
    blk_e = jnp.minimum(jnp.sum(pend[None, :] <= blk_start[:, None], axis=1), N_EXPERTS - 1).astype(I32)
    nused = (pend[-1:] // blk).astype(I32)
    pos = _positions(pstart.astype(I32), eidx, rank)
    tail = jnp.concatenate([nused, jnp.full((1,), n_blk, I32)])
    xs = _dispatch((pstart + counts).astype(I32), (padded - counts).astype(I32), tail, pos, h2p, m_pad)
    idx = jnp.arange(n_blk, dtype=I32)
    fresh = (idx < nused[0]) & (blk_e != jnp.concatenate([jnp.full((1,), -1, I32), blk_e[:-1]]))
    slot = ((jnp.cumsum(fresh.astype(I32)) - 1) % 2).astype(I32)
    first_at_or_after = lax.cummin(jnp.where(fresh, idx, n_blk)[::-1])[::-1]
    nxt_idx = jnp.concatenate([first_at_or_after[1:], jnp.full((1,), n_blk, I32)])
    nxt_e = jnp.where(nxt_idx < n_blk, blk_e[jnp.minimum(nxt_idx, n_blk - 1)], -1).astype(I32)
    ys = _experts(blk_e, nused, fresh.astype(I32), nxt_e, slot, xs, w_gate, w_up, w_down, blk)
    return _combine(pos, wts, ys, h2, x1, wsg, wsu, wsd, gt_f, g_post_ffn)


def _layer(x, c, w_ada, b_ada, g_pre_mix, w_in, g_q_lat, w_uq, g_kv_lat, w_ukv,
           lq1, lk1, lq2, lk2, g_diff_sub, w_out, g_post_mix, g_pre_ffn, w_router, b_router,
           w_gate, w_up, w_down, ws_gate, ws_up, ws_down, g_post_ffn):
    s, d = x.shape
    row = lambda a: a.reshape(1, -1)

    mod = _ada(c, w_ada, row(b_ada))
    sh_a, sc_a, gt_a, sh_f, sc_f, gt_f = [mod[:, i * d:(i + 1) * d] for i in range(6)]

    o1 = MLA_Q_RANK + MLA_KV_RANK
    o2 = o1 + MLA_ROPE
    o3 = o2 + 2 * DIFF_W
    w_kpe = jnp.pad(w_in[:, o1:o2], ((0, 0), (0, LANES - MLA_ROPE))).astype(BF16)
    wdv_t = lax.optimization_barrier(w_in[:, o3:]).T.astype(BF16)
    wq = w_uq.reshape(MLA_Q_RANK, MLA_HEADS, MLA_NOPE + MLA_ROPE)
    wq = jnp.pad(wq, ((0, 0), (0, 0), (0, MLA_QK_PAD - MLA_NOPE - MLA_ROPE)))
    wuq_r = wq.reshape(MLA_Q_RANK, MLA_HEADS * MLA_QK_PAD).astype(BF16)
    wkv = w_ukv.reshape(MLA_KV_RANK, MLA_HEADS, MLA_NOPE + MLA_V)
    wuk_r = wkv[:, :, :MLA_NOPE].reshape(MLA_KV_RANK, -1).astype(BF16)
    wmv_t = wkv[:, :, MLA_NOPE:].reshape(MLA_KV_RANK, -1).T.astype(BF16)
    dq, dk, dvt, qc, kc, vt = _inproj(
        x, row(g_pre_mix), sc_a, sh_a, w_in[:, :o1].astype(BF16), w_in[:, o2:o3].astype(BF16), w_kpe, wdv_t,
        _lane_tables(s, DIFF_ROT, DIFF_HEAD_DIM), row(g_q_lat), row(g_kv_lat), wuq_r, wuk_r, wmv_t,
        _lane_tables(s, MLA_ROPE, MLA_ROPE))

    o_mla = _mla_attn(qc, kc, vt)
    o_diff = _diff_attn(dq, dk, dvt, row(lq1), row(lk1), row(lq2), row(lk2), g_diff_sub.reshape(-1, 1))

    x1, h2, h2p, lg_t = _outproj(o_mla, o_diff, x, w_out.astype(BF16), gt_a, row(g_post_mix),
                                 row(g_pre_ffn), sc_f, sh_f, w_router.T)
    return _moe(h2, h2p, x1, lg_t, b_router, w_gate, w_up, w_down,
                ws_gate.astype(BF16), ws_up.astype(BF16), ws_down.astype(BF16), gt_f, row(g_post_ffn))


def kernel(x, c, w_ada, b_ada, g_pre_mix, w_in, g_q_lat, w_uq, g_kv_lat, w_ukv, lambda_q1, lambda_k1, lambda_q2, lambda_k2, g_diff_sub, w_out, g_post_mix, g_pre_ffn, w_router, b_router, w_gate, w_up, w_down, ws_gate, ws_up, ws_down, g_post_ffn):
    assert x.shape[0] == 1 and w_ada.shape[0] == 1
    out = _layer(x[0], c, w_ada[0], b_ada[0], g_pre_mix[0], w_in[0], g_q_lat[0], w_uq[0],
                 g_kv_lat[0], w_ukv[0], lambda_q1[0], lambda_k1[0], lambda_q2[0], lambda_k2[0],
                 g_diff_sub[0], w_out[0], g_post_mix[0], g_pre_ffn[0], w_router[0], b_router[0],
                 w_gate[0], w_up[0], w_down[0], ws_gate[0], ws_up[0], ws_down[0], g_post_ffn[0])
    return out[None]
```

```python
import functools
import math

import jax
import jax.numpy as jnp
import numpy as np
from jax import lax
from jax.experimental import pallas as pl
from jax.experimental.pallas import tpu as pltpu

F32 = jnp.float32
BF16 = jnp.bfloat16
U32 = jnp.uint32
I32 = jnp.int32

D_MODEL = 2048
CHUNK = 64
ROPE_THETA = 500000.0
EPS = 1e-6
LOG2E = 1.4426950408889634

MLA_HEADS = 8
MLA_Q_RANK = 768
MLA_KV_RANK = 512
MLA_NOPE = 128
MLA_ROPE = 64
MLA_V = 128
MLA_QK_PAD = 256

DIFF_HEADS = 8
DIFF_HEAD_DIM = 64
DIFF_ROT = DIFF_HEAD_DIM // 4
DIFF_W = DIFF_HEADS * 2 * DIFF_HEAD_DIM

N_EXPERTS = 64
TOP_K = 8
N_GROUPS = 8
GROUP_SIZE = N_EXPERTS // N_GROUPS
TOPK_GROUPS = 4
D_EXPERT = 512
ROUTED_SCALE = 2.5
LAMBDA_INIT = 0.8 - 0.6 * math.exp(-0.3 * 0)

LANES = 128
SUBLANES = 8
HALF = D_MODEL // 2
ROW_TILES = HALF // LANES

TM_PROJ = 256
TM_OUT = 2 * TM_PROJ
ATT_BQ = 4096
MLA_BK = 2048
DIFF_BK = 1024
UNIT_W = 512
V_EXT = MLA_V + 16
EXP_BLK = 256
BLKS_PER_STEP = 2
TD = 256
TR = 512
TN_ADA = 1024

VMEM_LIMIT = 56 * 1024 * 1024

assert ROW_TILES == SUBLANES


def _cparams(sem, vmem=VMEM_LIMIT, flags=None):
    return pltpu.CompilerParams(dimension_semantics=sem, vmem_limit_bytes=vmem, flags=flags)


def _dot(a, b):
    return jnp.dot(a, b, preferred_element_type=F32)


def _dot_nt(a, b):
    return lax.dot_general(a, b, (((1,), (1,)), ((), ())), preferred_element_type=F32)


def _rms(x):
    return x * lax.rsqrt(jnp.mean(x * x, axis=-1, keepdims=True) + EPS)


def _split_bf16(x):
    hi = x.astype(BF16)
    lo = (x - hi.astype(F32)).astype(BF16)
    return hi, lo


def _const_spec(shape):
    nd = len(shape)
    return pl.BlockSpec(shape, lambda *a: (0,) * nd, pipeline_mode=pl.Buffered(1))


def _ada_kernel(c_ref, w_ref, b_ref, o_ref):
    c = c_ref[...]
    a = c * jax.nn.sigmoid(c)
    a8 = jnp.broadcast_to(a, (SUBLANES, a.shape[1]))
    a_hi, a_lo = _split_bf16(a8)
    w_hi, w_lo = _split_bf16(w_ref[...])
    r = _dot(a_hi, w_hi) + _dot(a_lo, w_hi) + _dot(a_hi, w_lo)
    o_ref[...] = r[0:1] + b_ref[...]


def _ada(c, w, b):
    d, n = w.shape
    return pl.pallas_call(
        _ada_kernel,
        grid=(n // TN_ADA,),
        in_specs=[pl.BlockSpec((1, d), lambda j: (0, 0)),
                  pl.BlockSpec((d, TN_ADA), lambda j: (0, j)),
                  pl.BlockSpec((1, TN_ADA), lambda j: (0, j))],
        out_specs=pl.BlockSpec((1, TN_ADA), lambda j: (0, j)),
        out_shape=jax.ShapeDtypeStruct((1, n), F32),
        compiler_params=_cparams(("arbitrary",)),
        name="ada_mod",
    )(c, w, b)


def _tile_lanes(t, reps):
    return jnp.concatenate([t] * reps, axis=1)


def _rope_lanes(x, c, sa, sb, half):
    n = x.shape[1]
    return x * c + pltpu.roll(x, n - half, 1) * sa + pltpu.roll(x, half, 1) * sb


def _mla_heads(lat, kpe_raw, gq_ref, gkv_ref, wuq_ref, wuk_ref, wvt_ref, c, sa, sb, q_ref, k_ref, vt_ref):
    qn = (_rms(lat[:, :MLA_Q_RANK]) * gq_ref[...]).astype(BF16)
    kvn = (_rms(lat[:, MLA_Q_RANK:]) * gkv_ref[...]).astype(BF16)
    q = _dot(qn, wuq_ref[...])
    kn = _dot(kvn, wuk_ref[...])
    half = MLA_ROPE // 2
    qs = (MLA_NOPE + MLA_ROPE) ** -0.5 * LOG2E
    kpe = _rope_lanes(kpe_raw, c, sa, sb, half).astype(BF16)
    for h in range(MLA_HEADS):
        o = h * MLA_QK_PAD
        q_ref[:, o:o + LANES] = (q[:, o:o + LANES] * qs).astype(BF16)
        q_ref[:, o + LANES:o + 2 * LANES] = (
            _rope_lanes(q[:, o + LANES:o + 2 * LANES], c, sa, sb, half) * qs).astype(BF16)
        k_ref[:, o:o + LANES] = kn[:, h * MLA_NOPE:(h + 1) * MLA_NOPE].astype(BF16)
        k_ref[:, o + LANES:o + 2 * LANES] = kpe
    _store_vt_ext(vt_ref, _dot_nt(wvt_ref[...], kvn), MLA_HEADS)


def _inproj_kernel(x_ref, g_ref, sc_ref, sh_ref, wlat_ref, wqk_ref, wkpe_ref, wdvt_ref,
                   dc_ref, dsa_ref, dsb_ref, gq_ref, gkv_ref, wuq_ref, wuk_ref, wmvt_ref,
                   mc_ref, msa_ref, msb_ref,
                   dq_ref, dk_ref, dvt_ref, q_ref, k_ref, vt_ref):
    x = x_ref[...]
    h = _rms(x) * g_ref[...] * (1.0 + sc_ref[...]) + sh_ref[...]
    hb = h.astype(BF16)
    reps = DIFF_W // LANES
    c = _tile_lanes(dc_ref[...], reps)
    sa = _tile_lanes(dsa_ref[...], reps)
    sb = _tile_lanes(dsb_ref[...], reps)
    half = DIFF_ROT // 2
    q = _dot(hb, wqk_ref[:, 0:DIFF_W])
    dq_ref[...] = (_rope_lanes(q, c, sa, sb, half) * (DIFF_HEAD_DIM ** -0.5 * LOG2E)).astype(BF16)
    k = _dot(hb, wqk_ref[:, DIFF_W:])
    dk_ref[...] = _rope_lanes(k, c, sa, sb, half).astype(BF16)
    _store_vt_ext(dvt_ref, _dot_nt(wdvt_ref[...], hb), DIFF_HEADS)
    _mla_heads(_dot(hb, wlat_ref[...]), _dot(hb, wkpe_ref[...]), gq_ref, gkv_ref, wuq_ref, wuk_ref,
               wmvt_ref, mc_ref[...], msa_ref[...], msb_ref[...], q_ref, k_ref, vt_ref)


def _inproj(x, g, sc, sh, w_lat, w_qk, w_kpe, wdv_t, dtabs, gq, gkv, wuq_r, wuk_r, wmv_t, mtabs):
    s, d = x.shape
    tm = min(TM_PROJ, s)
    row = lambda n: pl.BlockSpec((tm, n), lambda i: (i, 0))
    col = lambda n: pl.BlockSpec((n, tm), lambda i: (0, i))
    const = lambda a: _const_spec(a.shape)
    hq = MLA_HEADS * MLA_QK_PAD
    return pl.pallas_call(
        _inproj_kernel,
        grid=(s // tm,),
        in_specs=[row(d), const(g), const(sc), const(sh), const(w_lat), const(w_qk), const(w_kpe),
                  const(wdv_t), row(LANES), row(LANES), row(LANES),
                  const(gq), const(gkv), const(wuq_r), const(wuk_r), const(wmv_t),
                  row(LANES), row(LANES), row(LANES)],
        out_specs=[row(DIFF_W), row(DIFF_W), col(DIFF_HEADS * V_EXT), row(hq), row(hq), col(MLA_HEADS * V_EXT)],
        out_shape=[jax.ShapeDtypeStruct((s, DIFF_W), BF16),
                   jax.ShapeDtypeStruct((s, DIFF_W), BF16),
                   jax.ShapeDtypeStruct((DIFF_HEADS * V_EXT, s), BF16),
                   jax.ShapeDtypeStruct((s, hq), BF16),
                   jax.ShapeDtypeStruct((s, hq), BF16),
                   jax.ShapeDtypeStruct((MLA_HEADS * V_EXT, s), BF16)],
        compiler_params=_cparams(("arbitrary",)),
        name="in_proj",
    )(x, g, sc, sh, w_lat, w_qk, w_kpe, wdv_t, *dtabs, gq, gkv, wuq_r, wuk_r, wmv_t, *mtabs)


def _pair_tables(nq, kv_per_q):
    n = [kv_per_q * (i + 1) for i in range(nq)]
    qi = np.concatenate([np.full(c, i) for i, c in enumerate(n)]).astype(np.int32)
    kj = np.concatenate([np.arange(c) for c in n]).astype(np.int32)
    return jnp.asarray(qi), jnp.asarray(kj)


def _ones_rows(n):
    rows = V_EXT - MLA_V
    return (lax.broadcasted_iota(I32, (rows, n), 0) == 0).astype(BF16)


def _store_vt_ext(vt_ref, vt, heads):
    ones = _ones_rows(vt.shape[1])
    for h in range(heads):
        vt_ref[h * V_EXT:h * V_EXT + MLA_V, :] = vt[h * MLA_V:(h + 1) * MLA_V, :].astype(BF16)
        vt_ref[h * V_EXT + MLA_V:(h + 1) * V_EXT, :] = ones


def _attn_step(q_refs, k_ref, vt_ref, s_bufs, m_refs, acc_refs, rel):
    w = UNIT_W
    nk = k_ref.shape[0]
    per_kv = nk // w
    specs = []
    for u in range(q_refs[0].shape[0] // w):
        d = None if rel is None else u - rel * per_kv
        if d is None or d >= per_kv:
            specs.append((u, nk, None))
        elif d >= 0:
            specs.append((u, (d + 1) * w, d * w))
    units = [(si, sp) for sp in specs for si in range(len(q_refs))]

    def scores(n):
        si, (u, rows, off) = units[n]
        s = _dot_nt(k_ref[0:rows, :], q_refs[si][u * w:(u + 1) * w, :])
        if off is not None:
            kc = lax.broadcasted_iota(I32, (rows, w), 0) // CHUNK
            qc = (lax.broadcasted_iota(I32, (rows, w), 1) + off) // CHUNK
            s = jnp.where(kc <= qc, s, -jnp.inf)
        s_bufs[n % 2][0:rows, :] = s
        return jnp.max(s, axis=0, keepdims=True)

    cmax_next = scores(0)
    for n, (si, (u, rows, _)) in enumerate(units):
        cmax = cmax_next
        if n + 1 < len(units):
            cmax_next = scores(n + 1)
        m_ref, acc_ref = m_refs[si][u], acc_refs[si][u]
        m_prev = m_ref[...]
        m_new = jnp.maximum(m_prev, cmax)
        m_ref[...] = m_new
        acc_ref[...] = jnp.exp2(m_prev - m_new) * acc_ref[...]
        p = jnp.exp2(s_bufs[n % 2][0:rows, :] - m_new).astype(BF16)
        acc_ref[...] += _dot(vt_ref[:, 0:rows], p)


def _init_stats(ms, accs):
    for m, a in zip(ms, accs):
        m[...] = jnp.full(m.shape, -jnp.inf, F32)
        a[...] = jnp.zeros(a.shape, F32)


def _normalized(acc_ref):
    acc = acc_ref[...]
    return acc[0:MLA_V, :] / acc[MLA_V:MLA_V + 1, :]


def _attn_phases(qi_ref, kj_ref, kv_per_q, init, step, finish):
    t = pl.program_id(1)
    rel = kj_ref[t] - kv_per_q * qi_ref[t]

    @pl.when(kj_ref[t] == 0)
    def _():
        init()

    @pl.when(rel < 0)
    def _():
        step(None)

    for r in range(kv_per_q):
        @pl.when(rel == r)
        def _(r=r):
            step(r)
            if r == kv_per_q - 1:
                finish()


def _unit_scratch(n_streams, bk):
    nu = ATT_BQ // UNIT_W
    return ([pltpu.VMEM((bk, UNIT_W), F32)] * 2
            + [pltpu.VMEM((1, UNIT_W), F32)] * (nu * n_streams)
            + [pltpu.VMEM((V_EXT, UNIT_W), F32)] * (nu * n_streams))


def _split_units(scr, n_streams):
    nu = ATT_BQ // UNIT_W
    ms = [list(scr[s * nu:(s + 1) * nu]) for s in range(n_streams)]
    accs = [list(scr[(n_streams + s) * nu:(n_streams + s + 1) * nu]) for s in range(n_streams)]
    return ms, accs


def _mla_attn_kernel(qi_ref, kj_ref, q_ref, k_ref, vt_ref, o_ref, *scr):
    s_bufs, (ms, accs) = scr[:2], _split_units(scr[2:], 1)

    def finish():
        for u, a in enumerate(accs[0]):
            o_ref[u * UNIT_W:(u + 1) * UNIT_W, :] = _normalized(a).T.astype(o_ref.dtype)

    _attn_phases(qi_ref, kj_ref, ATT_BQ // k_ref.shape[0],
                 lambda: _init_stats(ms[0], accs[0]),
                 lambda rel: _attn_step([q_ref], k_ref, vt_ref, s_bufs, ms, accs, rel),
                 finish)


def _mla_attn(qc, kc, vt):
    s = qc.shape[0]
    bk = MLA_BK
    assert s % ATT_BQ == 0
    qi, kj = _pair_tables(s // ATT_BQ, ATT_BQ // bk)
    gs = pltpu.PrefetchScalarGridSpec(
        num_scalar_prefetch=2,
        grid=(MLA_HEADS, qi.shape[0]),
        in_specs=[pl.BlockSpec((ATT_BQ, MLA_QK_PAD), lambda h, t, qi, kj: (qi[t], h)),
                  pl.BlockSpec((bk, MLA_QK_PAD), lambda h, t, qi, kj: (kj[t], h)),
                  pl.BlockSpec((V_EXT, bk), lambda h, t, qi, kj: (h, kj[t]))],
        out_specs=pl.BlockSpec((ATT_BQ, MLA_V), lambda h, t, qi, kj: (qi[t], h)),
        scratch_shapes=_unit_scratch(1, bk),
    )
    return pl.pallas_call(
        _mla_attn_kernel,
        grid_spec=gs,
        out_shape=jax.ShapeDtypeStruct((s, MLA_HEADS * MLA_V), BF16),
        compiler_params=_cparams(("arbitrary", "arbitrary")),
        name="mla_attn",
    )(qi, kj, qc, kc, vt)


def _diff_attn_kernel(qi_ref, kj_ref, q_ref, k_ref, vt_ref, lq1_ref, lk1_ref, lq2_ref, lk2_ref,
                      g_ref, o_ref, q0_scr, q1_scr, *scr):
    s_bufs, (ms, accs) = scr[:2], _split_units(scr[2:], 2)

    def init():
        q = q_ref[...]
        lane = lax.broadcasted_iota(I32, q.shape, 1)
        zero = jnp.zeros_like(q)
        q0_scr[...] = jnp.where(lane < DIFF_HEAD_DIM, q, zero)
        q1_scr[...] = jnp.where(lane >= DIFF_HEAD_DIM, q, zero)
        _init_stats(ms[0] + ms[1], accs[0] + accs[1])

    def finish():
        lam = (jnp.exp(jnp.sum(lq1_ref[...] * lk1_ref[...], axis=1, keepdims=True))
               - jnp.exp(jnp.sum(lq2_ref[...] * lk2_ref[...], axis=1, keepdims=True))
               + LAMBDA_INIT)
        for u, (a0, a1) in enumerate(zip(accs[0], accs[1])):
            o = _normalized(a0) - lam * _normalized(a1)
            o = o * lax.rsqrt(jnp.mean(o * o, axis=0, keepdims=True) + EPS)
            o_ref[u * UNIT_W:(u + 1) * UNIT_W, :] = (
                o * g_ref[...] * (1.0 - LAMBDA_INIT)).T.astype(o_ref.dtype)

    _attn_phases(qi_ref, kj_ref, ATT_BQ // k_ref.shape[0], init,
                 lambda rel: _attn_step([q0_scr, q1_scr], k_ref, vt_ref, s_bufs, ms, accs, rel),
                 finish)


def _diff_attn(dq, dk, dvt, lq1, lk1, lq2, lk2, g_col):
    s = dq.shape[0]
    bk = DIFF_BK
    assert s % ATT_BQ == 0
    qi, kj = _pair_tables(s // ATT_BQ, ATT_BQ // bk)
    hw = 2 * DIFF_HEAD_DIM
    small = lambda a: pl.BlockSpec(a.shape, lambda h, t, qi, kj: (0, 0))
    gs = pltpu.PrefetchScalarGridSpec(
        num_scalar_prefetch=2,
        grid=(DIFF_HEADS, qi.shape[0]),
        in_specs=[pl.BlockSpec((ATT_BQ, hw), lambda h, t, qi, kj: (qi[t], h)),
                  pl.BlockSpec((bk, hw), lambda h, t, qi, kj: (kj[t], h)),
                  pl.BlockSpec((V_EXT, bk), lambda h, t, qi, kj: (h, kj[t])),
                  small(lq1), small(lk1), small(lq2), small(lk2), small(g_col)],
        out_specs=pl.BlockSpec((ATT_BQ, hw), lambda h, t, qi, kj: (qi[t], h)),
        scratch_shapes=[pltpu.VMEM((ATT_BQ, hw), BF16), pltpu.VMEM((ATT_BQ, hw), BF16)] + _unit_scratch(2, bk),
    )
    return pl.pallas_call(
        _diff_attn_kernel,
        grid_spec=gs,
        out_shape=jax.ShapeDtypeStruct((s, DIFF_W), BF16),
        compiler_params=_cparams(("arbitrary", "arbitrary")),
        name="diff_attn",
    )(qi, kj, dq, dk, dvt, lq1, lk1, lq2, lk2, g_col)


def _pack_words(y):
    lo = pltpu.bitcast(y[:, :HALF].astype(BF16).astype(F32), U32) >> 16
    hi = pltpu.bitcast(y[:, HALF:].astype(BF16).astype(F32), U32) & jnp.uint32(0xFFFF0000)
    return lo | hi


def _store_packed(ref, words, rows):
    for s in range(ROW_TILES):
        ref[pl.ds(s, rows, stride=ROW_TILES), :] = words[:, s * LANES:(s + 1) * LANES]


def _load_packed(ref, rows):
    return jnp.concatenate(
        [ref[pl.ds(s, rows, stride=ROW_TILES), :] for s in range(ROW_TILES)], axis=1)


def _unpack_words(w):
    lo = pltpu.bitcast(w << 16, F32)
    hi = pltpu.bitcast(w & jnp.uint32(0xFFFF0000), F32)
    return lo, hi


def _outproj_kernel(om_ref, od_ref, x_ref, w_ref, gt_ref, gpost_ref, gpre_ref, sc_ref, sh_ref,
                    wr_ref, x1_ref, h2_ref, h2p_ref, lg_ref):
    n_sub = x_ref.shape[0] // TM_PROJ
    sub = lambda r, j: r.at[pl.ds(j * TM_PROJ, TM_PROJ)]
    w_hi, w_lo = _split_bf16(wr_ref[...])
    ne = w_hi.shape[0]
    w_hl = jnp.concatenate([w_hi, w_lo], axis=0)
    ys = [_dot(jnp.concatenate([sub(om_ref, j)[...], sub(od_ref, j)[...]], axis=1), w_ref[...])
          for j in range(n_sub)]
    for j, y in enumerate(ys):
        x1 = sub(x_ref, j)[...] + gt_ref[...] * (_rms(y) * gpost_ref[...])
        sub(x1_ref, j)[...] = x1
        h2 = _rms(x1) * gpre_ref[...] * (1.0 + sc_ref[...]) + sh_ref[...]
        h_hi, h_lo = _split_bf16(h2)
        sub(h2_ref, j)[...] = h_hi
        _store_packed(h2p_ref.at[pl.ds(j * TM_PROJ * ROW_TILES, TM_PROJ * ROW_TILES)], _pack_words(h2), TM_PROJ)
        a = _dot_nt(w_hl, h_hi)
        lg_ref[:, j * TM_PROJ:(j + 1) * TM_PROJ] = a[0:ne] + a[ne:] + _dot_nt(w_hi, h_lo)


def _outproj(om, od, x, w_out, gt, gpost, gpre, sc, sh, wr_t):
    s, d = x.shape
    tm = min(TM_OUT, s)
    row = lambda n: pl.BlockSpec((tm, n), lambda i: (i, 0))
    vec = _const_spec((1, d))
    return pl.pallas_call(
        _outproj_kernel,
        grid=(s // tm,),
        in_specs=[row(om.shape[1]), row(od.shape[1]), row(d), _const_spec(w_out.shape),
                  vec, vec, vec, vec, vec, _const_spec(wr_t.shape)],
        out_specs=[row(d), row(d),
                   pl.BlockSpec((tm * ROW_TILES, LANES), lambda i: (i, 0)),
                   pl.BlockSpec((N_EXPERTS, tm), lambda i: (0, i))],
        out_shape=[jax.ShapeDtypeStruct((s, d), F32),
                   jax.ShapeDtypeStruct((s, d), BF16),
                   jax.ShapeDtypeStruct((s * ROW_TILES, LANES), U32),
                   jax.ShapeDtypeStruct((N_EXPERTS, s), F32)],
        compiler_params=_cparams(("arbitrary",)),
        name="out_proj",
    )(om, od, x, w_out, gt, gpost, gpre, sc, sh, wr_t)


def _route_kernel(lg_ref, b_ref, eidx_ref, w_ref, rank_ref, cnt_ref, carry):
    step = pl.program_id(0)

    @pl.when(step == 0)
    def _():
        carry[...] = jnp.zeros(carry.shape, F32)

    tr = lg_ref.shape[1]
    ninf = jnp.float32(-jnp.inf)
    ie = lax.broadcasted_iota(I32, (GROUP_SIZE, tr), 0)
    rmax = lambda a: jnp.max(a, axis=0, keepdims=True)
    rmin = lambda a: jnp.min(a, axis=0, keepdims=True)
    rsum = lambda a: jnp.sum(a, axis=0, keepdims=True)

    sc, bi, gscore = [], [], []
    for g in range(N_GROUPS):
        lg = lg_ref[g * GROUP_SIZE:(g + 1) * GROUP_SIZE, :]
        s = jax.nn.sigmoid(lg)
        b = s + b_ref[g * GROUP_SIZE:(g + 1) * GROUP_SIZE, :]
        m1 = rmax(b)
        i1 = rmin(jnp.where(b == m1, ie, GROUP_SIZE))
        m2 = rmax(jnp.where(ie == i1, ninf, b))
        sc.append(s)
        bi.append(b)
        gscore.append(m1 + m2)

    gsel = [jnp.zeros((1, tr), I32) for _ in range(N_GROUPS)]
    for _ in range(TOPK_GROUPS):
        gm = functools.reduce(jnp.maximum, gscore)
        gi = functools.reduce(
            jnp.minimum, [jnp.where(gscore[g] == gm, g, N_GROUPS) for g in range(N_GROUPS)])
        for g in range(N_GROUPS):
            hit = gi == g
            gsel[g] = jnp.where(hit, 1, gsel[g])
            gscore[g] = jnp.where(hit, ninf, gscore[g])

    masked = [jnp.where(jnp.broadcast_to(gsel[g], (GROUP_SIZE, tr)) > 0, bi[g], ninf)
              for g in range(N_GROUPS)]
    sel = [jnp.zeros((GROUP_SIZE, tr), F32) for _ in range(N_GROUPS)]
    idxs, svals = [], []
    for _ in range(TOP_K):
        m = functools.reduce(jnp.maximum, [rmax(x) for x in masked])
        idx = functools.reduce(
            jnp.minimum,
            [rmin(jnp.where(masked[g] == m, ie + g * GROUP_SIZE, N_EXPERTS)) for g in range(N_GROUPS)])
        sv = jnp.zeros((1, tr), F32)
        for g in range(N_GROUPS):
            hit = (ie + g * GROUP_SIZE) == idx
            sv = sv + rsum(jnp.where(hit, sc[g], 0.0))
            masked[g] = jnp.where(hit, ninf, masked[g])
            sel[g] = jnp.where(hit, 1.0, sel[g])
        idxs.append(idx)
        svals.append(sv)
    tot = functools.reduce(lambda a, b: a + b, svals)

    sel2 = jnp.concatenate(sel, axis=0)
    before = (lax.broadcasted_iota(I32, (tr, tr), 0) < lax.broadcasted_iota(I32, (tr, tr), 1))
    rank2 = _dot(sel2.astype(BF16), before.astype(BF16)) + carry[:, 0:1]
    for k in range(TOP_K):
        rk = jnp.zeros((1, tr), F32)
        for g in range(N_GROUPS):
            hit = (ie + g * GROUP_SIZE) == idxs[k]
            rk = rk + rsum(jnp.where(hit, rank2[g * GROUP_SIZE:(g + 1) * GROUP_SIZE, :], 0.0))
        eidx_ref[k:k + 1, :] = idxs[k]
        w_ref[k:k + 1, :] = svals[k] / tot * ROUTED_SCALE
        rank_ref[k:k + 1, :] = rk.astype(I32)
    carry[...] = carry[...] + jnp.sum(sel2, axis=1, keepdims=True)
    cnt_ref[...] = carry[...]


def _route(lg_t, b_col):
    e, t = lg_t.shape
    tr = min(TR, t)
    tok = lambda: pl.BlockSpec((TOP_K, tr), lambda i: (0, i))
    return pl.pallas_call(
        _route_kernel,
        grid=(t // tr,),
        in_specs=[pl.BlockSpec((e, tr), lambda i: (0, i)), pl.BlockSpec((e, 1), lambda i: (0, 0))],
        out_specs=[tok(), tok(), tok(), pl.BlockSpec((e, LANES), lambda i: (0, 0))],
        out_shape=[jax.ShapeDtypeStruct((TOP_K, t), I32),
                   jax.ShapeDtypeStruct((TOP_K, t), F32),
                   jax.ShapeDtypeStruct((TOP_K, t), I32),
                   jax.ShapeDtypeStruct((e, LANES), F32)],
        scratch_shapes=[pltpu.VMEM((e, LANES), F32)],
        compiler_params=_cparams(("arbitrary",)),
        name="route",
    )(lg_t, b_col)


def _pos_kernel(pstart_ref, eidx_ref, rank_ref, pos_ref):
    e = eidx_ref[...]
    pos = rank_ref[...]
    for x in range(N_EXPERTS):
        pos = pos + jnp.where(e == x, pstart_ref[x], 0)
    pos_ref[...] = pos


def _positions(pstart, eidx, rank):
    k, t = eidx.shape
    tr = min(TR, t)
    gs = pltpu.PrefetchScalarGridSpec(
        num_scalar_prefetch=1,
        grid=(t // tr,),
        in_specs=[pl.BlockSpec((k, tr), lambda i, p: (0, i)), pl.BlockSpec((k, tr), lambda i, p: (0, i))],
        out_specs=pl.BlockSpec((k, tr), lambda i, p: (0, i)),
    )
    return pl.pallas_call(
        _pos_kernel, grid_spec=gs,
        out_shape=jax.ShapeDtypeStruct((k, t), I32),
        compiler_params=_cparams(("arbitrary",)),
        name="positions",
    )(pstart, eidx, rank)


def _row_copy(src, src_row, dst, dst_row, sem):
    return pltpu.make_async_copy(src.at[pl.ds(src_row * ROW_TILES, ROW_TILES)],
                                 dst.at[pl.ds(dst_row * ROW_TILES, ROW_TILES)], sem)


FILL_CHUNKS = (128, 64, 32, 16, 8, 4, 2, 1)


def _zero_fill(fs_ref, fl_ref, tail_ref, zbuf, xs_ref, sem, wait):
    def chunk(row, n):
        return pltpu.make_async_copy(zbuf.at[pl.ds(0, n * ROW_TILES)],
                                     xs_ref.at[pl.ds(row * ROW_TILES, n * ROW_TILES)], sem)

    def go(cp):
        if wait:
            cp.wait()
        else:
            cp.start()

    def per_expert(e, c):
        pad = fl_ref[e]
        row = fs_ref[e]
        for n in FILL_CHUNKS:
            @pl.when((pad & n) != 0)
            def _(row=row, n=n):
                go(chunk(row, n))
            row = row + (pad & n)
        return c

    lax.fori_loop(0, N_EXPERTS, per_expert, 0)

    def per_tail(b, c):
        for part in range(EXP_BLK // FILL_CHUNKS[0]):
            go(chunk(b * EXP_BLK + part * FILL_CHUNKS[0], FILL_CHUNKS[0]))
        return c

    lax.fori_loop(tail_ref[0], tail_ref[1], per_tail, 0)


def _dispatch_kernel(fs_ref, fl_ref, tail_ref, pos_ref, h_ref, xs_ref, zbuf, sem, fill_sem):
    td = pos_ref.shape[1]
    step = pl.program_id(0)

    @pl.when(step == 0)
    def _():
        zbuf[...] = jnp.zeros(zbuf.shape, zbuf.dtype)
        _zero_fill(fs_ref, fl_ref, tail_ref, zbuf, xs_ref, fill_sem, wait=False)

    def issue(t, c):
        for k in range(TOP_K):
            _row_copy(h_ref, t, xs_ref, pos_ref[k, t], sem).start(priority=k % 2)
        return c

    lax.fori_loop(0, td, issue, 0, unroll=4)
    for k in range(TOP_K):
        pltpu.make_async_copy(h_ref, xs_ref.at[pl.ds(0, td * ROW_TILES)], sem).wait()

    @pl.when(step == pl.num_programs(0) - 1)
    def _():
        _zero_fill(fs_ref, fl_ref, tail_ref, zbuf, xs_ref, fill_sem, wait=True)


def _dispatch(fill_start, fill_len, tail, pos, h2p, m_pad):
    k, t = pos.shape
    td = min(TD, t)
    gs = pltpu.PrefetchScalarGridSpec(
        num_scalar_prefetch=3,
        grid=(t // td,),
        in_specs=[pl.BlockSpec((k, td), lambda i, *_: (0, i), memory_space=pltpu.SMEM),
                  pl.BlockSpec((td * ROW_TILES, LANES), lambda i, *_: (i, 0))],
        out_specs=pl.BlockSpec(memory_space=pl.ANY),
        scratch_shapes=[pltpu.VMEM((FILL_CHUNKS[0] * ROW_TILES, LANES), U32),
                        pltpu.SemaphoreType.DMA, pltpu.SemaphoreType.DMA],
    )
    return pl.pallas_call(
        _dispatch_kernel,
        grid_spec=gs,
        out_shape=jax.ShapeDtypeStruct((m_pad * ROW_TILES, LANES), U32),
        compiler_params=_cparams(("arbitrary",)),
        name="dispatch",
    )(fill_start, fill_len, tail, pos, h2p)


def _weight_copies(w_hbm, w_buf, sem, e, s):
    return [pltpu.make_async_copy(h.at[e], b.at[s], sem.at[s]) for h, b in zip(w_hbm, w_buf)]


def _experts_kernel(be_ref, nu_ref, fresh_ref, nxt_ref, slot_ref, x_ref, wg_hbm, wu_hbm, wd_hbm, y_ref,
                    wg_buf, wu_buf, wd_buf, wg_s, wu_s, wd_s, sem):
    step = pl.program_id(0)
    b0 = step * BLKS_PER_STEP
    blk_rows = x_ref.shape[0] // BLKS_PER_STEP
    w_hbm = (wg_hbm, wu_hbm, wd_hbm)
    w_buf = (wg_buf, wu_buf, wd_buf)

    def maybe_weights(b):
        @pl.when(fresh_ref[b] == 1)
        def _():
            s = slot_ref[b]

            @pl.when(b == 0)
            def _():
                for c in _weight_copies(w_hbm, w_buf, sem, be_ref[0], 0):
                    c.start()

            for c in _weight_copies(w_hbm, w_buf, sem, be_ref[b], s):
                c.wait()

            @pl.when(nxt_ref[b] >= 0)
            def _():
                for c in _weight_copies(w_hbm, w_buf, sem, nxt_ref[b], 1 - s):
                    c.start()

            wg_s[...] = wg_buf[s].astype(BF16)
            wu_s[...] = wu_buf[s].astype(BF16)
            wd_s[...] = wd_buf[s].astype(BF16)

    def swiglu(first, n):
        x_blk = x_ref.at[pl.ds(first * blk_rows, n * blk_rows)]
        y_blk = y_ref.at[pl.ds(first * blk_rows, n * blk_rows)]
        rows = n * blk_rows // ROW_TILES
        xa, xb = _unpack_words(_load_packed(x_blk, rows))
        xa = xa.astype(BF16)
        xb = xb.astype(BF16)
        g = _dot(xa, wg_s[0:HALF, :]) + _dot(xb, wg_s[HALF:, :])
        u = _dot(xa, wu_s[0:HALF, :]) + _dot(xb, wu_s[HALF:, :])
        a = (g * jax.nn.sigmoid(g) * u).astype(BF16)
        _store_packed(y_blk, _pack_words(_dot(a, wd_s[...])), rows)

    def zero(first, n):
        y_ref[pl.ds(first * blk_rows, n * blk_rows), :] = jnp.zeros((n * blk_rows, LANES), y_ref.dtype)

    n_used = jnp.clip(nu_ref[0] - b0, 0, BLKS_PER_STEP)
    same = jnp.logical_and(n_used == BLKS_PER_STEP, be_ref[b0] == be_ref[b0 + 1])

    @pl.when(same)
    def _():
        maybe_weights(b0)
        swiglu(0, BLKS_PER_STEP)

    @pl.when(jnp.logical_and(jnp.logical_not(same), n_used > 0))
    def _():
        maybe_weights(b0)
        swiglu(0, 1)

        @pl.when(n_used > 1)
        def _():
            maybe_weights(b0 + 1)
            swiglu(1, 1)

        @pl.when(n_used == 1)
        def _():
            zero(1, 1)

    @pl.when(n_used == 0)
    def _():
        zero(0, BLKS_PER_STEP)


def _experts(blk_e, nused, fresh, nxt_e, slot, xs, w_gate, w_up, w_down, blk):
    n_blk = blk_e.shape[0]
    assert n_blk % BLKS_PER_STEP == 0 and BLKS_PER_STEP == 2
    ne, d, de = w_gate.shape
    rows = BLKS_PER_STEP * blk * ROW_TILES
    xmap = lambda i, be, nu, fr, nx, sl: (jnp.minimum(i, (nu[0] - 1) // BLKS_PER_STEP), 0)
    hbm = pl.BlockSpec(memory_space=pl.ANY)
    gs = pltpu.PrefetchScalarGridSpec(
        num_scalar_prefetch=5,
        grid=(n_blk // BLKS_PER_STEP,),
        in_specs=[pl.BlockSpec((rows, LANES), xmap), hbm, hbm, hbm],
        out_specs=pl.BlockSpec((rows, LANES), lambda i, *_: (i, 0)),
        scratch_shapes=[pltpu.VMEM((2, d, de), F32), pltpu.VMEM((2, d, de), F32), pltpu.VMEM((2, de, d), F32),
                        pltpu.VMEM((d, de), BF16), pltpu.VMEM((d, de), BF16), pltpu.VMEM((de, d), BF16),
                        pltpu.SemaphoreType.DMA((2,))],
    )
    return pl.pallas_call(
        _experts_kernel, grid_spec=gs,
        out_shape=jax.ShapeDtypeStruct(xs.shape, U32),
        compiler_params=_cparams(("arbitrary",)),
        name="experts",
    )(blk_e, nused, fresh, nxt_e, slot, xs, w_gate, w_up, w_down)


def _combine_kernel(pos_ref, wt_ref, ys_ref, h_ref, x1_ref, wsg_ref, wsu_ref, wsd_ref,
                    gt_ref, gpost_ref, o_ref, gbuf, sem):
    td = pos_ref.shape[1]

    def issue(t, c):
        for k in range(TOP_K):
            _row_copy(ys_ref, pos_ref[k, t], gbuf.at[k], t, sem).start(priority=k % 2)
        return c

    lax.fori_loop(0, td, issue, 0, unroll=4)

    hb = h_ref[...]
    g = _dot(hb, wsg_ref[...])
    u = _dot(hb, wsu_ref[...])
    y = _dot((g * jax.nn.sigmoid(g) * u).astype(BF16), wsd_ref[...])

    wpad = jnp.concatenate([wt_ref[...], jnp.zeros((LANES - TOP_K, td), F32)], axis=0)
    wcol = jnp.concatenate([wpad[:, j * LANES:(j + 1) * LANES].T for j in range(td // LANES)], axis=0)

    for k in range(TOP_K):
        pltpu.make_async_copy(ys_ref.at[pl.ds(0, td * ROW_TILES)], gbuf.at[k], sem).wait()
    lo = jnp.zeros((td, HALF), F32)
    hi = jnp.zeros((td, HALF), F32)
    for k in range(TOP_K):
        a, b = _unpack_words(_load_packed(gbuf.at[k], td))
        wk = wcol[:, k:k + 1]
        lo = lo + wk * a
        hi = hi + wk * b
    y = y + jnp.concatenate([lo, hi], axis=1)
    o_ref[...] = x1_ref[...] + gt_ref[...] * (_rms(y) * gpost_ref[...])


def _combine(pos, wts, ys, h2, x1, wsg, wsu, wsd, gt, gpost):
    k, t = pos.shape
    d = x1.shape[1]
    td = min(TD, t)
    assert td % LANES == 0
    row = lambda n: pl.BlockSpec((td, n), lambda i: (i, 0))
    vec = _const_spec((1, d))
    return pl.pallas_call(
        _combine_kernel,
        grid=(t // td,),
        in_specs=[pl.BlockSpec((k, td), lambda i: (0, i), memory_space=pltpu.SMEM),
                  pl.BlockSpec((k, td), lambda i: (0, i)),
                  pl.BlockSpec(memory_space=pl.ANY),
                  row(d), row(d), _const_spec(wsg.shape), _const_spec(wsu.shape), _const_spec(wsd.shape),
                  vec, vec],
        out_specs=row(d),
        out_shape=jax.ShapeDtypeStruct((t, d), F32),
        scratch_shapes=[pltpu.VMEM((k, td * ROW_TILES, LANES), U32), pltpu.SemaphoreType.DMA],
        compiler_params=_cparams(("arbitrary",)),
        name="combine",
    )(pos, wts, ys, h2, x1, wsg, wsu, wsd, gt, gpost)


def _rope_tables(seq, dim):
    pos = jnp.arange(seq, dtype=F32)
    inv = ROPE_THETA ** (-jnp.arange(0, dim, 2, dtype=F32) / dim)
    ang = pos[:, None] * inv[None, :]
    return jnp.cos(ang), jnp.sin(ang)


def _lane_tables(seq, dim, period):
    cos, sin = _rope_tables(seq, dim)
    half = dim // 2
    cos = jnp.tile(cos, (1, LANES // half))
    sin = jnp.tile(sin, (1, LANES // half))
    r = jnp.arange(LANES) % period
    c = jnp.where(r < dim, cos, 1.0)
    sa = jnp.where(r < half, -sin, 0.0)
    sb = jnp.where((r >= half) & (r < dim), sin, 0.0)
    return c, sa, sb


def _moe(h2, h2p, x1, lg_t, b_router, w_gate, w_up, w_down, wsg, wsu, wsd, gt_f, g_post_ffn):
    t = h2.shape[0]
    blk = EXP_BLK
    m_pad = t * TOP_K + N_EXPERTS * blk
    eidx, wts, rank, cnt = _route(lg_t, b_router.reshape(N_EXPERTS, 1))
    counts = cnt[:, 0].astype(I32)
    padded = (counts + blk - 1) // blk * blk
    pend = jnp.cumsum(padded)
    pstart = pend - padded
    n_blk = m_pad // blk
    blk_start = jnp.arange(n_blk, dtype=I32) * blk
    blk_e = jnp.minimum(jnp.sum(pend[None, :] <= blk_start[:, None], axis=1), N_EXPERTS - 1).astype(I32)
    nused = (pend[-1:] // blk).astype(I32)
    pos = _positions(pstart.astype(I32), eidx, rank)
    tail = jnp.concatenate([nused, jnp.full((1,), n_blk, I32)])
    xs = _dispatch((pstart + counts).astype(I32), (padded - counts).astype(I32), tail, pos, h2p, m_pad)
    idx = jnp.arange(n_blk, dtype=I32)
    fresh = (idx < nused[0]) & (blk_e != jnp.concatenate([jnp.full((1,), -1, I32), blk_e[:-1]]))
    slot = ((jnp.cumsum(fresh.astype(I32)) - 1) % 2).astype(I32)
    first_at_or_after = lax.cummin(jnp.where(fresh, idx, n_blk)[::-1])[::-1]
    nxt_idx = jnp.concatenate([first_at_or_after[1:], jnp.full((1,), n_blk, I32)])
    nxt_e = jnp.where(nxt_idx < n_blk, blk_e[jnp.minimum(nxt_idx, n_blk - 1)], -1).astype(I32)
    ys = _experts(blk_e, nused, fresh.astype(I32), nxt_e, slot, xs, w_gate, w_up, w_down, blk)
    return _combine(pos, wts, ys, h2, x1, wsg, wsu, wsd, gt_f, g_post_ffn)


def _layer(x, c, w_ada, b_ada, g_pre_mix, w_in, g_q_lat, w_uq, g_kv_lat, w_ukv,
           lq1, lk1, lq2, lk2, g_diff_sub, w_out, g_post_mix, g_pre_ffn, w_router, b_router,
           w_gate, w_up, w_down, ws_gate, ws_up, ws_down, g_post_ffn):
    s, d = x.shape
    row = lambda a: a.reshape(1, -1)

    mod = _ada(c, w_ada, row(b_ada))
    sh_a, sc_a, gt_a, sh_f, sc_f, gt_f = [mod[:, i * d:(i + 1) * d] for i in range(6)]

    o1 = MLA_Q_RANK + MLA_KV_RANK
    o2 = o1 + MLA_ROPE
    o3 = o2 + 2 * DIFF_W
    w_kpe = jnp.pad(w_in[:, o1:o2], ((0, 0), (0, LANES - MLA_ROPE))).astype(BF16)
    wdv_t = lax.optimization_barrier(w_in[:, o3:]).T.astype(BF16)
    wq = w_uq.reshape(MLA_Q_RANK, MLA_HEADS, MLA_NOPE + MLA_ROPE)
    wq = jnp.pad(wq, ((0, 0), (0, 0), (0, MLA_QK_PAD - MLA_NOPE - MLA_ROPE)))
    wuq_r = wq.reshape(MLA_Q_RANK, MLA_HEADS * MLA_QK_PAD).astype(BF16)
    wkv = w_ukv.reshape(MLA_KV_RANK, MLA_HEADS, MLA_NOPE + MLA_V)
    wuk_r = wkv[:, :, :MLA_NOPE].reshape(MLA_KV_RANK, -1).astype(BF16)
    wmv_t = wkv[:, :, MLA_NOPE:].reshape(MLA_KV_RANK, -1).T.astype(BF16)
    dq, dk, dvt, qc, kc, vt = _inproj(
        x, row(g_pre_mix), sc_a, sh_a, w_in[:, :o1].astype(BF16), w_in[:, o2:o3].astype(BF16), w_kpe, wdv_t,
        _lane_tables(s, DIFF_ROT, DIFF_HEAD_DIM), row(g_q_lat), row(g_kv_lat), wuq_r, wuk_r, wmv_t,
        _lane_tables(s, MLA_ROPE, MLA_ROPE))

    o_mla = _mla_attn(qc, kc, vt)
    o_diff = _diff_attn(dq, dk, dvt, row(lq1), row(lk1), row(lq2), row(lk2), g_diff_sub.reshape(-1, 1))

    x1, h2, h2p, lg_t = _outproj(o_mla, o_diff, x, w_out.astype(BF16), gt_a, row(g_post_mix),
                                 row(g_pre_ffn), sc_f, sh_f, w_router.T)
    return _moe(h2, h2p, x1, lg_t, b_router, w_gate, w_up, w_down,
                ws_gate.astype(BF16), ws_up.astype(BF16), ws_down.astype(BF16), gt_f, row(g_post_ffn))


def kernel(x, c, w_ada, b_ada, g_pre_mix, w_in, g_q_lat, w_uq, g_kv_lat, w_ukv, lambda_q1, lambda_k1, lambda_q2, lambda_k2, g_diff_sub, w_out, g_post_mix, g_pre_ffn, w_router, b_router, w_gate, w_up, w_down, ws_gate, ws_up, ws_down, g_post_ffn):
    assert x.shape[0] == 1 and w_ada.shape[0] == 1
    out = _layer(x[0], c, w_ada[0], b_ada[0], g_pre_mix[0], w_in[0], g_q_lat[0], w_uq[0],
                 g_kv_lat[0], w_ukv[0], lambda_q1[0], lambda_k1[0], lambda_q2[0], lambda_k2[0],
                 g_diff_sub[0], w_out[0], g_post_mix[0], g_pre_ffn[0], w_router[0], b_router[0],
                 w_gate[0], w_up[0], w_down[0], ws_gate[0], ws_up[0], ws_down[0], g_post_ffn[0])
    return out[None]
```

```python
import functools
import math

import jax
import jax.numpy as jnp
import numpy as np
from jax import lax
from jax.experimental import pallas as pl
from jax.experimental.pallas import tpu as pltpu

F32 = jnp.float32
BF16 = jnp.bfloat16
U32 = jnp.uint32
I32 = jnp.int32

D_MODEL = 2048
CHUNK = 64
ROPE_THETA = 500000.0
EPS = 1e-6
LOG2E = 1.4426950408889634

MLA_HEADS = 8
MLA_Q_RANK = 768
MLA_KV_RANK = 512
MLA_NOPE = 128
MLA_ROPE = 64
MLA_V = 128
MLA_QK_PAD = 256

DIFF_HEADS = 8
DIFF_HEAD_DIM = 64
DIFF_ROT = DIFF_HEAD_DIM // 4
DIFF_W = DIFF_HEADS * 2 * DIFF_HEAD_DIM

N_EXPERTS = 64
TOP_K = 8
N_GROUPS = 8
GROUP_SIZE = N_EXPERTS // N_GROUPS
TOPK_GROUPS = 4
D_EXPERT = 512
ROUTED_SCALE = 2.5
LAMBDA_INIT = 0.8 - 0.6 * math.exp(-0.3 * 0)

LANES = 128
SUBLANES = 8
HALF = D_MODEL // 2
ROW_TILES = HALF // LANES

TM_PROJ = 256
TM_OUT = 2 * TM_PROJ
ATT_BQ = 4096
MLA_BK = 2048
DIFF_BK = 1024
UNIT_W = 512
V_EXT = MLA_V + 16
EXP_BLK = 256
BLKS_PER_STEP = 2
TD = 256
TR = 512
TN_ADA = 1024

VMEM_LIMIT = 56 * 1024 * 1024

assert ROW_TILES == SUBLANES


def _cparams(sem, vmem=VMEM_LIMIT, flags=None):
    return pltpu.CompilerParams(dimension_semantics=sem, vmem_limit_bytes=vmem, flags=flags)


def _dot(a, b):
    return jnp.dot(a, b, preferred_element_type=F32)


def _dot_nt(a, b):
    return lax.dot_general(a, b, (((1,), (1,)), ((), ())), preferred_element_type=F32)


def _rms(x):
    return x * lax.rsqrt(jnp.mean(x * x, axis=-1, keepdims=True) + EPS)


def _split_bf16(x):
    hi = x.astype(BF16)
    lo = (x - hi.astype(F32)).astype(BF16)
    return hi, lo


def _const_spec(shape):
    nd = len(shape)
    return pl.BlockSpec(shape, lambda *a: (0,) * nd, pipeline_mode=pl.Buffered(1))


def _ada_kernel(c_ref, w_ref, b_ref, o_ref):
    c = c_ref[...]
    a = c * jax.nn.sigmoid(c)
    a8 = jnp.broadcast_to(a, (SUBLANES, a.shape[1]))
    a_hi, a_lo = _split_bf16(a8)
    w_hi, w_lo = _split_bf16(w_ref[...])
    r = _dot(a_hi, w_hi) + _dot(a_lo, w_hi) + _dot(a_hi, w_lo)
    o_ref[...] = r[0:1] + b_ref[...]


def _ada(c, w, b):
    d, n = w.shape
    return pl.pallas_call(
        _ada_kernel,
        grid=(n // TN_ADA,),
        in_specs=[pl.BlockSpec((1, d), lambda j: (0, 0)),
                  pl.BlockSpec((d, TN_ADA), lambda j: (0, j)),
                  pl.BlockSpec((1, TN_ADA), lambda j: (0, j))],
        out_specs=pl.BlockSpec((1, TN_ADA), lambda j: (0, j)),
        out_shape=jax.ShapeDtypeStruct((1, n), F32),
        compiler_params=_cparams(("arbitrary",)),
        name="ada_mod",
    )(c, w, b)


def _tile_lanes(t, reps):
    return jnp.concatenate([t] * reps, axis=1)


def _rope_lanes(x, c, sa, sb, half):
    n = x.shape[1]
    return x * c + pltpu.roll(x, n - half, 1) * sa + pltpu.roll(x, half, 1) * sb


def _split_w_in_kernel(offs, w_ref, lat_ref, qk_ref, kpe_ref, dvt_ref):
    o1, o2, o3 = offs
    w = w_ref[...]
    lat_ref[...] = w[:, :o1].astype(BF16)
    qk_ref[...] = w[:, o2:o3].astype(BF16)
    kpe = w[:, o1:o2]
    kpe_ref[...] = jnp.concatenate([kpe, jnp.zeros_like(kpe)], axis=1).astype(BF16)
    dvt_ref[...] = w[:, o3:].T.astype(BF16)


def _split_w_in(w_in, offs):
    d, n = w_in.shape
    o1, o2, o3 = offs
    tk = TM_PROJ
    return pl.pallas_call(
        functools.partial(_split_w_in_kernel, offs),
        grid=(d // tk,),
        in_specs=[pl.BlockSpec((tk, n), lambda i: (i, 0))],
        out_specs=[pl.BlockSpec((tk, o1), lambda i: (i, 0)), pl.BlockSpec((tk, o3 - o2), lambda i: (i, 0)),
                   pl.BlockSpec((tk, LANES), lambda i: (i, 0)), pl.BlockSpec((n - o3, tk), lambda i: (0, i))],
        out_shape=[jax.ShapeDtypeStruct((d, o1), BF16), jax.ShapeDtypeStruct((d, o3 - o2), BF16),
                   jax.ShapeDtypeStruct((d, LANES), BF16), jax.ShapeDtypeStruct((n - o3, d), BF16)],
        compiler_params=_cparams(("arbitrary",)),
        name="split_w_in",
    )(w_in)


def _mla_heads(lat, kpe_raw, gq_ref, gkv_ref, wuq_ref, wuk_ref, wvt_ref, c, sa, sb, q_ref, k_ref, vt_ref):
    qn = (_rms(lat[:, :MLA_Q_RANK]) * gq_ref[...]).astype(BF16)
    kvn = (_rms(lat[:, MLA_Q_RANK:]) * gkv_ref[...]).astype(BF16)
    q = _dot(qn, wuq_ref[...])
    kn = _dot(kvn, wuk_ref[...])
    half = MLA_ROPE // 2
    qs = (MLA_NOPE + MLA_ROPE) ** -0.5 * LOG2E
    kpe = _rope_lanes(kpe_raw, c, sa, sb, half).astype(BF16)
    for h in range(MLA_HEADS):
        o = h * MLA_QK_PAD
        q_ref[:, o:o + LANES] = (q[:, o:o + LANES] * qs).astype(BF16)
        q_ref[:, o + LANES:o + 2 * LANES] = (
            _rope_lanes(q[:, o + LANES:o + 2 * LANES], c, sa, sb, half) * qs).astype(BF16)
        k_ref[:, o:o + LANES] = kn[:, h * MLA_NOPE:(h + 1) * MLA_NOPE].astype(BF16)
        k_ref[:, o + LANES:o + 2 * LANES] = kpe
    _store_vt_ext(vt_ref, _dot_nt(wvt_ref[...], kvn), MLA_HEADS)


def _inproj_kernel(x_ref, g_ref, sc_ref, sh_ref, wlat_ref, wqk_ref, wkpe_ref, wdvt_ref,
                   dc_ref, dsa_ref, dsb_ref, gq_ref, gkv_ref, wuq_ref, wuk_ref, wmvt_ref,
                   mc_ref, msa_ref, msb_ref,
                   dq_ref, dk_ref, dvt_ref, q_ref, k_ref, vt_ref):
    x = x_ref[...]
    h = _rms(x) * g_ref[...] * (1.0 + sc_ref[...]) + sh_ref[...]
    hb = h.astype(BF16)
    reps = DIFF_W // LANES
    c = _tile_lanes(dc_ref[...], reps)
    sa = _tile_lanes(dsa_ref[...], reps)
    sb = _tile_lanes(dsb_ref[...], reps)
    half = DIFF_ROT // 2
    q = _dot(hb, wqk_ref[:, 0:DIFF_W])
    dq_ref[...] = (_rope_lanes(q, c, sa, sb, half) * (DIFF_HEAD_DIM ** -0.5 * LOG2E)).astype(BF16)
    k = _dot(hb, wqk_ref[:, DIFF_W:])
    dk_ref[...] = _rope_lanes(k, c, sa, sb, half).astype(BF16)
    _store_vt_ext(dvt_ref, _dot_nt(wdvt_ref[...], hb), DIFF_HEADS)
    _mla_heads(_dot(hb, wlat_ref[...]), _dot(hb, wkpe_ref[...]), gq_ref, gkv_ref, wuq_ref, wuk_ref,
               wmvt_ref, mc_ref[...], msa_ref[...], msb_ref[...], q_ref, k_ref, vt_ref)


def _inproj(x, g, sc, sh, w_lat, w_qk, w_kpe, wdv_t, dtabs, gq, gkv, wuq_r, wuk_r, wmv_t, mtabs):
    s, d = x.shape
    tm = min(TM_PROJ, s)
    row = lambda n: pl.BlockSpec((tm, n), lambda i: (i, 0))
    col = lambda n: pl.BlockSpec((n, tm), lambda i: (0, i))
    const = lambda a: _const_spec(a.shape)
    hq = MLA_HEADS * MLA_QK_PAD
    return pl.pallas_call(
        _inproj_kernel,
        grid=(s // tm,),
        in_specs=[row(d), const(g), const(sc), const(sh), const(w_lat), const(w_qk), const(w_kpe),
                  const(wdv_t), row(LANES), row(LANES), row(LANES),
                  const(gq), const(gkv), const(wuq_r), const(wuk_r), const(wmv_t),
                  row(LANES), row(LANES), row(LANES)],
        out_specs=[row(DIFF_W), row(DIFF_W), col(DIFF_HEADS * V_EXT), row(hq), row(hq), col(MLA_HEADS * V_EXT)],
        out_shape=[jax.ShapeDtypeStruct((s, DIFF_W), BF16),
                   jax.ShapeDtypeStruct((s, DIFF_W), BF16),
                   jax.ShapeDtypeStruct((DIFF_HEADS * V_EXT, s), BF16),
                   jax.ShapeDtypeStruct((s, hq), BF16),
                   jax.ShapeDtypeStruct((s, hq), BF16),
                   jax.ShapeDtypeStruct((MLA_HEADS * V_EXT, s), BF16)],
        compiler_params=_cparams(("arbitrary",)),
        name="in_proj",
    )(x, g, sc, sh, w_lat, w_qk, w_kpe, wdv_t, *dtabs, gq, gkv, wuq_r, wuk_r, wmv_t, *mtabs)


def _pair_tables(nq, kv_per_q):
    n = [kv_per_q * (i + 1) for i in range(nq)]
    qi = np.concatenate([np.full(c, i) for i, c in enumerate(n)]).astype(np.int32)
    kj = np.concatenate([np.arange(c) for c in n]).astype(np.int32)
    return jnp.asarray(qi), jnp.asarray(kj)


def _ones_rows(n):
    rows = V_EXT - MLA_V
    return (lax.broadcasted_iota(I32, (rows, n), 0) == 0).astype(BF16)


def _store_vt_ext(vt_ref, vt, heads):
    ones = _ones_rows(vt.shape[1])
    for h in range(heads):
        vt_ref[h * V_EXT:h * V_EXT + MLA_V, :] = vt[h * MLA_V:(h + 1) * MLA_V, :].astype(BF16)
        vt_ref[h * V_EXT + MLA_V:(h + 1) * V_EXT, :] = ones


def _attn_step(q_refs, k_ref, vt_ref, s_bufs, m_refs, acc_refs, rel):
    w = UNIT_W
    nk = k_ref.shape[0]
    per_kv = nk // w
    specs = []
    for u in range(q_refs[0].shape[0] // w):
        d = None if rel is None else u - rel * per_kv
        if d is None or d >= per_kv:
            specs.append((u, nk, None))
        elif d >= 0:
            specs.append((u, (d + 1) * w, d * w))
    units = [(si, sp) for sp in specs for si in range(len(q_refs))]

    def scores(n):
        si, (u, rows, off) = units[n]
        s = _dot_nt(k_ref[0:rows, :], q_refs[si][u * w:(u + 1) * w, :])
        if off is not None:
            kc = lax.broadcasted_iota(I32, (rows, w), 0) // CHUNK
            qc = (lax.broadcasted_iota(I32, (rows, w), 1) + off) // CHUNK
            s = jnp.where(kc <= qc, s, -jnp.inf)
        s_bufs[n % 2][0:rows, :] = s
        return jnp.max(s, axis=0, keepdims=True)

    cmax_next = scores(0)
    for n, (si, (u, rows, _)) in enumerate(units):
        cmax = cmax_next
        if n + 1 < len(units):
            cmax_next = scores(n + 1)
        m_ref, acc_ref = m_refs[si][u], acc_refs[si][u]
        m_prev = m_ref[...]
        m_new = jnp.maximum(m_prev, cmax)
        m_ref[...] = m_new
        acc_ref[...] = jnp.exp2(m_prev - m_new) * acc_ref[...]
        p = jnp.exp2(s_bufs[n % 2][0:rows, :] - m_new).astype(BF16)
        acc_ref[...] += _dot(vt_ref[:, 0:rows], p)


def _init_stats(ms, accs):
    for m, a in zip(ms, accs):
        m[...] = jnp.full(m.shape, -jnp.inf, F32)
        a[...] = jnp.zeros(a.shape, F32)


def _normalized(acc_ref):
    acc = acc_ref[...]
    return acc[0:MLA_V, :] / acc[MLA_V:MLA_V + 1, :]


def _attn_phases(qi_ref, kj_ref, kv_per_q, init, step, finish):
    t = pl.program_id(1)
    rel = kj_ref[t] - kv_per_q * qi_ref[t]

    @pl.when(kj_ref[t] == 0)
    def _():
        init()

    @pl.when(rel < 0)
    def _():
        step(None)

    for r in range(kv_per_q):
        @pl.when(rel == r)
        def _(r=r):
            step(r)
            if r == kv_per_q - 1:
                finish()


def _unit_scratch(n_streams, bk):
    nu = ATT_BQ // UNIT_W
    return ([pltpu.VMEM((bk, UNIT_W), F32)] * 2
            + [pltpu.VMEM((1, UNIT_W), F32)] * (nu * n_streams)
            + [pltpu.VMEM((V_EXT, UNIT_W), F32)] * (nu * n_streams))


def _split_units(scr, n_streams):
    nu = ATT_BQ // UNIT_W
    ms = [list(scr[s * nu:(s + 1) * nu]) for s in range(n_streams)]
    accs = [list(scr[(n_streams + s) * nu:(n_streams + s + 1) * nu]) for s in range(n_streams)]
    return ms, accs


def _mla_attn_kernel(qi_ref, kj_ref, q_ref, k_ref, vt_ref, o_ref, *scr):
    s_bufs, (ms, accs) = scr[:2], _split_units(scr[2:], 1)

    def finish():
        for u, a in enumerate(accs[0]):
            o_ref[u * UNIT_W:(u + 1) * UNIT_W, :] = _normalized(a).T.astype(o_ref.dtype)

    _attn_phases(qi_ref, kj_ref, ATT_BQ // k_ref.shape[0],
                 lambda: _init_stats(ms[0], accs[0]),
                 lambda rel: _attn_step([q_ref], k_ref, vt_ref, s_bufs, ms, accs, rel),
                 finish)


def _mla_attn(qc, kc, vt):
    s = qc.shape[0]
    bk = MLA_BK
    assert s % ATT_BQ == 0
    qi, kj = _pair_tables(s // ATT_BQ, ATT_BQ // bk)
    gs = pltpu.PrefetchScalarGridSpec(
        num_scalar_prefetch=2,
        grid=(MLA_HEADS, qi.shape[0]),
        in_specs=[pl.BlockSpec((ATT_BQ, MLA_QK_PAD), lambda h, t, qi, kj: (qi[t], h)),
                  pl.BlockSpec((bk, MLA_QK_PAD), lambda h, t, qi, kj: (kj[t], h)),
                  pl.BlockSpec((V_EXT, bk), lambda h, t, qi, kj: (h, kj[t]))],
        out_specs=pl.BlockSpec((ATT_BQ, MLA_V), lambda h, t, qi, kj: (qi[t], h)),
        scratch_shapes=_unit_scratch(1, bk),
    )
    return pl.pallas_call(
        _mla_attn_kernel,
        grid_spec=gs,
        out_shape=jax.ShapeDtypeStruct((s, MLA_HEADS * MLA_V), BF16),
        compiler_params=_cparams(("arbitrary", "arbitrary")),
        name="mla_attn",
    )(qi, kj, qc, kc, vt)


def _diff_attn_kernel(qi_ref, kj_ref, q_ref, k_ref, vt_ref, lq1_ref, lk1_ref, lq2_ref, lk2_ref,
                      g_ref, o_ref, q0_scr, q1_scr, *scr):
    s_bufs, (ms, accs) = scr[:2], _split_units(scr[2:], 2)

    def init():
        q = q_ref[...]
        lane = lax.broadcasted_iota(I32, q.shape, 1)
        zero = jnp.zeros_like(q)
        q0_scr[...] = jnp.where(lane < DIFF_HEAD_DIM, q, zero)
        q1_scr[...] = jnp.where(lane >= DIFF_HEAD_DIM, q, zero)
        _init_stats(ms[0] + ms[1], accs[0] + accs[1])

    def finish():
        lam = (jnp.exp(jnp.sum(lq1_ref[...] * lk1_ref[...], axis=1, keepdims=True))
               - jnp.exp(jnp.sum(lq2_ref[...] * lk2_ref[...], axis=1, keepdims=True))
               + LAMBDA_INIT)
        for u, (a0, a1) in enumerate(zip(accs[0], accs[1])):
            o = _normalized(a0) - lam * _normalized(a1)
            o = o * lax.rsqrt(jnp.mean(o * o, axis=0, keepdims=True) + EPS)
            o_ref[u * UNIT_W:(u + 1) * UNIT_W, :] = (
                o * g_ref[...] * (1.0 - LAMBDA_INIT)).T.astype(o_ref.dtype)

    _attn_phases(qi_ref, kj_ref, ATT_BQ // k_ref.shape[0], init,
                 lambda rel: _attn_step([q0_scr, q1_scr], k_ref, vt_ref, s_bufs, ms, accs, rel),
                 finish)


def _diff_attn(dq, dk, dvt, lq1, lk1, lq2, lk2, g_col):
    s = dq.shape[0]
    bk = DIFF_BK
    assert s % ATT_BQ == 0
    qi, kj = _pair_tables(s // ATT_BQ, ATT_BQ // bk)
    hw = 2 * DIFF_HEAD_DIM
    small = lambda a: pl.BlockSpec(a.shape, lambda h, t, qi, kj: (0, 0))
    gs = pltpu.PrefetchScalarGridSpec(
        num_scalar_prefetch=2,
        grid=(DIFF_HEADS, qi.shape[0]),
        in_specs=[pl.BlockSpec((ATT_BQ, hw), lambda h, t, qi, kj: (qi[t], h)),
                  pl.BlockSpec((bk, hw), lambda h, t, qi, kj: (kj[t], h)),
                  pl.BlockSpec((V_EXT, bk), lambda h, t, qi, kj: (h, kj[t])),
                  small(lq1), small(lk1), small(lq2), small(lk2), small(g_col)],
        out_specs=pl.BlockSpec((ATT_BQ, hw), lambda h, t, qi, kj: (qi[t], h)),
        scratch_shapes=[pltpu.VMEM((ATT_BQ, hw), BF16), pltpu.VMEM((ATT_BQ, hw), BF16)] + _unit_scratch(2, bk),
    )
    return pl.pallas_call(
        _diff_attn_kernel,
        grid_spec=gs,
        out_shape=jax.ShapeDtypeStruct((s, DIFF_W), BF16),
        compiler_params=_cparams(("arbitrary", "arbitrary")),
        name="diff_attn",
    )(qi, kj, dq, dk, dvt, lq1, lk1, lq2, lk2, g_col)


def _pack_words(y):
    lo = pltpu.bitcast(y[:, :HALF].astype(BF16).astype(F32), U32) >> 16
    hi = pltpu.bitcast(y[:, HALF:].astype(BF16).astype(F32), U32) & jnp.uint32(0xFFFF0000)
    return lo | hi


def _store_packed(ref, words, rows):
    for s in range(ROW_TILES):
        ref[pl.ds(s, rows, stride=ROW_TILES), :] = words[:, s * LANES:(s + 1) * LANES]


def _load_packed(ref, rows):
    return jnp.concatenate(
        [ref[pl.ds(s, rows, stride=ROW_TILES), :] for s in range(ROW_TILES)], axis=1)


def _unpack_words(w):
    lo = pltpu.bitcast(w << 16, F32)
    hi = pltpu.bitcast(w & jnp.uint32(0xFFFF0000), F32)
    return lo, hi


def _outproj_kernel(om_ref, od_ref, x_ref, w_ref, gt_ref, gpost_ref, gpre_ref, sc_ref, sh_ref,
                    wr_ref, x1_ref, h2_ref, h2p_ref, lg_ref):
    n_sub = x_ref.shape[0] // TM_PROJ
    sub = lambda r, j: r.at[pl.ds(j * TM_PROJ, TM_PROJ)]
    w_hi, w_lo = _split_bf16(wr_ref[...])
    ne = w_hi.shape[0]
    w_hl = jnp.concatenate([w_hi, w_lo], axis=0)
    ys = [_dot(jnp.concatenate([sub(om_ref, j)[...], sub(od_ref, j)[...]], axis=1), w_ref[...])
          for j in range(n_sub)]
    for j, y in enumerate(ys):
        x1 = sub(x_ref, j)[...] + gt_ref[...] * (_rms(y) * gpost_ref[...])
        sub(x1_ref, j)[...] = x1
        h2 = _rms(x1) * gpre_ref[...] * (1.0 + sc_ref[...]) + sh_ref[...]
        h_hi, h_lo = _split_bf16(h2)
        sub(h2_ref, j)[...] = h_hi
        _store_packed(h2p_ref.at[pl.ds(j * TM_PROJ * ROW_TILES, TM_PROJ * ROW_TILES)], _pack_words(h2), TM_PROJ)
        a = _dot_nt(w_hl, h_hi)
        lg_ref[:, j * TM_PROJ:(j + 1) * TM_PROJ] = a[0:ne] + a[ne:] + _dot_nt(w_hi, h_lo)


def _outproj(om, od, x, w_out, gt, gpost, gpre, sc, sh, wr_t):
    s, d = x.shape
    tm = min(TM_OUT, s)
    row = lambda n: pl.BlockSpec((tm, n), lambda i: (i, 0))
    vec = _const_spec((1, d))
    return pl.pallas_call(
        _outproj_kernel,
        grid=(s // tm,),
        in_specs=[row(om.shape[1]), row(od.shape[1]), row(d), _const_spec(w_out.shape),
                  vec, vec, vec, vec, vec, _const_spec(wr_t.shape)],
        out_specs=[row(d), row(d),
                   pl.BlockSpec((tm * ROW_TILES, LANES), lambda i: (i, 0)),
                   pl.BlockSpec((N_EXPERTS, tm), lambda i: (0, i))],
        out_shape=[jax.ShapeDtypeStruct((s, d), F32),
                   jax.ShapeDtypeStruct((s, d), BF16),
                   jax.ShapeDtypeStruct((s * ROW_TILES, LANES), U32),
                   jax.ShapeDtypeStruct((N_EXPERTS, s), F32)],
        compiler_params=_cparams(("arbitrary",)),
        name="out_proj",
    )(om, od, x, w_out, gt, gpost, gpre, sc, sh, wr_t)


def _route_kernel(lg_ref, b_ref, eidx_ref, w_ref, rank_ref, cnt_ref, carry):
    step = pl.program_id(0)

    @pl.when(step == 0)
    def _():
        carry[...] = jnp.zeros(carry.shape, F32)

    tr = lg_ref.shape[1]
    ninf = jnp.float32(-jnp.inf)
    ie = lax.broadcasted_iota(I32, (GROUP_SIZE, tr), 0)
    rmax = lambda a: jnp.max(a, axis=0, keepdims=True)
    rmin = lambda a: jnp.min(a, axis=0, keepdims=True)
    rsum = lambda a: jnp.sum(a, axis=0, keepdims=True)

    sc, bi, gscore = [], [], []
    for g in range(N_GROUPS):
        lg = lg_ref[g * GROUP_SIZE:(g + 1) * GROUP_SIZE, :]
        s = jax.nn.sigmoid(lg)
        b = s + b_ref[g * GROUP_SIZE:(g + 1) * GROUP_SIZE, :]
        m1 = rmax(b)
        i1 = rmin(jnp.where(b == m1, ie, GROUP_SIZE))
        m2 = rmax(jnp.where(ie == i1, ninf, b))
        sc.append(s)
        bi.append(b)
        gscore.append(m1 + m2)

    gsel = [jnp.zeros((1, tr), I32) for _ in range(N_GROUPS)]
    for _ in range(TOPK_GROUPS):
        gm = functools.reduce(jnp.maximum, gscore)
        gi = functools.reduce(
            jnp.minimum, [jnp.where(gscore[g] == gm, g, N_GROUPS) for g in range(N_GROUPS)])
        for g in range(N_GROUPS):
            hit = gi == g
            gsel[g] = jnp.where(hit, 1, gsel[g])
            gscore[g] = jnp.where(hit, ninf, gscore[g])

    masked = [jnp.where(jnp.broadcast_to(gsel[g], (GROUP_SIZE, tr)) > 0, bi[g], ninf)
              for g in range(N_GROUPS)]
    sel = [jnp.zeros((GROUP_SIZE, tr), F32) for _ in range(N_GROUPS)]
    idxs, svals = [], []
    for _ in range(TOP_K):
        m = functools.reduce(jnp.maximum, [rmax(x) for x in masked])
        idx = functools.reduce(
            jnp.minimum,
            [rmin(jnp.where(masked[g] == m, ie + g * GROUP_SIZE, N_EXPERTS)) for g in range(N_GROUPS)])
        sv = jnp.zeros((1, tr), F32)
        for g in range(N_GROUPS):
            hit = (ie + g * GROUP_SIZE) == idx
            sv = sv + rsum(jnp.where(hit, sc[g], 0.0))
            masked[g] = jnp.where(hit, ninf, masked[g])
            sel[g] = jnp.where(hit, 1.0, sel[g])
        idxs.append(idx)
        svals.append(sv)
    tot = functools.reduce(lambda a, b: a + b, svals)

    sel2 = jnp.concatenate(sel, axis=0)
    before = (lax.broadcasted_iota(I32, (tr, tr), 0) < lax.broadcasted_iota(I32, (tr, tr), 1))
    rank2 = _dot(sel2.astype(BF16), before.astype(BF16)) + carry[:, 0:1]
    for k in range(TOP_K):
        rk = jnp.zeros((1, tr), F32)
        for g in range(N_GROUPS):
            hit = (ie + g * GROUP_SIZE) == idxs[k]
            rk = rk + rsum(jnp.where(hit, rank2[g * GROUP_SIZE:(g + 1) * GROUP_SIZE, :], 0.0))
        eidx_ref[k:k + 1, :] = idxs[k]
        w_ref[k:k + 1, :] = svals[k] / tot * ROUTED_SCALE
        rank_ref[k:k + 1, :] = rk.astype(I32)
    carry[...] = carry[...] + jnp.sum(sel2, axis=1, keepdims=True)
    cnt_ref[...] = carry[...]


def _route(lg_t, b_col):
    e, t = lg_t.shape
    tr = min(TR, t)
    tok = lambda: pl.BlockSpec((TOP_K, tr), lambda i: (0, i))
    return pl.pallas_call(
        _route_kernel,
        grid=(t // tr,),
        in_specs=[pl.BlockSpec((e, tr), lambda i: (0, i)), pl.BlockSpec((e, 1), lambda i: (0, 0))],
        out_specs=[tok(), tok(), tok(), pl.BlockSpec((e, LANES), lambda i: (0, 0))],
        out_shape=[jax.ShapeDtypeStruct((TOP_K, t), I32),
                   jax.ShapeDtypeStruct((TOP_K, t), F32),
                   jax.ShapeDtypeStruct((TOP_K, t), I32),
                   jax.ShapeDtypeStruct((e, LANES), F32)],
        scratch_shapes=[pltpu.VMEM((e, LANES), F32)],
        compiler_params=_cparams(("arbitrary",)),
        name="route",
    )(lg_t, b_col)


def _pos_kernel(pstart_ref, eidx_ref, rank_ref, pos_ref):
    e = eidx_ref[...]
    pos = rank_ref[...]
    for x in range(N_EXPERTS):
        pos = pos + jnp.where(e == x, pstart_ref[x], 0)
    pos_ref[...] = pos


def _positions(pstart, eidx, rank):
    k, t = eidx.shape
    tr = min(TR, t)
    gs = pltpu.PrefetchScalarGridSpec(
        num_scalar_prefetch=1,
        grid=(t // tr,),
        in_specs=[pl.BlockSpec((k, tr), lambda i, p: (0, i)), pl.BlockSpec((k, tr), lambda i, p: (0, i))],
        out_specs=pl.BlockSpec((k, tr), lambda i, p: (0, i)),
    )
    return pl.pallas_call(
        _pos_kernel, grid_spec=gs,
        out_shape=jax.ShapeDtypeStruct((k, t), I32),
        compiler_params=_cparams(("arbitrary",)),
        name="positions",
    )(pstart, eidx, rank)


def _row_copy(src, src_row, dst, dst_row, sem):
    return pltpu.make_async_copy(src.at[pl.ds(src_row * ROW_TILES, ROW_TILES)],
                                 dst.at[pl.ds(dst_row * ROW_TILES, ROW_TILES)], sem)


FILL_CHUNKS = (128, 64, 32, 16, 8, 4, 2, 1)


def _zero_fill(fs_ref, fl_ref, tail_ref, zbuf, xs_ref, sem, wait):
    def chunk(row, n):
        return pltpu.make_async_copy(zbuf.at[pl.ds(0, n * ROW_TILES)],
                                     xs_ref.at[pl.ds(row * ROW_TILES, n * ROW_TILES)], sem)

    def go(cp):
        if wait:
            cp.wait()
        else:
            cp.start()

    def per_expert(e, c):
        pad = fl_ref[e]
        row = fs_ref[e]
        for n in FILL_CHUNKS:
            @pl.when((pad & n) != 0)
            def _(row=row, n=n):
                go(chunk(row, n))
            row = row + (pad & n)
        return c

    lax.fori_loop(0, N_EXPERTS, per_expert, 0)

    def per_tail(b, c):
        for part in range(EXP_BLK // FILL_CHUNKS[0]):
            go(chunk(b * EXP_BLK + part * FILL_CHUNKS[0], FILL_CHUNKS[0]))
        return c

    lax.fori_loop(tail_ref[0], tail_ref[1], per_tail, 0)


def _dispatch_kernel(fs_ref, fl_ref, tail_ref, pos_ref, h_ref, xs_ref, zbuf, sem, fill_sem):
    td = pos_ref.shape[1]
    step = pl.program_id(0)

    @pl.when(step == 0)
    def _():
        zbuf[...] = jnp.zeros(zbuf.shape, zbuf.dtype)
        _zero_fill(fs_ref, fl_ref, tail_ref, zbuf, xs_ref, fill_sem, wait=False)

    def issue(t, c):
        for k in range(TOP_K):
            _row_copy(h_ref, t, xs_ref, pos_ref[k, t], sem).start(priority=k % 2)
        return c

    lax.fori_loop(0, td, issue, 0, unroll=4)
    for k in range(TOP_K):
        pltpu.make_async_copy(h_ref, xs_ref.at[pl.ds(0, td * ROW_TILES)], sem).wait()

    @pl.when(step == pl.num_programs(0) - 1)
    def _():
        _zero_fill(fs_ref, fl_ref, tail_ref, zbuf, xs_ref, fill_sem, wait=True)


def _dispatch(fill_start, fill_len, tail, pos, h2p, m_pad):
    k, t = pos.shape
    td = min(TD, t)
    gs = pltpu.PrefetchScalarGridSpec(
        num_scalar_prefetch=3,
        grid=(t // td,),
        in_specs=[pl.BlockSpec((k, td), lambda i, *_: (0, i), memory_space=pltpu.SMEM),
                  pl.BlockSpec((td * ROW_TILES, LANES), lambda i, *_: (i, 0))],
        out_specs=pl.BlockSpec(memory_space=pl.ANY),
        scratch_shapes=[pltpu.VMEM((FILL_CHUNKS[0] * ROW_TILES, LANES), U32),
                        pltpu.SemaphoreType.DMA, pltpu.SemaphoreType.DMA],
    )
    return pl.pallas_call(
        _dispatch_kernel,
        grid_spec=gs,
        out_shape=jax.ShapeDtypeStruct((m_pad * ROW_TILES, LANES), U32),
        compiler_params=_cparams(("arbitrary",)),
        name="dispatch",
    )(fill_start, fill_len, tail, pos, h2p)


def _weight_copies(w_hbm, w_buf, sem, e, s):
    return [pltpu.make_async_copy(h.at[e], b.at[s], sem.at[s]) for h, b in zip(w_hbm, w_buf)]


def _experts_kernel(be_ref, nu_ref, fresh_ref, nxt_ref, slot_ref, x_ref, wg_hbm, wu_hbm, wd_hbm, y_ref,
                    wg_buf, wu_buf, wd_buf, wg_s, wu_s, wd_s, sem):
    step = pl.program_id(0)
    b0 = step * BLKS_PER_STEP
    blk_rows = x_ref.shape[0] // BLKS_PER_STEP
    w_hbm = (wg_hbm, wu_hbm, wd_hbm)
    w_buf = (wg_buf, wu_buf, wd_buf)

    def maybe_weights(b):
        @pl.when(fresh_ref[b] == 1)
        def _():
            s = slot_ref[b]

            @pl.when(b == 0)
            def _():
                for c in _weight_copies(w_hbm, w_buf, sem, be_ref[0], 0):
                    c.start()

            for c in _weight_copies(w_hbm, w_buf, sem, be_ref[b], s):
                c.wait()

            @pl.when(nxt_ref[b] >= 0)
            def _():
                for c in _weight_copies(w_hbm, w_buf, sem, nxt_ref[b], 1 - s):
                    c.start()

            wg_s[...] = wg_buf[s].astype(BF16)
            wu_s[...] = wu_buf[s].astype(BF16)
            wd_s[...] = wd_buf[s].astype(BF16)

    def swiglu(first, n):
        x_blk = x_ref.at[pl.ds(first * blk_rows, n * blk_rows)]
        y_blk = y_ref.at[pl.ds(first * blk_rows, n * blk_rows)]
        rows = n * blk_rows // ROW_TILES
        xa, xb = _unpack_words(_load_packed(x_blk, rows))
        xa = xa.astype(BF16)
        xb = xb.astype(BF16)
        g = _dot(xa, wg_s[0:HALF, :]) + _dot(xb, wg_s[HALF:, :])
        u = _dot(xa, wu_s[0:HALF, :]) + _dot(xb, wu_s[HALF:, :])
        a = (g * jax.nn.sigmoid(g) * u).astype(BF16)
        _store_packed(y_blk, _pack_words(_dot(a, wd_s[...])), rows)

    def zero(first, n):
        y_ref[pl.ds(first * blk_rows, n * blk_rows), :] = jnp.zeros((n * blk_rows, LANES), y_ref.dtype)

    n_used = jnp.clip(nu_ref[0] - b0, 0, BLKS_PER_STEP)
    same = jnp.logical_and(n_used == BLKS_PER_STEP, be_ref[b0] == be_ref[b0 + 1])

    @pl.when(same)
    def _():
        maybe_weights(b0)
        swiglu(0, BLKS_PER_STEP)

    @pl.when(jnp.logical_and(jnp.logical_not(same), n_used > 0))
    def _():
        maybe_weights(b0)
        swiglu(0, 1)

        @pl.when(n_used > 1)
        def _():
            maybe_weights(b0 + 1)
            swiglu(1, 1)

        @pl.when(n_used == 1)
        def _():
            zero(1, 1)

    @pl.when(n_used == 0)
    def _():
        zero(0, BLKS_PER_STEP)


def _experts(blk_e, nused, fresh, nxt_e, slot, xs, w_gate, w_up, w_down, blk):
    n_blk = blk_e.shape[0]
    assert n_blk % BLKS_PER_STEP == 0 and BLKS_PER_STEP == 2
    ne, d, de = w_gate.shape
    rows = BLKS_PER_STEP * blk * ROW_TILES
    xmap = lambda i, be, nu, fr, nx, sl: (jnp.minimum(i, (nu[0] - 1) // BLKS_PER_STEP), 0)
    hbm = pl.BlockSpec(memory_space=pl.ANY)
    gs = pltpu.PrefetchScalarGridSpec(
        num_scalar_prefetch=5,
        grid=(n_blk // BLKS_PER_STEP,),
        in_specs=[pl.BlockSpec((rows, LANES), xmap), hbm, hbm, hbm],
        out_specs=pl.BlockSpec((rows, LANES), lambda i, *_: (i, 0)),
        scratch_shapes=[pltpu.VMEM((2, d, de), F32), pltpu.VMEM((2, d, de), F32), pltpu.VMEM((2, de, d), F32),
                        pltpu.VMEM((d, de), BF16), pltpu.VMEM((d, de), BF16), pltpu.VMEM((de, d), BF16),
                        pltpu.SemaphoreType.DMA((2,))],
    )
    return pl.pallas_call(
        _experts_kernel, grid_spec=gs,
        out_shape=jax.ShapeDtypeStruct(xs.shape, U32),
        compiler_params=_cparams(("arbitrary",)),
        name="experts",
    )(blk_e, nused, fresh, nxt_e, slot, xs, w_gate, w_up, w_down)


def _combine_kernel(pos_ref, wt_ref, ys_ref, h_ref, x1_ref, wsg_ref, wsu_ref, wsd_ref,
                    gt_ref, gpost_ref, o_ref, gbuf, sem):
    td = pos_ref.shape[1]

    def issue(t, c):
        for k in range(TOP_K):
            _row_copy(ys_ref, pos_ref[k, t], gbuf.at[k], t, sem).start(priority=k % 2)
        return c

    lax.fori_loop(0, td, issue, 0, unroll=4)

    hb = h_ref[...]
    g = _dot(hb, wsg_ref[...])
    u = _dot(hb, wsu_ref[...])
    y = _dot((g * jax.nn.sigmoid(g) * u).astype(BF16), wsd_ref[...])

    wpad = jnp.concatenate([wt_ref[...], jnp.zeros((LANES - TOP_K, td), F32)], axis=0)
    wcol = jnp.concatenate([wpad[:, j * LANES:(j + 1) * LANES].T for j in range(td // LANES)], axis=0)

    for k in range(TOP_K):
        pltpu.make_async_copy(ys_ref.at[pl.ds(0, td * ROW_TILES)], gbuf.at[k], sem).wait()
    lo = jnp.zeros((td, HALF), F32)
    hi = jnp.zeros((td, HALF), F32)
    for k in range(TOP_K):
        a, b = _unpack_words(_load_packed(gbuf.at[k], td))
        wk = wcol[:, k:k + 1]
        lo = lo + wk * a
        hi = hi + wk * b
    y = y + jnp.concatenate([lo, hi], axis=1)
    o_ref[...] = x1_ref[...] + gt_ref[...] * (_rms(y) * gpost_ref[...])


def _combine(pos, wts, ys, h2, x1, wsg, wsu, wsd, gt, gpost):
    k, t = pos.shape
    d = x1.shape[1]
    td = min(TD, t)
    assert td % LANES == 0
    row = lambda n: pl.BlockSpec((td, n), lambda i: (i, 0))
    vec = _const_spec((1, d))
    return pl.pallas_call(
        _combine_kernel,
        grid=(t // td,),
        in_specs=[pl.BlockSpec((k, td), lambda i: (0, i), memory_space=pltpu.SMEM),
                  pl.BlockSpec((k, td), lambda i: (0, i)),
                  pl.BlockSpec(memory_space=pl.ANY),
                  row(d), row(d), _const_spec(wsg.shape), _const_spec(wsu.shape), _const_spec(wsd.shape),
                  vec, vec],
        out_specs=row(d),
        out_shape=jax.ShapeDtypeStruct((t, d), F32),
        scratch_shapes=[pltpu.VMEM((k, td * ROW_TILES, LANES), U32), pltpu.SemaphoreType.DMA],
        compiler_params=_cparams(("arbitrary",)),
        name="combine",
    )(pos, wts, ys, h2, x1, wsg, wsu, wsd, gt, gpost)


def _rope_tables(seq, dim):
    pos = jnp.arange(seq, dtype=F32)
    inv = ROPE_THETA ** (-jnp.arange(0, dim, 2, dtype=F32) / dim)
    ang = pos[:, None] * inv[None, :]
    return jnp.cos(ang), jnp.sin(ang)


def _lane_tables(seq, dim, period):
    cos, sin = _rope_tables(seq, dim)
    half = dim // 2
    cos = jnp.tile(cos, (1, LANES // half))
    sin = jnp.tile(sin, (1, LANES // half))
    r = jnp.arange(LANES) % period
    c = jnp.where(r < dim, cos, 1.0)
    sa = jnp.where(r < half, -sin, 0.0)
    sb = jnp.where((r >= half) & (r < dim), sin, 0.0)
    return c, sa, sb


def _moe(h2, h2p, x1, lg_t, b_router, w_gate, w_up, w_down, wsg, wsu, wsd, gt_f, g_post_ffn):
    t = h2.shape[0]
    blk = EXP_BLK
    m_pad = t * TOP_K + N_EXPERTS * blk
    eidx, wts, rank, cnt = _route(lg_t, b_router.reshape(N_EXPERTS, 1))
    counts = cnt[:, 0].astype(I32)
    padded = (counts + blk - 1) // blk * blk
    pend = jnp.cumsum(padded)
    pstart = pend - padded
    n_blk = m_pad // blk
    blk_start = jnp.arange(n_blk, dtype=I32) * blk
    blk_e = jnp.minimum(jnp.sum(pend[None, :] <= blk_start[:, None], axis=1), N_EXPERTS - 1).astype(I32)
    nused = (pend[-1:] // blk).astype(I32)
    pos = _positions(pstart.astype(I32), eidx, rank)
    tail = jnp.concatenate([nused, jnp.full((1,), n_blk, I32)])
    xs = _dispatch((pstart + counts).astype(I32), (padded - counts).astype(I32), tail, pos, h2p, m_pad)
    idx = jnp.arange(n_blk, dtype=I32)
    fresh = (idx < nused[0]) & (blk_e != jnp.concatenate([jnp.full((1,), -1, I32), blk_e[:-1]]))
    slot = ((jnp.cumsum(fresh.astype(I32)) - 1) % 2).astype(I32)
    first_at_or_after = lax.cummin(jnp.where(fresh, idx, n_blk)[::-1])[::-1]
    nxt_idx = jnp.concatenate([first_at_or_after[1:], jnp.full((1,), n_blk, I32)])
    nxt_e = jnp.where(nxt_idx < n_blk, blk_e[jnp.minimum(nxt_idx, n_blk - 1)], -1).astype(I32)
    ys = _experts(blk_e, nused, fresh.astype(I32), nxt_e, slot, xs, w_gate, w_up, w_down, blk)
    return _combine(pos, wts, ys, h2, x1, wsg, wsu, wsd, gt_f, g_post_ffn)


def _layer(x, c, w_ada, b_ada, g_pre_mix, w_in, g_q_lat, w_uq, g_kv_lat, w_ukv,
           lq1, lk1, lq2, lk2, g_diff_sub, w_out, g_post_mix, g_pre_ffn, w_router, b_router,
           w_gate, w_up, w_down, ws_gate, ws_up, ws_down, g_post_ffn):
    s, d = x.shape
    row = lambda a: a.reshape(1, -1)

    mod = _ada(c, w_ada, row(b_ada))
    sh_a, sc_a, gt_a, sh_f, sc_f, gt_f = [mod[:, i * d:(i + 1) * d] for i in range(6)]

    o1 = MLA_Q_RANK + MLA_KV_RANK
    o2 = o1 + MLA_ROPE
    o3 = o2 + 2 * DIFF_W
    w_lat, w_qk, w_kpe, wdv_t = _split_w_in(w_in, (o1, o2, o3))
    wq = w_uq.reshape(MLA_Q_RANK, MLA_HEADS, MLA_NOPE + MLA_ROPE)
    wq = jnp.pad(wq, ((0, 0), (0, 0), (0, MLA_QK_PAD - MLA_NOPE - MLA_ROPE)))
    wuq_r = wq.reshape(MLA_Q_RANK, MLA_HEADS * MLA_QK_PAD).astype(BF16)
    wkv = w_ukv.reshape(MLA_KV_RANK, MLA_HEADS, MLA_NOPE + MLA_V)
    wuk_r = wkv[:, :, :MLA_NOPE].reshape(MLA_KV_RANK, -1).astype(BF16)
    wmv_t = wkv[:, :, MLA_NOPE:].reshape(MLA_KV_RANK, -1).T.astype(BF16)
    dq, dk, dvt, qc, kc, vt = _inproj(
        x, row(g_pre_mix), sc_a, sh_a, w_lat, w_qk, w_kpe, wdv_t,
        _lane_tables(s, DIFF_ROT, DIFF_HEAD_DIM), row(g_q_lat), row(g_kv_lat), wuq_r, wuk_r, wmv_t,
        _lane_tables(s, MLA_ROPE, MLA_ROPE))

    o_mla = _mla_attn(qc, kc, vt)
    o_diff = _diff_attn(dq, dk, dvt, row(lq1), row(lk1), row(lq2), row(lk2), g_diff_sub.reshape(-1, 1))

    x1, h2, h2p, lg_t = _outproj(o_mla, o_diff, x, w_out.astype(BF16), gt_a, row(g_post_mix),
                                 row(g_pre_ffn), sc_f, sh_f, w_router.T)
    return _moe(h2, h2p, x1, lg_t, b_router, w_gate, w_up, w_down,
                ws_gate.astype(BF16), ws_up.astype(BF16), ws_down.astype(BF16), gt_f, row(g_post_ffn))


def kernel(x, c, w_ada, b_ada, g_pre_mix, w_in, g_q_lat, w_uq, g_kv_lat, w_ukv, lambda_q1, lambda_k1, lambda_q2, lambda_k2, g_diff_sub, w_out, g_post_mix, g_pre_ffn, w_router, b_router, w_gate, w_up, w_down, ws_gate, ws_up, ws_down, g_post_ffn):
    assert x.shape[0] == 1 and w_ada.shape[0] == 1
    out = _layer(x[0], c, w_ada[0], b_ada[0], g_pre_mix[0], w_in[0], g_q_lat[0], w_uq[0],
                 g_kv_lat[0], w_ukv[0], lambda_q1[0], lambda_k1[0], lambda_q2[0], lambda_k2[0],
                 g_diff_sub[0], w_out[0], g_post_mix[0], g_pre_ffn[0], w_router[0], b_router[0],
                 w_gate[0], w_up[0], w_down[0], ws_gate[0], ws_up[0], ws_down[0], g_post_ffn[0])
    return out[None]
```

```python
import functools
import math

import jax
import jax.numpy as jnp
import numpy as np
from jax import lax
from jax.experimental import pallas as pl
from jax.experimental.pallas import tpu as pltpu

F32 = jnp.float32
BF16 = jnp.bfloat16
U32 = jnp.uint32
I32 = jnp.int32

D_MODEL = 2048
CHUNK = 64
ROPE_THETA = 500000.0
EPS = 1e-6
LOG2E = 1.4426950408889634

MLA_HEADS = 8
MLA_Q_RANK = 768
MLA_KV_RANK = 512
MLA_NOPE = 128
MLA_ROPE = 64
MLA_V = 128
MLA_QK_PAD = 256

DIFF_HEADS = 8
DIFF_HEAD_DIM = 64
DIFF_ROT = DIFF_HEAD_DIM // 4
DIFF_W = DIFF_HEADS * 2 * DIFF_HEAD_DIM

N_EXPERTS = 64
TOP_K = 8
N_GROUPS = 8
GROUP_SIZE = N_EXPERTS // N_GROUPS
TOPK_GROUPS = 4
D_EXPERT = 512
ROUTED_SCALE = 2.5
LAMBDA_INIT = 0.8 - 0.6 * math.exp(-0.3 * 0)

LANES = 128
SUBLANES = 8
HALF = D_MODEL // 2
ROW_TILES = HALF // LANES

TM_PROJ = 256
TM_OUT = 2 * TM_PROJ
ATT_BQ = 4096
MLA_BK = 2048
DIFF_BK = 1024
UNIT_W = 512
V_EXT = MLA_V + 16
EXP_BLK = 256
BLKS_PER_STEP = 2
TD = 256
TR = 512
TN_ADA = 1024

VMEM_LIMIT = 56 * 1024 * 1024

assert ROW_TILES == SUBLANES


def _cparams(sem, vmem=VMEM_LIMIT, flags=None):
    return pltpu.CompilerParams(dimension_semantics=sem, vmem_limit_bytes=vmem, flags=flags)


def _dot(a, b):
    return jnp.dot(a, b, preferred_element_type=F32)


def _dot_nt(a, b):
    return lax.dot_general(a, b, (((1,), (1,)), ((), ())), preferred_element_type=F32)


def _rms(x):
    return x * lax.rsqrt(jnp.mean(x * x, axis=-1, keepdims=True) + EPS)


def _split_bf16(x):
    hi = x.astype(BF16)
    lo = (x - hi.astype(F32)).astype(BF16)
    return hi, lo


def _const_spec(shape):
    nd = len(shape)
    return pl.BlockSpec(shape, lambda *a: (0,) * nd, pipeline_mode=pl.Buffered(1))


def _ada_kernel(c_ref, w_ref, b_ref, o_ref):
    c = c_ref[...]
    a = c * jax.nn.sigmoid(c)
    a8 = jnp.broadcast_to(a, (SUBLANES, a.shape[1]))
    a_hi, a_lo = _split_bf16(a8)
    w_hi, w_lo = _split_bf16(w_ref[...])
    r = _dot(a_hi, w_hi) + _dot(a_lo, w_hi) + _dot(a_hi, w_lo)
    o_ref[...] = r[0:1] + b_ref[...]


def _ada(c, w, b):
    d, n = w.shape
    return pl.pallas_call(
        _ada_kernel,
        grid=(n // TN_ADA,),
        in_specs=[pl.BlockSpec((1, d), lambda j: (0, 0)),
                  pl.BlockSpec((d, TN_ADA), lambda j: (0, j)),
                  pl.BlockSpec((1, TN_ADA), lambda j: (0, j))],
        out_specs=pl.BlockSpec((1, TN_ADA), lambda j: (0, j)),
        out_shape=jax.ShapeDtypeStruct((1, n), F32),
        compiler_params=_cparams(("arbitrary",)),
        name="ada_mod",
    )(c, w, b)


def _tile_lanes(t, reps):
    return jnp.concatenate([t] * reps, axis=1)


def _rope_lanes(x, c, sa, sb, half):
    n = x.shape[1]
    return x * c + pltpu.roll(x, n - half, 1) * sa + pltpu.roll(x, half, 1) * sb


def _split_w_in_kernel(offs, wt_ref, lat_ref, qk_ref, kpe_ref, dvt_ref):
    o1, o2, o3 = offs
    wt = wt_ref[...]
    lat_ref[...] = wt[:o1, :].T.astype(BF16)
    qk_ref[...] = wt[o2:o3, :].T.astype(BF16)
    kpe = wt[o1:o2, :].T
    kpe_ref[...] = jnp.concatenate([kpe, jnp.zeros_like(kpe)], axis=1).astype(BF16)
    dvt_ref[...] = wt[o3:, :].astype(BF16)


def _split_w_in(w_in_t, offs):
    n, d = w_in_t.shape
    o1, o2, o3 = offs
    tk = TM_PROJ
    return pl.pallas_call(
        functools.partial(_split_w_in_kernel, offs),
        grid=(d // tk,),
        in_specs=[pl.BlockSpec((n, tk), lambda i: (0, i))],
        out_specs=[pl.BlockSpec((tk, o1), lambda i: (i, 0)), pl.BlockSpec((tk, o3 - o2), lambda i: (i, 0)),
                   pl.BlockSpec((tk, LANES), lambda i: (i, 0)), pl.BlockSpec((n - o3, tk), lambda i: (0, i))],
        out_shape=[jax.ShapeDtypeStruct((d, o1), BF16), jax.ShapeDtypeStruct((d, o3 - o2), BF16),
                   jax.ShapeDtypeStruct((d, LANES), BF16), jax.ShapeDtypeStruct((n - o3, d), BF16)],
        compiler_params=_cparams(("arbitrary",)),
        name="split_w_in",
    )(w_in_t)


def _mla_heads(lat, kpe_raw, gq_ref, gkv_ref, wuq_ref, wuk_ref, wvt_ref, c, sa, sb, q_ref, k_ref, vt_ref):
    qn = (_rms(lat[:, :MLA_Q_RANK]) * gq_ref[...]).astype(BF16)
    kvn = (_rms(lat[:, MLA_Q_RANK:]) * gkv_ref[...]).astype(BF16)
    q = _dot(qn, wuq_ref[...])
    kn = _dot(kvn, wuk_ref[...])
    half = MLA_ROPE // 2
    qs = (MLA_NOPE + MLA_ROPE) ** -0.5 * LOG2E
    kpe = _rope_lanes(kpe_raw, c, sa, sb, half).astype(BF16)
    for h in range(MLA_HEADS):
        o = h * MLA_QK_PAD
        q_ref[:, o:o + LANES] = (q[:, o:o + LANES] * qs).astype(BF16)
        q_ref[:, o + LANES:o + 2 * LANES] = (
            _rope_lanes(q[:, o + LANES:o + 2 * LANES], c, sa, sb, half) * qs).astype(BF16)
        k_ref[:, o:o + LANES] = kn[:, h * MLA_NOPE:(h + 1) * MLA_NOPE].astype(BF16)
        k_ref[:, o + LANES:o + 2 * LANES] = kpe
    _store_vt_ext(vt_ref, _dot_nt(wvt_ref[...], kvn), MLA_HEADS)


def _inproj_kernel(x_ref, g_ref, sc_ref, sh_ref, wlat_ref, wqk_ref, wkpe_ref, wdvt_ref,
                   dc_ref, dsa_ref, dsb_ref, gq_ref, gkv_ref, wuq_ref, wuk_ref, wmvt_ref,
                   mc_ref, msa_ref, msb_ref,
                   dq_ref, dk_ref, dvt_ref, q_ref, k_ref, vt_ref):
    x = x_ref[...]
    h = _rms(x) * g_ref[...] * (1.0 + sc_ref[...]) + sh_ref[...]
    hb = h.astype(BF16)
    reps = DIFF_W // LANES
    c = _tile_lanes(dc_ref[...], reps)
    sa = _tile_lanes(dsa_ref[...], reps)
    sb = _tile_lanes(dsb_ref[...], reps)
    half = DIFF_ROT // 2
    q = _dot(hb, wqk_ref[:, 0:DIFF_W])
    dq_ref[...] = (_rope_lanes(q, c, sa, sb, half) * (DIFF_HEAD_DIM ** -0.5 * LOG2E)).astype(BF16)
    k = _dot(hb, wqk_ref[:, DIFF_W:])
    dk_ref[...] = _rope_lanes(k, c, sa, sb, half).astype(BF16)
    _store_vt_ext(dvt_ref, _dot_nt(wdvt_ref[...], hb), DIFF_HEADS)
    _mla_heads(_dot(hb, wlat_ref[...]), _dot(hb, wkpe_ref[...]), gq_ref, gkv_ref, wuq_ref, wuk_ref,
               wmvt_ref, mc_ref[...], msa_ref[...], msb_ref[...], q_ref, k_ref, vt_ref)


def _inproj(x, g, sc, sh, w_lat, w_qk, w_kpe, wdv_t, dtabs, gq, gkv, wuq_r, wuk_r, wmv_t, mtabs):
    s, d = x.shape
    tm = min(TM_PROJ, s)
    row = lambda n: pl.BlockSpec((tm, n), lambda i: (i, 0))
    col = lambda n: pl.BlockSpec((n, tm), lambda i: (0, i))
    const = lambda a: _const_spec(a.shape)
    hq = MLA_HEADS * MLA_QK_PAD
    return pl.pallas_call(
        _inproj_kernel,
        grid=(s // tm,),
        in_specs=[row(d), const(g), const(sc), const(sh), const(w_lat), const(w_qk), const(w_kpe),
                  const(wdv_t), row(LANES), row(LANES), row(LANES),
                  const(gq), const(gkv), const(wuq_r), const(wuk_r), const(wmv_t),
                  row(LANES), row(LANES), row(LANES)],
        out_specs=[row(DIFF_W), row(DIFF_W), col(DIFF_HEADS * V_EXT), row(hq), row(hq), col(MLA_HEADS * V_EXT)],
        out_shape=[jax.ShapeDtypeStruct((s, DIFF_W), BF16),
                   jax.ShapeDtypeStruct((s, DIFF_W), BF16),
                   jax.ShapeDtypeStruct((DIFF_HEADS * V_EXT, s), BF16),
                   jax.ShapeDtypeStruct((s, hq), BF16),
                   jax.ShapeDtypeStruct((s, hq), BF16),
                   jax.ShapeDtypeStruct((MLA_HEADS * V_EXT, s), BF16)],
        compiler_params=_cparams(("arbitrary",)),
        name="in_proj",
    )(x, g, sc, sh, w_lat, w_qk, w_kpe, wdv_t, *dtabs, gq, gkv, wuq_r, wuk_r, wmv_t, *mtabs)


def _pair_tables(nq, kv_per_q):
    n = [kv_per_q * (i + 1) for i in range(nq)]
    qi = np.concatenate([np.full(c, i) for i, c in enumerate(n)]).astype(np.int32)
    kj = np.concatenate([np.arange(c) for c in n]).astype(np.int32)
    return jnp.asarray(qi), jnp.asarray(kj)


def _ones_rows(n):
    rows = V_EXT - MLA_V
    return (lax.broadcasted_iota(I32, (rows, n), 0) == 0).astype(BF16)


def _store_vt_ext(vt_ref, vt, heads):
    ones = _ones_rows(vt.shape[1])
    for h in range(heads):
        vt_ref[h * V_EXT:h * V_EXT + MLA_V, :] = vt[h * MLA_V:(h + 1) * MLA_V, :].astype(BF16)
        vt_ref[h * V_EXT + MLA_V:(h + 1) * V_EXT, :] = ones


def _attn_step(q_refs, k_ref, vt_ref, s_bufs, m_refs, acc_refs, rel):
    w = UNIT_W
    nk = k_ref.shape[0]
    per_kv = nk // w
    specs = []
    for u in range(q_refs[0].shape[0] // w):
        d = None if rel is None else u - rel * per_kv
        if d is None or d >= per_kv:
            specs.append((u, nk, None))
        elif d >= 0:
            specs.append((u, (d + 1) * w, d * w))
    units = [(si, sp) for sp in specs for si in range(len(q_refs))]

    def scores(n):
        si, (u, rows, off) = units[n]
        s = _dot_nt(k_ref[0:rows, :], q_refs[si][u * w:(u + 1) * w, :])
        if off is not None:
            kc = lax.broadcasted_iota(I32, (rows, w), 0) // CHUNK
            qc = (lax.broadcasted_iota(I32, (rows, w), 1) + off) // CHUNK
            s = jnp.where(kc <= qc, s, -jnp.inf)
        s_bufs[n % 2][0:rows, :] = s
        return jnp.max(s, axis=0, keepdims=True)

    cmax_next = scores(0)
    for n, (si, (u, rows, _)) in enumerate(units):
        cmax = cmax_next
        if n + 1 < len(units):
            cmax_next = scores(n + 1)
        m_ref, acc_ref = m_refs[si][u], acc_refs[si][u]
        m_prev = m_ref[...]
        m_new = jnp.maximum(m_prev, cmax)
        m_ref[...] = m_new
        acc_ref[...] = jnp.exp2(m_prev - m_new) * acc_ref[...]
        p = jnp.exp2(s_bufs[n % 2][0:rows, :] - m_new).astype(BF16)
        acc_ref[...] += _dot(vt_ref[:, 0:rows], p)


def _init_stats(ms, accs):
    for m, a in zip(ms, accs):
        m[...] = jnp.full(m.shape, -jnp.inf, F32)
        a[...] = jnp.zeros(a.shape, F32)


def _normalized(acc_ref):
    acc = acc_ref[...]
    return acc[0:MLA_V, :] / acc[MLA_V:MLA_V + 1, :]


def _attn_phases(qi_ref, kj_ref, kv_per_q, init, step, finish):
    t = pl.program_id(1)
    rel = kj_ref[t] - kv_per_q * qi_ref[t]

    @pl.when(kj_ref[t] == 0)
    def _():
        init()

    @pl.when(rel < 0)
    def _():
        step(None)

    for r in range(kv_per_q):
        @pl.when(rel == r)
        def _(r=r):
            step(r)
            if r == kv_per_q - 1:
                finish()


def _unit_scratch(n_streams, bk):
    nu = ATT_BQ // UNIT_W
    return ([pltpu.VMEM((bk, UNIT_W), F32)] * 2
            + [pltpu.VMEM((1, UNIT_W), F32)] * (nu * n_streams)
            + [pltpu.VMEM((V_EXT, UNIT_W), F32)] * (nu * n_streams))


def _split_units(scr, n_streams):
    nu = ATT_BQ // UNIT_W
    ms = [list(scr[s * nu:(s + 1) * nu]) for s in range(n_streams)]
    accs = [list(scr[(n_streams + s) * nu:(n_streams + s + 1) * nu]) for s in range(n_streams)]
    return ms, accs


def _mla_attn_kernel(qi_ref, kj_ref, q_ref, k_ref, vt_ref, o_ref, *scr):
    s_bufs, (ms, accs) = scr[:2], _split_units(scr[2:], 1)

    def finish():
        for u, a in enumerate(accs[0]):
            o_ref[u * UNIT_W:(u + 1) * UNIT_W, :] = _normalized(a).T.astype(o_ref.dtype)

    _attn_phases(qi_ref, kj_ref, ATT_BQ // k_ref.shape[0],
                 lambda: _init_stats(ms[0], accs[0]),
                 lambda rel: _attn_step([q_ref], k_ref, vt_ref, s_bufs, ms, accs, rel),
                 finish)


def _mla_attn(qc, kc, vt):
    s = qc.shape[0]
    bk = MLA_BK
    assert s % ATT_BQ == 0
    qi, kj = _pair_tables(s // ATT_BQ, ATT_BQ // bk)
    gs = pltpu.PrefetchScalarGridSpec(
        num_scalar_prefetch=2,
        grid=(MLA_HEADS, qi.shape[0]),
        in_specs=[pl.BlockSpec((ATT_BQ, MLA_QK_PAD), lambda h, t, qi, kj: (qi[t], h)),
                  pl.BlockSpec((bk, MLA_QK_PAD), lambda h, t, qi, kj: (kj[t], h)),
                  pl.BlockSpec((V_EXT, bk), lambda h, t, qi, kj: (h, kj[t]))],
        out_specs=pl.BlockSpec((ATT_BQ, MLA_V), lambda h, t, qi, kj: (qi[t], h)),
        scratch_shapes=_unit_scratch(1, bk),
    )
    return pl.pallas_call(
        _mla_attn_kernel,
        grid_spec=gs,
        out_shape=jax.ShapeDtypeStruct((s, MLA_HEADS * MLA_V), BF16),
        compiler_params=_cparams(("arbitrary", "arbitrary")),
        name="mla_attn",
    )(qi, kj, qc, kc, vt)


def _diff_attn_kernel(qi_ref, kj_ref, q_ref, k_ref, vt_ref, lq1_ref, lk1_ref, lq2_ref, lk2_ref,
                      g_ref, o_ref, q0_scr, q1_scr, *scr):
    s_bufs, (ms, accs) = scr[:2], _split_units(scr[2:], 2)

    def init():
        q = q_ref[...]
        lane = lax.broadcasted_iota(I32, q.shape, 1)
        zero = jnp.zeros_like(q)
        q0_scr[...] = jnp.where(lane < DIFF_HEAD_DIM, q, zero)
        q1_scr[...] = jnp.where(lane >= DIFF_HEAD_DIM, q, zero)
        _init_stats(ms[0] + ms[1], accs[0] + accs[1])

    def finish():
        lam = (jnp.exp(jnp.sum(lq1_ref[...] * lk1_ref[...], axis=1, keepdims=True))
               - jnp.exp(jnp.sum(lq2_ref[...] * lk2_ref[...], axis=1, keepdims=True))
               + LAMBDA_INIT)
        for u, (a0, a1) in enumerate(zip(accs[0], accs[1])):
            o = _normalized(a0) - lam * _normalized(a1)
            o = o * lax.rsqrt(jnp.mean(o * o, axis=0, keepdims=True) + EPS)
            o_ref[u * UNIT_W:(u + 1) * UNIT_W, :] = (
                o * g_ref[...] * (1.0 - LAMBDA_INIT)).T.astype(o_ref.dtype)

    _attn_phases(qi_ref, kj_ref, ATT_BQ // k_ref.shape[0], init,
                 lambda rel: _attn_step([q0_scr, q1_scr], k_ref, vt_ref, s_bufs, ms, accs, rel),
                 finish)


def _diff_attn(dq, dk, dvt, lq1, lk1, lq2, lk2, g_col):
    s = dq.shape[0]
    bk = DIFF_BK
    assert s % ATT_BQ == 0
    qi, kj = _pair_tables(s // ATT_BQ, ATT_BQ // bk)
    hw = 2 * DIFF_HEAD_DIM
    small = lambda a: pl.BlockSpec(a.shape, lambda h, t, qi, kj: (0, 0))
    gs = pltpu.PrefetchScalarGridSpec(
        num_scalar_prefetch=2,
        grid=(DIFF_HEADS, qi.shape[0]),
        in_specs=[pl.BlockSpec((ATT_BQ, hw), lambda h, t, qi, kj: (qi[t], h)),
                  pl.BlockSpec((bk, hw), lambda h, t, qi, kj: (kj[t], h)),
                  pl.BlockSpec((V_EXT, bk), lambda h, t, qi, kj: (h, kj[t])),
                  small(lq1), small(lk1), small(lq2), small(lk2), small(g_col)],
        out_specs=pl.BlockSpec((ATT_BQ, hw), lambda h, t, qi, kj: (qi[t], h)),
        scratch_shapes=[pltpu.VMEM((ATT_BQ, hw), BF16), pltpu.VMEM((ATT_BQ, hw), BF16)] + _unit_scratch(2, bk),
    )
    return pl.pallas_call(
        _diff_attn_kernel,
        grid_spec=gs,
        out_shape=jax.ShapeDtypeStruct((s, DIFF_W), BF16),
        compiler_params=_cparams(("arbitrary", "arbitrary")),
        name="diff_attn",
    )(qi, kj, dq, dk, dvt, lq1, lk1, lq2, lk2, g_col)


def _pack_words(y):
    lo = pltpu.bitcast(y[:, :HALF].astype(BF16).astype(F32), U32) >> 16
    hi = pltpu.bitcast(y[:, HALF:].astype(BF16).astype(F32), U32) & jnp.uint32(0xFFFF0000)
    return lo | hi


def _store_packed(ref, words, rows):
    for s in range(ROW_TILES):
        ref[pl.ds(s, rows, stride=ROW_TILES), :] = words[:, s * LANES:(s + 1) * LANES]


def _load_packed(ref, rows):
    return jnp.concatenate(
        [ref[pl.ds(s, rows, stride=ROW_TILES), :] for s in range(ROW_TILES)], axis=1)


def _unpack_words(w):
    lo = pltpu.bitcast(w << 16, F32)
    hi = pltpu.bitcast(w & jnp.uint32(0xFFFF0000), F32)
    return lo, hi


def _outproj_kernel(om_ref, od_ref, x_ref, w_ref, gt_ref, gpost_ref, gpre_ref, sc_ref, sh_ref,
                    wr_ref, x1_ref, h2_ref, h2p_ref, lg_ref):
    n_sub = x_ref.shape[0] // TM_PROJ
    sub = lambda r, j: r.at[pl.ds(j * TM_PROJ, TM_PROJ)]
    w_hi, w_lo = _split_bf16(wr_ref[...])
    ne = w_hi.shape[0]
    w_hl = jnp.concatenate([w_hi, w_lo], axis=0)
    ys = [_dot(jnp.concatenate([sub(om_ref, j)[...], sub(od_ref, j)[...]], axis=1), w_ref[...])
          for j in range(n_sub)]
    for j, y in enumerate(ys):
        x1 = sub(x_ref, j)[...] + gt_ref[...] * (_rms(y) * gpost_ref[...])
        sub(x1_ref, j)[...] = x1
        h2 = _rms(x1) * gpre_ref[...] * (1.0 + sc_ref[...]) + sh_ref[...]
        h_hi, h_lo = _split_bf16(h2)
        sub(h2_ref, j)[...] = h_hi
        _store_packed(h2p_ref.at[pl.ds(j * TM_PROJ * ROW_TILES, TM_PROJ * ROW_TILES)], _pack_words(h2), TM_PROJ)
        a = _dot_nt(w_hl, h_hi)
        lg_ref[:, j * TM_PROJ:(j + 1) * TM_PROJ] = a[0:ne] + a[ne:] + _dot_nt(w_hi, h_lo)


def _outproj(om, od, x, w_out, gt, gpost, gpre, sc, sh, wr_t):
    s, d = x.shape
    tm = min(TM_OUT, s)
    row = lambda n: pl.BlockSpec((tm, n), lambda i: (i, 0))
    vec = _const_spec((1, d))
    return pl.pallas_call(
        _outproj_kernel,
        grid=(s // tm,),
        in_specs=[row(om.shape[1]), row(od.shape[1]), row(d), _const_spec(w_out.shape),
                  vec, vec, vec, vec, vec, _const_spec(wr_t.shape)],
        out_specs=[row(d), row(d),
                   pl.BlockSpec((tm * ROW_TILES, LANES), lambda i: (i, 0)),
                   pl.BlockSpec((N_EXPERTS, tm), lambda i: (0, i))],
        out_shape=[jax.ShapeDtypeStruct((s, d), F32),
                   jax.ShapeDtypeStruct((s, d), BF16),
                   jax.ShapeDtypeStruct((s * ROW_TILES, LANES), U32),
                   jax.ShapeDtypeStruct((N_EXPERTS, s), F32)],
        compiler_params=_cparams(("arbitrary",)),
        name="out_proj",
    )(om, od, x, w_out, gt, gpost, gpre, sc, sh, wr_t)


def _route_kernel(lg_ref, b_ref, eidx_ref, w_ref, rank_ref, cnt_ref, carry):
    step = pl.program_id(0)

    @pl.when(step == 0)
    def _():
        carry[...] = jnp.zeros(carry.shape, F32)

    tr = lg_ref.shape[1]
    ninf = jnp.float32(-jnp.inf)
    ie = lax.broadcasted_iota(I32, (GROUP_SIZE, tr), 0)
    rmax = lambda a: jnp.max(a, axis=0, keepdims=True)
    rmin = lambda a: jnp.min(a, axis=0, keepdims=True)
    rsum = lambda a: jnp.sum(a, axis=0, keepdims=True)

    sc, bi, gscore = [], [], []
    for g in range(N_GROUPS):
        lg = lg_ref[g * GROUP_SIZE:(g + 1) * GROUP_SIZE, :]
        s = jax.nn.sigmoid(lg)
        b = s + b_ref[g * GROUP_SIZE:(g + 1) * GROUP_SIZE, :]
        m1 = rmax(b)
        i1 = rmin(jnp.where(b == m1, ie, GROUP_SIZE))
        m2 = rmax(jnp.where(ie == i1, ninf, b))
        sc.append(s)
        bi.append(b)
        gscore.append(m1 + m2)

    gsel = [jnp.zeros((1, tr), I32) for _ in range(N_GROUPS)]
    for _ in range(TOPK_GROUPS):
        gm = functools.reduce(jnp.maximum, gscore)
        gi = functools.reduce(
            jnp.minimum, [jnp.where(gscore[g] == gm, g, N_GROUPS) for g in range(N_GROUPS)])
        for g in range(N_GROUPS):
            hit = gi == g
            gsel[g] = jnp.where(hit, 1, gsel[g])
            gscore[g] = jnp.where(hit, ninf, gscore[g])

    masked = [jnp.where(jnp.broadcast_to(gsel[g], (GROUP_SIZE, tr)) > 0, bi[g], ninf)
              for g in range(N_GROUPS)]
    sel = [jnp.zeros((GROUP_SIZE, tr), F32) for _ in range(N_GROUPS)]
    idxs, svals = [], []
    for _ in range(TOP_K):
        m = functools.reduce(jnp.maximum, [rmax(x) for x in masked])
        idx = functools.reduce(
            jnp.minimum,
            [rmin(jnp.where(masked[g] == m, ie + g * GROUP_SIZE, N_EXPERTS)) for g in range(N_GROUPS)])
        sv = jnp.zeros((1, tr), F32)
        for g in range(N_GROUPS):
            hit = (ie + g * GROUP_SIZE) == idx
            sv = sv + rsum(jnp.where(hit, sc[g], 0.0))
            masked[g] = jnp.where(hit, ninf, masked[g])
            sel[g] = jnp.where(hit, 1.0, sel[g])
        idxs.append(idx)
        svals.append(sv)
    tot = functools.reduce(lambda a, b: a + b, svals)

    sel2 = jnp.concatenate(sel, axis=0)
    before = (lax.broadcasted_iota(I32, (tr, tr), 0) < lax.broadcasted_iota(I32, (tr, tr), 1))
    rank2 = _dot(sel2.astype(BF16), before.astype(BF16)) + carry[:, 0:1]
    for k in range(TOP_K):
        rk = jnp.zeros((1, tr), F32)
        for g in range(N_GROUPS):
            hit = (ie + g * GROUP_SIZE) == idxs[k]
            rk = rk + rsum(jnp.where(hit, rank2[g * GROUP_SIZE:(g + 1) * GROUP_SIZE, :], 0.0))
        eidx_ref[k:k + 1, :] = idxs[k]
        w_ref[k:k + 1, :] = svals[k] / tot * ROUTED_SCALE
        rank_ref[k:k + 1, :] = rk.astype(I32)
    carry[...] = carry[...] + jnp.sum(sel2, axis=1, keepdims=True)
    cnt_ref[...] = carry[...]


def _route(lg_t, b_col):
    e, t = lg_t.shape
    tr = min(TR, t)
    tok = lambda: pl.BlockSpec((TOP_K, tr), lambda i: (0, i))
    return pl.pallas_call(
        _route_kernel,
        grid=(t // tr,),
        in_specs=[pl.BlockSpec((e, tr), lambda i: (0, i)), pl.BlockSpec((e, 1), lambda i: (0, 0))],
        out_specs=[tok(), tok(), tok(), pl.BlockSpec((e, LANES), lambda i: (0, 0))],
        out_shape=[jax.ShapeDtypeStruct((TOP_K, t), I32),
                   jax.ShapeDtypeStruct((TOP_K, t), F32),
                   jax.ShapeDtypeStruct((TOP_K, t), I32),
                   jax.ShapeDtypeStruct((e, LANES), F32)],
        scratch_shapes=[pltpu.VMEM((e, LANES), F32)],
        compiler_params=_cparams(("arbitrary",)),
        name="route",
    )(lg_t, b_col)


def _pos_kernel(pstart_ref, eidx_ref, rank_ref, pos_ref):
    e = eidx_ref[...]
    pos = rank_ref[...]
    for x in range(N_EXPERTS):
        pos = pos + jnp.where(e == x, pstart_ref[x], 0)
    pos_ref[...] = pos


def _positions(pstart, eidx, rank):
    k, t = eidx.shape
    tr = min(TR, t)
    gs = pltpu.PrefetchScalarGridSpec(
        num_scalar_prefetch=1,
        grid=(t // tr,),
        in_specs=[pl.BlockSpec((k, tr), lambda i, p: (0, i)), pl.BlockSpec((k, tr), lambda i, p: (0, i))],
        out_specs=pl.BlockSpec((k, tr), lambda i, p: (0, i)),
    )
    return pl.pallas_call(
        _pos_kernel, grid_spec=gs,
        out_shape=jax.ShapeDtypeStruct((k, t), I32),
        compiler_params=_cparams(("arbitrary",)),
        name="positions",
    )(pstart, eidx, rank)


def _row_copy(src, src_row, dst, dst_row, sem):
    return pltpu.make_async_copy(src.at[pl.ds(src_row * ROW_TILES, ROW_TILES)],
                                 dst.at[pl.ds(dst_row * ROW_TILES, ROW_TILES)], sem)


FILL_CHUNKS = (128, 64, 32, 16, 8, 4, 2, 1)


def _zero_fill(fs_ref, fl_ref, tail_ref, zbuf, xs_ref, sem, wait):
    def chunk(row, n):
        return pltpu.make_async_copy(zbuf.at[pl.ds(0, n * ROW_TILES)],
                                     xs_ref.at[pl.ds(row * ROW_TILES, n * ROW_TILES)], sem)

    def go(cp):
        if wait:
            cp.wait()
        else:
            cp.start()

    def per_expert(e, c):
        pad = fl_ref[e]
        row = fs_ref[e]
        for n in FILL_CHUNKS:
            @pl.when((pad & n) != 0)
            def _(row=row, n=n):
                go(chunk(row, n))
            row = row + (pad & n)
        return c

    lax.fori_loop(0, N_EXPERTS, per_expert, 0)

    def per_tail(b, c):
        for part in range(EXP_BLK // FILL_CHUNKS[0]):
            go(chunk(b * EXP_BLK + part * FILL_CHUNKS[0], FILL_CHUNKS[0]))
        return c

    lax.fori_loop(tail_ref[0], tail_ref[1], per_tail, 0)


def _dispatch_kernel(fs_ref, fl_ref, tail_ref, pos_ref, h_ref, xs_ref, zbuf, sem, fill_sem):
    td = pos_ref.shape[1]
    step = pl.program_id(0)

    @pl.when(step == 0)
    def _():
        zbuf[...] = jnp.zeros(zbuf.shape, zbuf.dtype)
        _zero_fill(fs_ref, fl_ref, tail_ref, zbuf, xs_ref, fill_sem, wait=False)

    def issue(t, c):
        for k in range(TOP_K):
            _row_copy(h_ref, t, xs_ref, pos_ref[k, t], sem).start(priority=k % 2)
        return c

    lax.fori_loop(0, td, issue, 0, unroll=4)
    for k in range(TOP_K):
        pltpu.make_async_copy(h_ref, xs_ref.at[pl.ds(0, td * ROW_TILES)], sem).wait()

    @pl.when(step == pl.num_programs(0) - 1)
    def _():
        _zero_fill(fs_ref, fl_ref, tail_ref, zbuf, xs_ref, fill_sem, wait=True)


def _dispatch(fill_start, fill_len, tail, pos, h2p, m_pad):
    k, t = pos.shape
    td = min(TD, t)
    gs = pltpu.PrefetchScalarGridSpec(
        num_scalar_prefetch=3,
        grid=(t // td,),
        in_specs=[pl.BlockSpec((k, td), lambda i, *_: (0, i), memory_space=pltpu.SMEM),
                  pl.BlockSpec((td * ROW_TILES, LANES), lambda i, *_: (i, 0))],
        out_specs=pl.BlockSpec(memory_space=pl.ANY),
        scratch_shapes=[pltpu.VMEM((FILL_CHUNKS[0] * ROW_TILES, LANES), U32),
                        pltpu.SemaphoreType.DMA, pltpu.SemaphoreType.DMA],
    )
    return pl.pallas_call(
        _dispatch_kernel,
        grid_spec=gs,
        out_shape=jax.ShapeDtypeStruct((m_pad * ROW_TILES, LANES), U32),
        compiler_params=_cparams(("arbitrary",)),
        name="dispatch",
    )(fill_start, fill_len, tail, pos, h2p)


def _weight_copies(w_hbm, w_buf, sem, e, s):
    return [pltpu.make_async_copy(h.at[e], b.at[s], sem.at[s]) for h, b in zip(w_hbm, w_buf)]


def _experts_kernel(be_ref, nu_ref, fresh_ref, nxt_ref, slot_ref, x_ref, wg_hbm, wu_hbm, wd_hbm, y_ref,
                    wg_buf, wu_buf, wd_buf, wg_s, wu_s, wd_s, sem):
    step = pl.program_id(0)
    b0 = step * BLKS_PER_STEP
    blk_rows = x_ref.shape[0] // BLKS_PER_STEP
    w_hbm = (wg_hbm, wu_hbm, wd_hbm)
    w_buf = (wg_buf, wu_buf, wd_buf)

    def maybe_weights(b):
        @pl.when(fresh_ref[b] == 1)
        def _():
            s = slot_ref[b]

            @pl.when(b == 0)
            def _():
                for c in _weight_copies(w_hbm, w_buf, sem, be_ref[0], 0):
                    c.start()

            for c in _weight_copies(w_hbm, w_buf, sem, be_ref[b], s):
                c.wait()

            @pl.when(nxt_ref[b] >= 0)
            def _():
                for c in _weight_copies(w_hbm, w_buf, sem, nxt_ref[b], 1 - s):
                    c.start()

            wg_s[...] = wg_buf[s].astype(BF16)
            wu_s[...] = wu_buf[s].astype(BF16)
            wd_s[...] = wd_buf[s].astype(BF16)

    def swiglu(first, n):
        x_blk = x_ref.at[pl.ds(first * blk_rows, n * blk_rows)]
        y_blk = y_ref.at[pl.ds(first * blk_rows, n * blk_rows)]
        rows = n * blk_rows // ROW_TILES
        xa, xb = _unpack_words(_load_packed(x_blk, rows))
        xa = xa.astype(BF16)
        xb = xb.astype(BF16)
        g = _dot(xa, wg_s[0:HALF, :]) + _dot(xb, wg_s[HALF:, :])
        u = _dot(xa, wu_s[0:HALF, :]) + _dot(xb, wu_s[HALF:, :])
        a = (g * jax.nn.sigmoid(g) * u).astype(BF16)
        _store_packed(y_blk, _pack_words(_dot(a, wd_s[...])), rows)

    def zero(first, n):
        y_ref[pl.ds(first * blk_rows, n * blk_rows), :] = jnp.zeros((n * blk_rows, LANES), y_ref.dtype)

    n_used = jnp.clip(nu_ref[0] - b0, 0, BLKS_PER_STEP)
    same = jnp.logical_and(n_used == BLKS_PER_STEP, be_ref[b0] == be_ref[b0 + 1])

    @pl.when(same)
    def _():
        maybe_weights(b0)
        swiglu(0, BLKS_PER_STEP)

    @pl.when(jnp.logical_and(jnp.logical_not(same), n_used > 0))
    def _():
        maybe_weights(b0)
        swiglu(0, 1)

        @pl.when(n_used > 1)
        def _():
            maybe_weights(b0 + 1)
            swiglu(1, 1)

        @pl.when(n_used == 1)
        def _():
            zero(1, 1)

    @pl.when(n_used == 0)
    def _():
        zero(0, BLKS_PER_STEP)


def _experts(blk_e, nused, fresh, nxt_e, slot, xs, w_gate, w_up, w_down, blk):
    n_blk = blk_e.shape[0]
    assert n_blk % BLKS_PER_STEP == 0 and BLKS_PER_STEP == 2
    ne, d, de = w_gate.shape
    rows = BLKS_PER_STEP * blk * ROW_TILES
    xmap = lambda i, be, nu, fr, nx, sl: (jnp.minimum(i, (nu[0] - 1) // BLKS_PER_STEP), 0)
    hbm = pl.BlockSpec(memory_space=pl.ANY)
    gs = pltpu.PrefetchScalarGridSpec(
        num_scalar_prefetch=5,
        grid=(n_blk // BLKS_PER_STEP,),
        in_specs=[pl.BlockSpec((rows, LANES), xmap), hbm, hbm, hbm],
        out_specs=pl.BlockSpec((rows, LANES), lambda i, *_: (i, 0)),
        scratch_shapes=[pltpu.VMEM((2, d, de), F32), pltpu.VMEM((2, d, de), F32), pltpu.VMEM((2, de, d), F32),
                        pltpu.VMEM((d, de), BF16), pltpu.VMEM((d, de), BF16), pltpu.VMEM((de, d), BF16),
                        pltpu.SemaphoreType.DMA((2,))],
    )
    return pl.pallas_call(
        _experts_kernel, grid_spec=gs,
        out_shape=jax.ShapeDtypeStruct(xs.shape, U32),
        compiler_params=_cparams(("arbitrary",)),
        name="experts",
    )(blk_e, nused, fresh, nxt_e, slot, xs, w_gate, w_up, w_down)


def _combine_kernel(pos_ref, wt_ref, ys_ref, h_ref, x1_ref, wsg_ref, wsu_ref, wsd_ref,
                    gt_ref, gpost_ref, o_ref, gbuf, sem):
    td = pos_ref.shape[1]

    def issue(t, c):
        for k in range(TOP_K):
            _row_copy(ys_ref, pos_ref[k, t], gbuf.at[k], t, sem).start(priority=k % 2)
        return c

    lax.fori_loop(0, td, issue, 0, unroll=4)

    hb = h_ref[...]
    g = _dot(hb, wsg_ref[...])
    u = _dot(hb, wsu_ref[...])
    y = _dot((g * jax.nn.sigmoid(g) * u).astype(BF16), wsd_ref[...])

    wpad = jnp.concatenate([wt_ref[...], jnp.zeros((LANES - TOP_K, td), F32)], axis=0)
    wcol = jnp.concatenate([wpad[:, j * LANES:(j + 1) * LANES].T for j in range(td // LANES)], axis=0)

    for k in range(TOP_K):
        pltpu.make_async_copy(ys_ref.at[pl.ds(0, td * ROW_TILES)], gbuf.at[k], sem).wait()
    lo = jnp.zeros((td, HALF), F32)
    hi = jnp.zeros((td, HALF), F32)
    for k in range(TOP_K):
        a, b = _unpack_words(_load_packed(gbuf.at[k], td))
        wk = wcol[:, k:k + 1]
        lo = lo + wk * a
        hi = hi + wk * b
    y = y + jnp.concatenate([lo, hi], axis=1)
    o_ref[...] = x1_ref[...] + gt_ref[...] * (_rms(y) * gpost_ref[...])


def _combine(pos, wts, ys, h2, x1, wsg, wsu, wsd, gt, gpost):
    k, t = pos.shape
    d = x1.shape[1]
    td = min(TD, t)
    assert td % LANES == 0
    row = lambda n: pl.BlockSpec((td, n), lambda i: (i, 0))
    vec = _const_spec((1, d))
    return pl.pallas_call(
        _combine_kernel,
        grid=(t // td,),
        in_specs=[pl.BlockSpec((k, td), lambda i: (0, i), memory_space=pltpu.SMEM),
                  pl.BlockSpec((k, td), lambda i: (0, i)),
                  pl.BlockSpec(memory_space=pl.ANY),
                  row(d), row(d), _const_spec(wsg.shape), _const_spec(wsu.shape), _const_spec(wsd.shape),
                  vec, vec],
        out_specs=row(d),
        out_shape=jax.ShapeDtypeStruct((t, d), F32),
        scratch_shapes=[pltpu.VMEM((k, td * ROW_TILES, LANES), U32), pltpu.SemaphoreType.DMA],
        compiler_params=_cparams(("arbitrary",)),
        name="combine",
    )(pos, wts, ys, h2, x1, wsg, wsu, wsd, gt, gpost)


def _rope_tables(seq, dim):
    pos = jnp.arange(seq, dtype=F32)
    inv = ROPE_THETA ** (-jnp.arange(0, dim, 2, dtype=F32) / dim)
    ang = pos[:, None] * inv[None, :]
    return jnp.cos(ang), jnp.sin(ang)


def _lane_tables(seq, dim, period):
    cos, sin = _rope_tables(seq, dim)
    half = dim // 2
    cos = jnp.tile(cos, (1, LANES // half))
    sin = jnp.tile(sin, (1, LANES // half))
    r = jnp.arange(LANES) % period
    c = jnp.where(r < dim, cos, 1.0)
    sa = jnp.where(r < half, -sin, 0.0)
    sb = jnp.where((r >= half) & (r < dim), sin, 0.0)
    return c, sa, sb


def _moe(h2, h2p, x1, lg_t, b_router, w_gate, w_up, w_down, wsg, wsu, wsd, gt_f, g_post_ffn):
    t = h2.shape[0]
    blk = EXP_BLK
    m_pad = t * TOP_K + N_EXPERTS * blk
    eidx, wts, rank, cnt = _route(lg_t, b_router.reshape(N_EXPERTS, 1))
    counts = cnt[:, 0].astype(I32)
    padded = (counts + blk - 1) // blk * blk
    pend = jnp.cumsum(padded)
    pstart = pend - padded
    n_blk = m_pad // blk
    blk_start = jnp.arange(n_blk, dtype=I32) * blk
    blk_e = jnp.minimum(jnp.sum(pend[None, :] <= blk_start[:, None], axis=1), N_EXPERTS - 1).astype(I32)
    nused = (pend[-1:] // blk).astype(I32)
    pos = _positions(pstart.astype(I32), eidx, rank)
    tail = jnp.concatenate([nused, jnp.full((1,), n_blk, I32)])
    xs = _dispatch((pstart + counts).astype(I32), (padded - counts).astype(I32), tail, pos, h2p, m_pad)
    idx = jnp.arange(n_blk, dtype=I32)
    fresh = (idx < nused[0]) & (blk_e != jnp.concatenate([jnp.full((1,), -1, I32), blk_e[:-1]]))
    slot = ((jnp.cumsum(fresh.astype(I32)) - 1) % 2).astype(I32)
    first_at_or_after = lax.cummin(jnp.where(fresh, idx, n_blk)[::-1])[::-1]
    nxt_idx = jnp.concatenate([first_at_or_after[1:], jnp.full((1,), n_blk, I32)])
    nxt_e = jnp.where(nxt_idx < n_blk, blk_e[jnp.minimum(nxt_idx, n_blk - 1)], -1).astype(I32)
    ys = _experts(blk_e, nused, fresh.astype(I32), nxt_e, slot, xs, w_gate, w_up, w_down, blk)
    return _combine(pos, wts, ys, h2, x1, wsg, wsu, wsd, gt_f, g_post_ffn)


def _layer(x, c, w_ada, b_ada, g_pre_mix, w_in, g_q_lat, w_uq, g_kv_lat, w_ukv,
           lq1, lk1, lq2, lk2, g_diff_sub, w_out, g_post_mix, g_pre_ffn, w_router, b_router,
           w_gate, w_up, w_down, ws_gate, ws_up, ws_down, g_post_ffn):
    s, d = x.shape
    row = lambda a: a.reshape(1, -1)

    mod = _ada(c, w_ada, row(b_ada))
    sh_a, sc_a, gt_a, sh_f, sc_f, gt_f = [mod[:, i * d:(i + 1) * d] for i in range(6)]

    o1 = MLA_Q_RANK + MLA_KV_RANK
    o2 = o1 + MLA_ROPE
    o3 = o2 + 2 * DIFF_W
    w_lat, w_qk, w_kpe, wdv_t = _split_w_in(w_in.T, (o1, o2, o3))
    wq = w_uq.reshape(MLA_Q_RANK, MLA_HEADS, MLA_NOPE + MLA_ROPE)
    wq = jnp.pad(wq, ((0, 0), (0, 0), (0, MLA_QK_PAD - MLA_NOPE - MLA_ROPE)))
    wuq_r = wq.reshape(MLA_Q_RANK, MLA_HEADS * MLA_QK_PAD).astype(BF16)
    wkv = w_ukv.reshape(MLA_KV_RANK, MLA_HEADS, MLA_NOPE + MLA_V)
    wuk_r = wkv[:, :, :MLA_NOPE].reshape(MLA_KV_RANK, -1).astype(BF16)
    wmv_t = wkv[:, :, MLA_NOPE:].reshape(MLA_KV_RANK, -1).T.astype(BF16)
    dq, dk, dvt, qc, kc, vt = _inproj(
        x, row(g_pre_mix), sc_a, sh_a, w_lat, w_qk, w_kpe, wdv_t,
        _lane_tables(s, DIFF_ROT, DIFF_HEAD_DIM), row(g_q_lat), row(g_kv_lat), wuq_r, wuk_r, wmv_t,
        _lane_tables(s, MLA_ROPE, MLA_ROPE))

    o_mla = _mla_attn(qc, kc, vt)
    o_diff = _diff_attn(dq, dk, dvt, row(lq1), row(lk1), row(lq2), row(lk2), g_diff_sub.reshape(-1, 1))

    x1, h2, h2p, lg_t = _outproj(o_mla, o_diff, x, w_out.astype(BF16), gt_a, row(g_post_mix),
                                 row(g_pre_ffn), sc_f, sh_f, w_router.T)
    return _moe(h2, h2p, x1, lg_t, b_router, w_gate, w_up, w_down,
                ws_gate.astype(BF16), ws_up.astype(BF16), ws_down.astype(BF16), gt_f, row(g_post_ffn))


def kernel(x, c, w_ada, b_ada, g_pre_mix, w_in, g_q_lat, w_uq, g_kv_lat, w_ukv, lambda_q1, lambda_k1, lambda_q2, lambda_k2, g_diff_sub, w_out, g_post_mix, g_pre_ffn, w_router, b_router, w_gate, w_up, w_down, ws_gate, ws_up, ws_down, g_post_ffn):
    assert x.shape[0] == 1 and w_ada.shape[0] == 1
    out = _layer(x[0], c, w_ada[0], b_ada[0], g_pre_mix[0], w_in[0], g_q_lat[0], w_uq[0],
                 g_kv_lat[0], w_ukv[0], lambda_q1[0], lambda_k1[0], lambda_q2[0], lambda_k2[0],
                 g_diff_sub[0], w_out[0], g_post_mix[0], g_pre_ffn[0], w_router[0], b_router[0],
                 w_gate[0], w_up[0], w_down[0], ws_gate[0], ws_up[0], ws_down[0], g_post_ffn[0])
    return out[None]
```

```python
import functools
import math

import jax
import jax.numpy as jnp
import numpy as np
from jax import lax
from jax.experimental import pallas as pl
from jax.experimental.pallas import tpu as pltpu

F32 = jnp.float32
BF16 = jnp.bfloat16
U32 = jnp.uint32
I32 = jnp.int32

D_MODEL = 2048
CHUNK = 64
ROPE_THETA = 500000.0
EPS = 1e-6
LOG2E = 1.4426950408889634

MLA_HEADS = 8
MLA_Q_RANK = 768
MLA_KV_RANK = 512
MLA_NOPE = 128
MLA_ROPE = 64
MLA_V = 128
MLA_QK_PAD = 256

DIFF_HEADS = 8
DIFF_HEAD_DIM = 64
DIFF_ROT = DIFF_HEAD_DIM // 4
DIFF_W = DIFF_HEADS * 2 * DIFF_HEAD_DIM

N_EXPERTS = 64
TOP_K = 8
N_GROUPS = 8
GROUP_SIZE = N_EXPERTS // N_GROUPS
TOPK_GROUPS = 4
D_EXPERT = 512
ROUTED_SCALE = 2.5
LAMBDA_INIT = 0.8 - 0.6 * math.exp(-0.3 * 0)

LANES = 128
SUBLANES = 8
HALF = D_MODEL // 2
ROW_TILES = HALF // LANES

TM_PROJ = 256
TM_OUT = 2 * TM_PROJ
MLA_BQ = 4096
DIFF_BQ = 4096
MLA_BK = 2048
DIFF_BK = 1024
UNIT_W = 512
V_EXT = MLA_V + 16
EXP_BLK = 256
BLKS_PER_STEP = 2
TD = 256
TR = 512
TN_ADA = 1024

VMEM_LIMIT = 56 * 1024 * 1024

assert ROW_TILES == SUBLANES


def _cparams(sem, vmem=VMEM_LIMIT, flags=None):
    return pltpu.CompilerParams(dimension_semantics=sem, vmem_limit_bytes=vmem, flags=flags)


def _dot(a, b):
    return jnp.dot(a, b, preferred_element_type=F32)


def _dot_nt(a, b):
    return lax.dot_general(a, b, (((1,), (1,)), ((), ())), preferred_element_type=F32)


def _rms(x):
    return x * lax.rsqrt(jnp.mean(x * x, axis=-1, keepdims=True) + EPS)


def _split_bf16(x):
    hi = x.astype(BF16)
    lo = (x - hi.astype(F32)).astype(BF16)
    return hi, lo


def _const_spec(shape):
    nd = len(shape)
    return pl.BlockSpec(shape, lambda *a: (0,) * nd, pipeline_mode=pl.Buffered(1))


def _ada_kernel(c_ref, w_ref, b_ref, o_ref):
    c = c_ref[...]
    a = c * jax.nn.sigmoid(c)
    a8 = jnp.broadcast_to(a, (SUBLANES, a.shape[1]))
    a_hi, a_lo = _split_bf16(a8)
    w_hi, w_lo = _split_bf16(w_ref[...])
    r = _dot(a_hi, w_hi) + _dot(a_lo, w_hi) + _dot(a_hi, w_lo)
    o_ref[...] = r[0:1] + b_ref[...]


def _ada(c, w, b):
    d, n = w.shape
    return pl.pallas_call(
        _ada_kernel,
        grid=(n // TN_ADA,),
        in_specs=[pl.BlockSpec((1, d), lambda j: (0, 0)),
                  pl.BlockSpec((d, TN_ADA), lambda j: (0, j)),
                  pl.BlockSpec((1, TN_ADA), lambda j: (0, j))],
        out_specs=pl.BlockSpec((1, TN_ADA), lambda j: (0, j)),
        out_shape=jax.ShapeDtypeStruct((1, n), F32),
        compiler_params=_cparams(("arbitrary",)),
        name="ada_mod",
    )(c, w, b)


def _tile_lanes(t, reps):
    return jnp.concatenate([t] * reps, axis=1)


def _rope_lanes(x, c, sa, sb, half):
    n = x.shape[1]
    return x * c + pltpu.roll(x, n - half, 1) * sa + pltpu.roll(x, half, 1) * sb


def _split_w_in_kernel(offs, wt_ref, lat_ref, qk_ref, kpe_ref, dvt_ref):
    o1, o2, o3 = offs
    wt = wt_ref[...]
    lat_ref[...] = wt[:o1, :].T.astype(BF16)
    qk_ref[...] = wt[o2:o3, :].T.astype(BF16)
    kpe = wt[o1:o2, :].T
    kpe_ref[...] = jnp.concatenate([kpe, jnp.zeros_like(kpe)], axis=1).astype(BF16)
    dvt_ref[...] = wt[o3:, :].astype(BF16)


def _split_w_in(w_in_t, offs):
    n, d = w_in_t.shape
    o1, o2, o3 = offs
    tk = TM_PROJ
    return pl.pallas_call(
        functools.partial(_split_w_in_kernel, offs),
        grid=(d // tk,),
        in_specs=[pl.BlockSpec((n, tk), lambda i: (0, i))],
        out_specs=[pl.BlockSpec((tk, o1), lambda i: (i, 0)), pl.BlockSpec((tk, o3 - o2), lambda i: (i, 0)),
                   pl.BlockSpec((tk, LANES), lambda i: (i, 0)), pl.BlockSpec((n - o3, tk), lambda i: (0, i))],
        out_shape=[jax.ShapeDtypeStruct((d, o1), BF16), jax.ShapeDtypeStruct((d, o3 - o2), BF16),
                   jax.ShapeDtypeStruct((d, LANES), BF16), jax.ShapeDtypeStruct((n - o3, d), BF16)],
        compiler_params=_cparams(("arbitrary",)),
        name="split_w_in",
    )(w_in_t)


def _mla_heads(lat, kpe_raw, gq_ref, gkv_ref, wuq_ref, wuk_ref, wvt_ref, c, sa, sb, q_ref, k_ref, vt_ref):
    qn = (_rms(lat[:, :MLA_Q_RANK]) * gq_ref[...]).astype(BF16)
    kvn = (_rms(lat[:, MLA_Q_RANK:]) * gkv_ref[...]).astype(BF16)
    q = _dot(qn, wuq_ref[...])
    kn = _dot(kvn, wuk_ref[...])
    half = MLA_ROPE // 2
    qs = (MLA_NOPE + MLA_ROPE) ** -0.5 * LOG2E
    kpe = _rope_lanes(kpe_raw, c, sa, sb, half).astype(BF16)
    for h in range(MLA_HEADS):
        o = h * MLA_QK_PAD
        q_ref[:, o:o + LANES] = (q[:, o:o + LANES] * qs).astype(BF16)
        q_ref[:, o + LANES:o + 2 * LANES] = (
            _rope_lanes(q[:, o + LANES:o + 2 * LANES], c, sa, sb, half) * qs).astype(BF16)
        k_ref[:, o:o + LANES] = kn[:, h * MLA_NOPE:(h + 1) * MLA_NOPE].astype(BF16)
        k_ref[:, o + LANES:o + 2 * LANES] = kpe
    _store_vt_ext(vt_ref, _dot_nt(wvt_ref[...], kvn), MLA_HEADS)


def _inproj_kernel(x_ref, g_ref, sc_ref, sh_ref, wlat_ref, wqk_ref, wkpe_ref, wdvt_ref,
                   dc_ref, dsa_ref, dsb_ref, gq_ref, gkv_ref, wuq_ref, wuk_ref, wmvt_ref,
                   mc_ref, msa_ref, msb_ref,
                   dq_ref, dk_ref, dvt_ref, q_ref, k_ref, vt_ref):
    x = x_ref[...]
    h = _rms(x) * g_ref[...] * (1.0 + sc_ref[...]) + sh_ref[...]
    hb = h.astype(BF16)
    reps = DIFF_W // LANES
    c = _tile_lanes(dc_ref[...], reps)
    sa = _tile_lanes(dsa_ref[...], reps)
    sb = _tile_lanes(dsb_ref[...], reps)
    half = DIFF_ROT // 2
    q = _dot(hb, wqk_ref[:, 0:DIFF_W])
    dq_ref[...] = (_rope_lanes(q, c, sa, sb, half) * (DIFF_HEAD_DIM ** -0.5 * LOG2E)).astype(BF16)
    k = _dot(hb, wqk_ref[:, DIFF_W:])
    dk_ref[...] = _rope_lanes(k, c, sa, sb, half).astype(BF16)
    _store_vt_ext(dvt_ref, _dot_nt(wdvt_ref[...], hb), DIFF_HEADS)
    _mla_heads(_dot(hb, wlat_ref[...]), _dot(hb, wkpe_ref[...]), gq_ref, gkv_ref, wuq_ref, wuk_ref,
               wmvt_ref, mc_ref[...], msa_ref[...], msb_ref[...], q_ref, k_ref, vt_ref)


def _inproj(x, g, sc, sh, w_lat, w_qk, w_kpe, wdv_t, dtabs, gq, gkv, wuq_r, wuk_r, wmv_t, mtabs):
    s, d = x.shape
    tm = min(TM_PROJ, s)
    row = lambda n: pl.BlockSpec((tm, n), lambda i: (i, 0))
    col = lambda n: pl.BlockSpec((n, tm), lambda i: (0, i))
    const = lambda a: _const_spec(a.shape)
    hq = MLA_HEADS * MLA_QK_PAD
    return pl.pallas_call(
        _inproj_kernel,
        grid=(s // tm,),
        in_specs=[row(d), const(g), const(sc), const(sh), const(w_lat), const(w_qk), const(w_kpe),
                  const(wdv_t), row(LANES), row(LANES), row(LANES),
                  const(gq), const(gkv), const(wuq_r), const(wuk_r), const(wmv_t),
                  row(LANES), row(LANES), row(LANES)],
        out_specs=[row(DIFF_W), row(DIFF_W), col(DIFF_HEADS * V_EXT), row(hq), row(hq), col(MLA_HEADS * V_EXT)],
        out_shape=[jax.ShapeDtypeStruct((s, DIFF_W), BF16),
                   jax.ShapeDtypeStruct((s, DIFF_W), BF16),
                   jax.ShapeDtypeStruct((DIFF_HEADS * V_EXT, s), BF16),
                   jax.ShapeDtypeStruct((s, hq), BF16),
                   jax.ShapeDtypeStruct((s, hq), BF16),
                   jax.ShapeDtypeStruct((MLA_HEADS * V_EXT, s), BF16)],
        compiler_params=_cparams(("arbitrary",)),
        name="in_proj",
    )(x, g, sc, sh, w_lat, w_qk, w_kpe, wdv_t, *dtabs, gq, gkv, wuq_r, wuk_r, wmv_t, *mtabs)


def _pair_tables(nq, kv_per_q):
    n = [kv_per_q * (i + 1) for i in range(nq)]
    qi = np.concatenate([np.full(c, i) for i, c in enumerate(n)]).astype(np.int32)
    kj = np.concatenate([np.arange(c) for c in n]).astype(np.int32)
    return jnp.asarray(qi), jnp.asarray(kj)


def _ones_rows(n):
    rows = V_EXT - MLA_V
    return (lax.broadcasted_iota(I32, (rows, n), 0) == 0).astype(BF16)


def _store_vt_ext(vt_ref, vt, heads):
    ones = _ones_rows(vt.shape[1])
    for h in range(heads):
        vt_ref[h * V_EXT:h * V_EXT + MLA_V, :] = vt[h * MLA_V:(h + 1) * MLA_V, :].astype(BF16)
        vt_ref[h * V_EXT + MLA_V:(h + 1) * V_EXT, :] = ones


def _attn_step(q_refs, k_ref, vt_ref, s_bufs, m_refs, acc_refs, rel):
    w = UNIT_W
    nk = k_ref.shape[0]
    per_kv = nk // w
    specs = []
    for u in range(q_refs[0].shape[0] // w):
        d = None if rel is None else u - rel * per_kv
        if d is None or d >= per_kv:
            specs.append((u, nk, None))
        elif d >= 0:
            specs.append((u, (d + 1) * w, d * w))
    units = [(si, sp) for sp in specs for si in range(len(q_refs))]

    def scores(n):
        si, (u, rows, off) = units[n]
        s = _dot_nt(k_ref[0:rows, :], q_refs[si][u * w:(u + 1) * w, :])
        if off is not None:
            kc = lax.broadcasted_iota(I32, (rows, w), 0) // CHUNK
            qc = (lax.broadcasted_iota(I32, (rows, w), 1) + off) // CHUNK
            s = jnp.where(kc <= qc, s, -jnp.inf)
        s_bufs[n % 2][0:rows, :] = s
        return jnp.max(s, axis=0, keepdims=True)

    cmax_next = scores(0)
    for n, (si, (u, rows, _)) in enumerate(units):
        cmax = cmax_next
        if n + 1 < len(units):
            cmax_next = scores(n + 1)
        m_ref, acc_ref = m_refs[si][u], acc_refs[si][u]
        m_prev = m_ref[...]
        m_new = jnp.maximum(m_prev, cmax)
        m_ref[...] = m_new
        acc_ref[...] = jnp.exp2(m_prev - m_new) * acc_ref[...]
        p = jnp.exp2(s_bufs[n % 2][0:rows, :] - m_new).astype(BF16)
        acc_ref[...] += _dot(vt_ref[:, 0:rows], p)


def _init_stats(ms, accs):
    for m, a in zip(ms, accs):
        m[...] = jnp.full(m.shape, -jnp.inf, F32)
        a[...] = jnp.zeros(a.shape, F32)


def _normalized(acc_ref):
    acc = acc_ref[...]
    return acc[0:MLA_V, :] / acc[MLA_V:MLA_V + 1, :]


def _attn_phases(qi_ref, kj_ref, kv_per_q, init, step, finish):
    t = pl.program_id(1)
    rel = kj_ref[t] - kv_per_q * qi_ref[t]

    @pl.when(kj_ref[t] == 0)
    def _():
        init()

    @pl.when(rel < 0)
    def _():
        step(None)

    for r in range(kv_per_q):
        @pl.when(rel == r)
        def _(r=r):
            step(r)
            if r == kv_per_q - 1:
                finish()


def _unit_scratch(n_streams, bq, bk):
    nu = bq // UNIT_W
    return ([pltpu.VMEM((bk, UNIT_W), F32)] * 2
            + [pltpu.VMEM((1, UNIT_W), F32)] * (nu * n_streams)
            + [pltpu.VMEM((V_EXT, UNIT_W), F32)] * (nu * n_streams))


def _split_units(scr, n_streams):
    nu = len(scr) // (2 * n_streams)
    ms = [list(scr[s * nu:(s + 1) * nu]) for s in range(n_streams)]
    accs = [list(scr[(n_streams + s) * nu:(n_streams + s + 1) * nu]) for s in range(n_streams)]
    return ms, accs


def _mla_attn_kernel(qi_ref, kj_ref, q_ref, k_ref, vt_ref, o_ref, *scr):
    s_bufs, (ms, accs) = scr[:2], _split_units(scr[2:], 1)

    def finish():
        for u, a in enumerate(accs[0]):
            o_ref[u * UNIT_W:(u + 1) * UNIT_W, :] = _normalized(a).T.astype(o_ref.dtype)

    _attn_phases(qi_ref, kj_ref, q_ref.shape[0] // k_ref.shape[0],
                 lambda: _init_stats(ms[0], accs[0]),
                 lambda rel: _attn_step([q_ref], k_ref, vt_ref, s_bufs, ms, accs, rel),
                 finish)


def _mla_attn(qc, kc, vt):
    s = qc.shape[0]
    bq, bk = min(MLA_BQ, s), MLA_BK
    assert s % bq == 0
    qi, kj = _pair_tables(s // bq, bq // bk)
    gs = pltpu.PrefetchScalarGridSpec(
        num_scalar_prefetch=2,
        grid=(MLA_HEADS, qi.shape[0]),
        in_specs=[pl.BlockSpec((bq, MLA_QK_PAD), lambda h, t, qi, kj: (qi[t], h)),
                  pl.BlockSpec((bk, MLA_QK_PAD), lambda h, t, qi, kj: (kj[t], h)),
                  pl.BlockSpec((V_EXT, bk), lambda h, t, qi, kj: (h, kj[t]))],
        out_specs=pl.BlockSpec((bq, MLA_V), lambda h, t, qi, kj: (qi[t], h)),
        scratch_shapes=_unit_scratch(1, bq, bk),
    )
    return pl.pallas_call(
        _mla_attn_kernel,
        grid_spec=gs,
        out_shape=jax.ShapeDtypeStruct((s, MLA_HEADS * MLA_V), BF16),
        compiler_params=_cparams(("arbitrary", "arbitrary")),
        name="mla_attn",
    )(qi, kj, qc, kc, vt)


def _diff_attn_kernel(qi_ref, kj_ref, q_ref, k_ref, vt_ref, lq1_ref, lk1_ref, lq2_ref, lk2_ref,
                      g_ref, o_ref, q0_scr, q1_scr, *scr):
    s_bufs, (ms, accs) = scr[:2], _split_units(scr[2:], 2)

    def init():
        q = q_ref[...]
        lane = lax.broadcasted_iota(I32, q.shape, 1)
        zero = jnp.zeros_like(q)
        q0_scr[...] = jnp.where(lane < DIFF_HEAD_DIM, q, zero)
        q1_scr[...] = jnp.where(lane >= DIFF_HEAD_DIM, q, zero)
        _init_stats(ms[0] + ms[1], accs[0] + accs[1])

    def finish():
        lam = (jnp.exp(jnp.sum(lq1_ref[...] * lk1_ref[...], axis=1, keepdims=True))
               - jnp.exp(jnp.sum(lq2_ref[...] * lk2_ref[...], axis=1, keepdims=True))
               + LAMBDA_INIT)
        for u, (a0, a1) in enumerate(zip(accs[0], accs[1])):
            o = _normalized(a0) - lam * _normalized(a1)
            o = o * lax.rsqrt(jnp.mean(o * o, axis=0, keepdims=True) + EPS)
            o_ref[u * UNIT_W:(u + 1) * UNIT_W, :] = (
                o * g_ref[...] * (1.0 - LAMBDA_INIT)).T.astype(o_ref.dtype)

    _attn_phases(qi_ref, kj_ref, q_ref.shape[0] // k_ref.shape[0], init,
                 lambda rel: _attn_step([q0_scr, q1_scr], k_ref, vt_ref, s_bufs, ms, accs, rel),
                 finish)


def _diff_attn(dq, dk, dvt, lq1, lk1, lq2, lk2, g_col):
    s = dq.shape[0]
    bq, bk = min(DIFF_BQ, s), DIFF_BK
    assert s % bq == 0
    qi, kj = _pair_tables(s // bq, bq // bk)
    hw = 2 * DIFF_HEAD_DIM
    small = lambda a: pl.BlockSpec(a.shape, lambda h, t, qi, kj: (0, 0))
    gs = pltpu.PrefetchScalarGridSpec(
        num_scalar_prefetch=2,
        grid=(DIFF_HEADS, qi.shape[0]),
        in_specs=[pl.BlockSpec((bq, hw), lambda h, t, qi, kj: (qi[t], h)),
                  pl.BlockSpec((bk, hw), lambda h, t, qi, kj: (kj[t], h)),
                  pl.BlockSpec((V_EXT, bk), lambda h, t, qi, kj: (h, kj[t])),
                  small(lq1), small(lk1), small(lq2), small(lk2), small(g_col)],
        out_specs=pl.BlockSpec((bq, hw), lambda h, t, qi, kj: (qi[t], h)),
        scratch_shapes=[pltpu.VMEM((bq, hw), BF16), pltpu.VMEM((bq, hw), BF16)] + _unit_scratch(2, bq, bk),
    )
    return pl.pallas_call(
        _diff_attn_kernel,
        grid_spec=gs,
        out_shape=jax.ShapeDtypeStruct((s, DIFF_W), BF16),
        compiler_params=_cparams(("arbitrary", "arbitrary")),
        name="diff_attn",
    )(qi, kj, dq, dk, dvt, lq1, lk1, lq2, lk2, g_col)


def _pack_words(y):
    lo = pltpu.bitcast(y[:, :HALF].astype(BF16).astype(F32), U32) >> 16
    hi = pltpu.bitcast(y[:, HALF:].astype(BF16).astype(F32), U32) & jnp.uint32(0xFFFF0000)
    return lo | hi


def _store_packed(ref, words, rows):
    for s in range(ROW_TILES):
        ref[pl.ds(s, rows, stride=ROW_TILES), :] = words[:, s * LANES:(s + 1) * LANES]


def _load_packed(ref, rows):
    return jnp.concatenate(
        [ref[pl.ds(s, rows, stride=ROW_TILES), :] for s in range(ROW_TILES)], axis=1)


def _unpack_words(w):
    lo = pltpu.bitcast(w << 16, F32)
    hi = pltpu.bitcast(w & jnp.uint32(0xFFFF0000), F32)
    return lo, hi


def _outproj_kernel(om_ref, od_ref, x_ref, w_ref, gt_ref, gpost_ref, gpre_ref, sc_ref, sh_ref,
                    wr_ref, x1_ref, h2_ref, h2p_ref, lg_ref):
    n_sub = x_ref.shape[0] // TM_PROJ
    sub = lambda r, j: r.at[pl.ds(j * TM_PROJ, TM_PROJ)]
    w_hi, w_lo = _split_bf16(wr_ref[...])
    ne = w_hi.shape[0]
    w_hl = jnp.concatenate([w_hi, w_lo], axis=0)
    ys = [_dot(jnp.concatenate([sub(om_ref, j)[...], sub(od_ref, j)[...]], axis=1), w_ref[...])
          for j in range(n_sub)]
    for j, y in enumerate(ys):
        x1 = sub(x_ref, j)[...] + gt_ref[...] * (_rms(y) * gpost_ref[...])
        sub(x1_ref, j)[...] = x1
        h2 = _rms(x1) * gpre_ref[...] * (1.0 + sc_ref[...]) + sh_ref[...]
        h_hi, h_lo = _split_bf16(h2)
        sub(h2_ref, j)[...] = h_hi
        _store_packed(h2p_ref.at[pl.ds(j * TM_PROJ * ROW_TILES, TM_PROJ * ROW_TILES)], _pack_words(h2), TM_PROJ)
        a = _dot_nt(w_hl, h_hi)
        lg_ref[:, j * TM_PROJ:(j + 1) * TM_PROJ] = a[0:ne] + a[ne:] + _dot_nt(w_hi, h_lo)


def _outproj(om, od, x, w_out, gt, gpost, gpre, sc, sh, wr_t):
    s, d = x.shape
    tm = min(TM_OUT, s)
    row = lambda n: pl.BlockSpec((tm, n), lambda i: (i, 0))
    vec = _const_spec((1, d))
    return pl.pallas_call(
        _outproj_kernel,
        grid=(s // tm,),
        in_specs=[row(om.shape[1]), row(od.shape[1]), row(d), _const_spec(w_out.shape),
                  vec, vec, vec, vec, vec, _const_spec(wr_t.shape)],
        out_specs=[row(d), row(d),
                   pl.BlockSpec((tm * ROW_TILES, LANES), lambda i: (i, 0)),
                   pl.BlockSpec((N_EXPERTS, tm), lambda i: (0, i))],
        out_shape=[jax.ShapeDtypeStruct((s, d), F32),
                   jax.ShapeDtypeStruct((s, d), BF16),
                   jax.ShapeDtypeStruct((s * ROW_TILES, LANES), U32),
                   jax.ShapeDtypeStruct((N_EXPERTS, s), F32)],
        compiler_params=_cparams(("arbitrary",)),
        name="out_proj",
    )(om, od, x, w_out, gt, gpost, gpre, sc, sh, wr_t)


def _route_kernel(lg_ref, b_ref, eidx_ref, w_ref, rank_ref, cnt_ref, carry):
    step = pl.program_id(0)

    @pl.when(step == 0)
    def _():
        carry[...] = jnp.zeros(carry.shape, F32)

    tr = lg_ref.shape[1]
    ninf = jnp.float32(-jnp.inf)
    ie = lax.broadcasted_iota(I32, (GROUP_SIZE, tr), 0)
    rmax = lambda a: jnp.max(a, axis=0, keepdims=True)
    rmin = lambda a: jnp.min(a, axis=0, keepdims=True)
    rsum = lambda a: jnp.sum(a, axis=0, keepdims=True)

    sc, bi, gscore = [], [], []
    for g in range(N_GROUPS):
        lg = lg_ref[g * GROUP_SIZE:(g + 1) * GROUP_SIZE, :]
        s = jax.nn.sigmoid(lg)
        b = s + b_ref[g * GROUP_SIZE:(g + 1) * GROUP_SIZE, :]
        m1 = rmax(b)
        i1 = rmin(jnp.where(b == m1, ie, GROUP_SIZE))
        m2 = rmax(jnp.where(ie == i1, ninf, b))
        sc.append(s)
        bi.append(b)
        gscore.append(m1 + m2)

    gsel = [jnp.zeros((1, tr), I32) for _ in range(N_GROUPS)]
    for _ in range(TOPK_GROUPS):
        gm = functools.reduce(jnp.maximum, gscore)
        gi = functools.reduce(
            jnp.minimum, [jnp.where(gscore[g] == gm, g, N_GROUPS) for g in range(N_GROUPS)])
        for g in range(N_GROUPS):
            hit = gi == g
            gsel[g] = jnp.where(hit, 1, gsel[g])
            gscore[g] = jnp.where(hit, ninf, gscore[g])

    masked = [jnp.where(jnp.broadcast_to(gsel[g], (GROUP_SIZE, tr)) > 0, bi[g], ninf)
              for g in range(N_GROUPS)]
    sel = [jnp.zeros((GROUP_SIZE, tr), F32) for _ in range(N_GROUPS)]
    idxs, svals = [], []
    for _ in range(TOP_K):
        m = functools.reduce(jnp.maximum, [rmax(x) for x in masked])
        idx = functools.reduce(
            jnp.minimum,
            [rmin(jnp.where(masked[g] == m, ie + g * GROUP_SIZE, N_EXPERTS)) for g in range(N_GROUPS)])
        sv = jnp.zeros((1, tr), F32)
        for g in range(N_GROUPS):
            hit = (ie + g * GROUP_SIZE) == idx
            sv = sv + rsum(jnp.where(hit, sc[g], 0.0))
            masked[g] = jnp.where(hit, ninf, masked[g])
            sel[g] = jnp.where(hit, 1.0, sel[g])
        idxs.append(idx)
        svals.append(sv)
    tot = functools.reduce(lambda a, b: a + b, svals)

    sel2 = jnp.concatenate(sel, axis=0)
    before = (lax.broadcasted_iota(I32, (tr, tr), 0) < lax.broadcasted_iota(I32, (tr, tr), 1))
    rank2 = _dot(sel2.astype(BF16), before.astype(BF16)) + carry[:, 0:1]
    for k in range(TOP_K):
        rk = jnp.zeros((1, tr), F32)
        for g in range(N_GROUPS):
            hit = (ie + g * GROUP_SIZE) == idxs[k]
            rk = rk + rsum(jnp.where(hit, rank2[g * GROUP_SIZE:(g + 1) * GROUP_SIZE, :], 0.0))
        eidx_ref[k:k + 1, :] = idxs[k]
        w_ref[k:k + 1, :] = svals[k] / tot * ROUTED_SCALE
        rank_ref[k:k + 1, :] = rk.astype(I32)
    carry[...] = carry[...] + jnp.sum(sel2, axis=1, keepdims=True)
    cnt_ref[...] = carry[...]


def _route(lg_t, b_col):
    e, t = lg_t.shape
    tr = min(TR, t)
    tok = lambda: pl.BlockSpec((TOP_K, tr), lambda i: (0, i))
    return pl.pallas_call(
        _route_kernel,
        grid=(t // tr,),
        in_specs=[pl.BlockSpec((e, tr), lambda i: (0, i)), pl.BlockSpec((e, 1), lambda i: (0, 0))],
        out_specs=[tok(), tok(), tok(), pl.BlockSpec((e, LANES), lambda i: (0, 0))],
        out_shape=[jax.ShapeDtypeStruct((TOP_K, t), I32),
                   jax.ShapeDtypeStruct((TOP_K, t), F32),
                   jax.ShapeDtypeStruct((TOP_K, t), I32),
                   jax.ShapeDtypeStruct((e, LANES), F32)],
        scratch_shapes=[pltpu.VMEM((e, LANES), F32)],
        compiler_params=_cparams(("arbitrary",)),
        name="route",
    )(lg_t, b_col)


def _pos_kernel(pstart_ref, eidx_ref, rank_ref, pos_ref):
    e = eidx_ref[...]
    pos = rank_ref[...]
    for x in range(N_EXPERTS):
        pos = pos + jnp.where(e == x, pstart_ref[x], 0)
    pos_ref[...] = pos


def _positions(pstart, eidx, rank):
    k, t = eidx.shape
    tr = min(TR, t)
    gs = pltpu.PrefetchScalarGridSpec(
        num_scalar_prefetch=1,
        grid=(t // tr,),
        in_specs=[pl.BlockSpec((k, tr), lambda i, p: (0, i)), pl.BlockSpec((k, tr), lambda i, p: (0, i))],
        out_specs=pl.BlockSpec((k, tr), lambda i, p: (0, i)),
    )
    return pl.pallas_call(
        _pos_kernel, grid_spec=gs,
        out_shape=jax.ShapeDtypeStruct((k, t), I32),
        compiler_params=_cparams(("arbitrary",)),
        name="positions",
    )(pstart, eidx, rank)


def _row_copy(src, src_row, dst, dst_row, sem):
    return pltpu.make_async_copy(src.at[pl.ds(src_row * ROW_TILES, ROW_TILES)],
                                 dst.at[pl.ds(dst_row * ROW_TILES, ROW_TILES)], sem)


FILL_CHUNKS = (128, 64, 32, 16, 8, 4, 2, 1)


def _zero_fill(fs_ref, fl_ref, tail_ref, zbuf, xs_ref, sem, wait):
    def chunk(row, n):
        return pltpu.make_async_copy(zbuf.at[pl.ds(0, n * ROW_TILES)],
                                     xs_ref.at[pl.ds(row * ROW_TILES, n * ROW_TILES)], sem)

    def go(cp):
        if wait:
            cp.wait()
        else:
            cp.start()

    def per_expert(e, c):
        pad = fl_ref[e]
        row = fs_ref[e]
        for n in FILL_CHUNKS:
            @pl.when((pad & n) != 0)
            def _(row=row, n=n):
                go(chunk(row, n))
            row = row + (pad & n)
        return c

    lax.fori_loop(0, N_EXPERTS, per_expert, 0)

    def per_tail(b, c):
        for part in range(EXP_BLK // FILL_CHUNKS[0]):
            go(chunk(b * EXP_BLK + part * FILL_CHUNKS[0], FILL_CHUNKS[0]))
        return c

    lax.fori_loop(tail_ref[0], tail_ref[1], per_tail, 0)


def _dispatch_kernel(fs_ref, fl_ref, tail_ref, pos_ref, h_ref, xs_ref, zbuf, sem, fill_sem):
    td = pos_ref.shape[1]
    step = pl.program_id(0)

    @pl.when(step == 0)
    def _():
        zbuf[...] = jnp.zeros(zbuf.shape, zbuf.dtype)
        _zero_fill(fs_ref, fl_ref, tail_ref, zbuf, xs_ref, fill_sem, wait=False)

    def issue(t, c):
        for k in range(TOP_K):
            _row_copy(h_ref, t, xs_ref, pos_ref[k, t], sem).start(priority=k % 2)
        return c

    lax.fori_loop(0, td, issue, 0, unroll=4)
    for k in range(TOP_K):
        pltpu.make_async_copy(h_ref, xs_ref.at[pl.ds(0, td * ROW_TILES)], sem).wait()

    @pl.when(step == pl.num_programs(0) - 1)
    def _():
        _zero_fill(fs_ref, fl_ref, tail_ref, zbuf, xs_ref, fill_sem, wait=True)


def _dispatch(fill_start, fill_len, tail, pos, h2p, m_pad):
    k, t = pos.shape
    td = min(TD, t)
    gs = pltpu.PrefetchScalarGridSpec(
        num_scalar_prefetch=3,
        grid=(t // td,),
        in_specs=[pl.BlockSpec((k, td), lambda i, *_: (0, i), memory_space=pltpu.SMEM),
                  pl.BlockSpec((td * ROW_TILES, LANES), lambda i, *_: (i, 0))],
        out_specs=pl.BlockSpec(memory_space=pl.ANY),
        scratch_shapes=[pltpu.VMEM((FILL_CHUNKS[0] * ROW_TILES, LANES), U32),
                        pltpu.SemaphoreType.DMA, pltpu.SemaphoreType.DMA],
    )
    return pl.pallas_call(
        _dispatch_kernel,
        grid_spec=gs,
        out_shape=jax.ShapeDtypeStruct((m_pad * ROW_TILES, LANES), U32),
        compiler_params=_cparams(("arbitrary",)),
        name="dispatch",
    )(fill_start, fill_len, tail, pos, h2p)


def _weight_copies(w_hbm, w_buf, sem, e, s):
    return [pltpu.make_async_copy(h.at[e], b.at[s], sem.at[s]) for h, b in zip(w_hbm, w_buf)]


def _experts_kernel(be_ref, nu_ref, fresh_ref, nxt_ref, slot_ref, x_ref, wg_hbm, wu_hbm, wd_hbm, y_ref,
                    wg_buf, wu_buf, wd_buf, wg_s, wu_s, wd_s, sem):
    step = pl.program_id(0)
    b0 = step * BLKS_PER_STEP
    blk_rows = x_ref.shape[0] // BLKS_PER_STEP
    w_hbm = (wg_hbm, wu_hbm, wd_hbm)
    w_buf = (wg_buf, wu_buf, wd_buf)

    def maybe_weights(b):
        @pl.when(fresh_ref[b] == 1)
        def _():
            s = slot_ref[b]

            @pl.when(b == 0)
            def _():
                for c in _weight_copies(w_hbm, w_buf, sem, be_ref[0], 0):
                    c.start()

            for c in _weight_copies(w_hbm, w_buf, sem, be_ref[b], s):
                c.wait()

            @pl.when(nxt_ref[b] >= 0)
            def _():
                for c in _weight_copies(w_hbm, w_buf, sem, nxt_ref[b], 1 - s):
                    c.start()

            wg_s[...] = wg_buf[s].astype(BF16)
            wu_s[...] = wu_buf[s].astype(BF16)
            wd_s[...] = wd_buf[s].astype(BF16)

    def swiglu(first, n):
        x_blk = x_ref.at[pl.ds(first * blk_rows, n * blk_rows)]
        y_blk = y_ref.at[pl.ds(first * blk_rows, n * blk_rows)]
        rows = n * blk_rows // ROW_TILES
        xa, xb = _unpack_words(_load_packed(x_blk, rows))
        xa = xa.astype(BF16)
        xb = xb.astype(BF16)
        g = _dot(xa, wg_s[0:HALF, :]) + _dot(xb, wg_s[HALF:, :])
        u = _dot(xa, wu_s[0:HALF, :]) + _dot(xb, wu_s[HALF:, :])
        a = (g * jax.nn.sigmoid(g) * u).astype(BF16)
        _store_packed(y_blk, _pack_words(_dot(a, wd_s[...])), rows)

    def zero(first, n):
        y_ref[pl.ds(first * blk_rows, n * blk_rows), :] = jnp.zeros((n * blk_rows, LANES), y_ref.dtype)

    n_used = jnp.clip(nu_ref[0] - b0, 0, BLKS_PER_STEP)
    same = jnp.logical_and(n_used == BLKS_PER_STEP, be_ref[b0] == be_ref[b0 + 1])

    @pl.when(same)
    def _():
        maybe_weights(b0)
        swiglu(0, BLKS_PER_STEP)

    @pl.when(jnp.logical_and(jnp.logical_not(same), n_used > 0))
    def _():
        maybe_weights(b0)
        swiglu(0, 1)

        @pl.when(n_used > 1)
        def _():
            maybe_weights(b0 + 1)
            swiglu(1, 1)

        @pl.when(n_used == 1)
        def _():
            zero(1, 1)

    @pl.when(n_used == 0)
    def _():
        zero(0, BLKS_PER_STEP)


def _experts(blk_e, nused, fresh, nxt_e, slot, xs, w_gate, w_up, w_down, blk):
    n_blk = blk_e.shape[0]
    assert n_blk % BLKS_PER_STEP == 0 and BLKS_PER_STEP == 2
    ne, d, de = w_gate.shape
    rows = BLKS_PER_STEP * blk * ROW_TILES
    xmap = lambda i, be, nu, fr, nx, sl: (jnp.minimum(i, (nu[0] - 1) // BLKS_PER_STEP), 0)
    hbm = pl.BlockSpec(memory_space=pl.ANY)
    gs = pltpu.PrefetchScalarGridSpec(
        num_scalar_prefetch=5,
        grid=(n_blk // BLKS_PER_STEP,),
        in_specs=[pl.BlockSpec((rows, LANES), xmap), hbm, hbm, hbm],
        out_specs=pl.BlockSpec((rows, LANES), lambda i, *_: (i, 0)),
        scratch_shapes=[pltpu.VMEM((2, d, de), F32), pltpu.VMEM((2, d, de), F32), pltpu.VMEM((2, de, d), F32),
                        pltpu.VMEM((d, de), BF16), pltpu.VMEM((d, de), BF16), pltpu.VMEM((de, d), BF16),
                        pltpu.SemaphoreType.DMA((2,))],
    )
    return pl.pallas_call(
        _experts_kernel, grid_spec=gs,
        out_shape=jax.ShapeDtypeStruct(xs.shape, U32),
        compiler_params=_cparams(("arbitrary",)),
        name="experts",
    )(blk_e, nused, fresh, nxt_e, slot, xs, w_gate, w_up, w_down)


def _combine_kernel(pos_ref, wt_ref, ys_ref, h_ref, x1_ref, wsg_ref, wsu_ref, wsd_ref,
                    gt_ref, gpost_ref, o_ref, gbuf, sem):
    td = pos_ref.shape[1]

    def issue(t, c):
        for k in range(TOP_K):
            _row_copy(ys_ref, pos_ref[k, t], gbuf.at[k], t, sem).start(priority=k % 2)
        return c

    lax.fori_loop(0, td, issue, 0, unroll=4)

    hb = h_ref[...]
    g = _dot(hb, wsg_ref[...])
    u = _dot(hb, wsu_ref[...])
    y = _dot((g * jax.nn.sigmoid(g) * u).astype(BF16), wsd_ref[...])

    wpad = jnp.concatenate([wt_ref[...], jnp.zeros((LANES - TOP_K, td), F32)], axis=0)
    wcol = jnp.concatenate([wpad[:, j * LANES:(j + 1) * LANES].T for j in range(td // LANES)], axis=0)

    for k in range(TOP_K):
        pltpu.make_async_copy(ys_ref.at[pl.ds(0, td * ROW_TILES)], gbuf.at[k], sem).wait()
    lo = jnp.zeros((td, HALF), F32)
    hi = jnp.zeros((td, HALF), F32)
    for k in range(TOP_K):
        a, b = _unpack_words(_load_packed(gbuf.at[k], td))
        wk = wcol[:, k:k + 1]
        lo = lo + wk * a
        hi = hi + wk * b
    y = y + jnp.concatenate([lo, hi], axis=1)
    o_ref[...] = x1_ref[...] + gt_ref[...] * (_rms(y) * gpost_ref[...])


def _combine(pos, wts, ys, h2, x1, wsg, wsu, wsd, gt, gpost):
    k, t = pos.shape
    d = x1.shape[1]
    td = min(TD, t)
    assert td % LANES == 0
    row = lambda n: pl.BlockSpec((td, n), lambda i: (i, 0))
    vec = _const_spec((1, d))
    return pl.pallas_call(
        _combine_kernel,
        grid=(t // td,),
        in_specs=[pl.BlockSpec((k, td), lambda i: (0, i), memory_space=pltpu.SMEM),
                  pl.BlockSpec((k, td), lambda i: (0, i)),
                  pl.BlockSpec(memory_space=pl.ANY),
                  row(d), row(d), _const_spec(wsg.shape), _const_spec(wsu.shape), _const_spec(wsd.shape),
                  vec, vec],
        out_specs=row(d),
        out_shape=jax.ShapeDtypeStruct((t, d), F32),
        scratch_shapes=[pltpu.VMEM((k, td * ROW_TILES, LANES), U32), pltpu.SemaphoreType.DMA],
        compiler_params=_cparams(("arbitrary",)),
        name="combine",
    )(pos, wts, ys, h2, x1, wsg, wsu, wsd, gt, gpost)


def _rope_tables(seq, dim):
    pos = jnp.arange(seq, dtype=F32)
    inv = ROPE_THETA ** (-jnp.arange(0, dim, 2, dtype=F32) / dim)
    ang = pos[:, None] * inv[None, :]
    return jnp.cos(ang), jnp.sin(ang)


def _lane_tables(seq, dim, period):
    cos, sin = _rope_tables(seq, dim)
    half = dim // 2
    cos = jnp.tile(cos, (1, LANES // half))
    sin = jnp.tile(sin, (1, LANES // half))
    r = jnp.arange(LANES) % period
    c = jnp.where(r < dim, cos, 1.0)
    sa = jnp.where(r < half, -sin, 0.0)
    sb = jnp.where((r >= half) & (r < dim), sin, 0.0)
    return c, sa, sb


def _moe(h2, h2p, x1, lg_t, b_router, w_gate, w_up, w_down, wsg, wsu, wsd, gt_f, g_post_ffn):
    t = h2.shape[0]
    blk = EXP_BLK
    m_pad = t * TOP_K + N_EXPERTS * blk
    eidx, wts, rank, cnt = _route(lg_t, b_router.reshape(N_EXPERTS, 1))
    counts = cnt[:, 0].astype(I32)
    padded = (counts + blk - 1) // blk * blk
    pend = jnp.cumsum(padded)
    pstart = pend - padded
    n_blk = m_pad // blk
    blk_start = jnp.arange(n_blk, dtype=I32) * blk
    blk_e = jnp.minimum(jnp.sum(pend[None, :] <= blk_start[:, None], axis=1), N_EXPERTS - 1).astype(I32)
    nused = (pend[-1:] // blk).astype(I32)
    pos = _positions(pstart.astype(I32), eidx, rank)
    tail = jnp.concatenate([nused, jnp.full((1,), n_blk, I32)])
    xs = _dispatch((pstart + counts).astype(I32), (padded - counts).astype(I32), tail, pos, h2p, m_pad)
    idx = jnp.arange(n_blk, dtype=I32)
    fresh = (idx < nused[0]) & (blk_e != jnp.concatenate([jnp.full((1,), -1, I32), blk_e[:-1]]))
    slot = ((jnp.cumsum(fresh.astype(I32)) - 1) % 2).astype(I32)
    first_at_or_after = lax.cummin(jnp.where(fresh, idx, n_blk)[::-1])[::-1]
    nxt_idx = jnp.concatenate([first_at_or_after[1:], jnp.full((1,), n_blk, I32)])
    nxt_e = jnp.where(nxt_idx < n_blk, blk_e[jnp.minimum(nxt_idx, n_blk - 1)], -1).astype(I32)
    ys = _experts(blk_e, nused, fresh.astype(I32), nxt_e, slot, xs, w_gate, w_up, w_down, blk)
    return _combine(pos, wts, ys, h2, x1, wsg, wsu, wsd, gt_f, g_post_ffn)


def _layer(x, c, w_ada, b_ada, g_pre_mix, w_in, g_q_lat, w_uq, g_kv_lat, w_ukv,
           lq1, lk1, lq2, lk2, g_diff_sub, w_out, g_post_mix, g_pre_ffn, w_router, b_router,
           w_gate, w_up, w_down, ws_gate, ws_up, ws_down, g_post_ffn):
    s, d = x.shape
    row = lambda a: a.reshape(1, -1)

    mod = _ada(c, w_ada, row(b_ada))
    sh_a, sc_a, gt_a, sh_f, sc_f, gt_f = [mod[:, i * d:(i + 1) * d] for i in range(6)]

    o1 = MLA_Q_RANK + MLA_KV_RANK
    o2 = o1 + MLA_ROPE
    o3 = o2 + 2 * DIFF_W
    w_lat, w_qk, w_kpe, wdv_t = _split_w_in(w_in.T, (o1, o2, o3))
    wq = w_uq.reshape(MLA_Q_RANK, MLA_HEADS, MLA_NOPE + MLA_ROPE)
    wq = jnp.pad(wq, ((0, 0), (0, 0), (0, MLA_QK_PAD - MLA_NOPE - MLA_ROPE)))
    wuq_r = wq.reshape(MLA_Q_RANK, MLA_HEADS * MLA_QK_PAD).astype(BF16)
    wkv = w_ukv.reshape(MLA_KV_RANK, MLA_HEADS, MLA_NOPE + MLA_V)
    wuk_r = wkv[:, :, :MLA_NOPE].reshape(MLA_KV_RANK, -1).astype(BF16)
    wmv_t = wkv[:, :, MLA_NOPE:].reshape(MLA_KV_RANK, -1).T.astype(BF16)
    dq, dk, dvt, qc, kc, vt = _inproj(
        x, row(g_pre_mix), sc_a, sh_a, w_lat, w_qk, w_kpe, wdv_t,
        _lane_tables(s, DIFF_ROT, DIFF_HEAD_DIM), row(g_q_lat), row(g_kv_lat), wuq_r, wuk_r, wmv_t,
        _lane_tables(s, MLA_ROPE, MLA_ROPE))

    o_mla = _mla_attn(qc, kc, vt)
    o_diff = _diff_attn(dq, dk, dvt, row(lq1), row(lk1), row(lq2), row(lk2), g_diff_sub.reshape(-1, 1))

    x1, h2, h2p, lg_t = _outproj(o_mla, o_diff, x, w_out.astype(BF16), gt_a, row(g_post_mix),
                                 row(g_pre_ffn), sc_f, sh_f, w_router.T)
    return _moe(h2, h2p, x1, lg_t, b_router, w_gate, w_up, w_down,
                ws_gate.astype(BF16), ws_up.astype(BF16), ws_down.astype(BF16), gt_f, row(g_post_ffn))


def kernel(x, c, w_ada, b_ada, g_pre_mix, w_in, g_q_lat, w_uq, g_kv_lat, w_ukv, lambda_q1, lambda_k1, lambda_q2, lambda_k2, g_diff_sub, w_out, g_post_mix, g_pre_ffn, w_router, b_router, w_gate, w_up, w_down, ws_gate, ws_up, ws_down, g_post_ffn):
    assert x.shape[0] == 1 and w_ada.shape[0] == 1
    out = _layer(x[0], c, w_ada[0], b_ada[0], g_pre_mix[0], w_in[0], g_q_lat[0], w_uq[0],
                 g_kv_lat[0], w_ukv[0], lambda_q1[0], lambda_k1[0], lambda_q2[0], lambda_k2[0],
                 g_diff_sub[0], w_out[0], g_post_mix[0], g_pre_ffn[0], w_router[0], b_router[0],
                 w_gate[0], w_up[0], w_down[0], ws_gate[0], ws_up[0], ws_down[0], g_post_ffn[0])
    return out[None]
```

```python
import functools
import math

import jax
import jax.numpy as jnp
import numpy as np
from jax import lax
from jax.experimental import pallas as pl
from jax.experimental.pallas import tpu as pltpu

F32 = jnp.float32
BF16 = jnp.bfloat16
U32 = jnp.uint32
I32 = jnp.int32

D_MODEL = 2048
CHUNK = 64
ROPE_THETA = 500000.0
EPS = 1e-6
LOG2E = 1.4426950408889634

MLA_HEADS = 8
MLA_Q_RANK = 768
MLA_KV_RANK = 512
MLA_NOPE = 128
MLA_ROPE = 64
MLA_V = 128
MLA_QK_PAD = 256

DIFF_HEADS = 8
DIFF_HEAD_DIM = 64
DIFF_ROT = DIFF_HEAD_DIM // 4
DIFF_W = DIFF_HEADS * 2 * DIFF_HEAD_DIM

N_EXPERTS = 64
TOP_K = 8
N_GROUPS = 8
GROUP_SIZE = N_EXPERTS // N_GROUPS
TOPK_GROUPS = 4
D_EXPERT = 512
ROUTED_SCALE = 2.5
LAMBDA_INIT = 0.8 - 0.6 * math.exp(-0.3 * 0)

LANES = 128
SUBLANES = 8
HALF = D_MODEL // 2
ROW_TILES = HALF // LANES

TM_PROJ = 256
TM_OUT = 2 * TM_PROJ
MLA_BQ = 4096
DIFF_BQ = 4096
MLA_BK = 4096
DIFF_BK = 1024
UNIT_W = 512
V_EXT = MLA_V + 16
EXP_BLK = 256
BLKS_PER_STEP = 2
TD = 256
TR = 512
TN_ADA = 1024

VMEM_LIMIT = 56 * 1024 * 1024

assert ROW_TILES == SUBLANES


def _cparams(sem, vmem=VMEM_LIMIT, flags=None):
    return pltpu.CompilerParams(dimension_semantics=sem, vmem_limit_bytes=vmem, flags=flags)


def _dot(a, b):
    return jnp.dot(a, b, preferred_element_type=F32)


def _dot_nt(a, b):
    return lax.dot_general(a, b, (((1,), (1,)), ((), ())), preferred_element_type=F32)


def _rms(x):
    return x * lax.rsqrt(jnp.mean(x * x, axis=-1, keepdims=True) + EPS)


def _split_bf16(x):
    hi = x.astype(BF16)
    lo = (x - hi.astype(F32)).astype(BF16)
    return hi, lo


def _const_spec(shape):
    nd = len(shape)
    return pl.BlockSpec(shape, lambda *a: (0,) * nd, pipeline_mode=pl.Buffered(1))


def _ada_kernel(c_ref, w_ref, b_ref, o_ref):
    c = c_ref[...]
    a = c * jax.nn.sigmoid(c)
    a8 = jnp.broadcast_to(a, (SUBLANES, a.shape[1]))
    a_hi, a_lo = _split_bf16(a8)
    w_hi, w_lo = _split_bf16(w_ref[...])
    r = _dot(a_hi, w_hi) + _dot(a_lo, w_hi) + _dot(a_hi, w_lo)
    o_ref[...] = r[0:1] + b_ref[...]


def _ada(c, w, b):
    d, n = w.shape
    return pl.pallas_call(
        _ada_kernel,
        grid=(n // TN_ADA,),
        in_specs=[pl.BlockSpec((1, d), lambda j: (0, 0)),
                  pl.BlockSpec((d, TN_ADA), lambda j: (0, j)),
                  pl.BlockSpec((1, TN_ADA), lambda j: (0, j))],
        out_specs=pl.BlockSpec((1, TN_ADA), lambda j: (0, j)),
        out_shape=jax.ShapeDtypeStruct((1, n), F32),
        compiler_params=_cparams(("arbitrary",)),
        name="ada_mod",
    )(c, w, b)


def _tile_lanes(t, reps):
    return jnp.concatenate([t] * reps, axis=1)


def _rope_lanes(x, c, sa, sb, half):
    n = x.shape[1]
    return x * c + pltpu.roll(x, n - half, 1) * sa + pltpu.roll(x, half, 1) * sb


def _split_w_in_kernel(offs, wt_ref, lat_ref, qk_ref, kpe_ref, dvt_ref):
    o1, o2, o3 = offs
    wt = wt_ref[...]
    lat_ref[...] = wt[:o1, :].T.astype(BF16)
    qk_ref[...] = wt[o2:o3, :].T.astype(BF16)
    kpe = wt[o1:o2, :].T
    kpe_ref[...] = jnp.concatenate([kpe, jnp.zeros_like(kpe)], axis=1).astype(BF16)
    dvt_ref[...] = wt[o3:, :].astype(BF16)


def _split_w_in(w_in_t, offs):
    n, d = w_in_t.shape
    o1, o2, o3 = offs
    tk = TM_PROJ
    return pl.pallas_call(
        functools.partial(_split_w_in_kernel, offs),
        grid=(d // tk,),
        in_specs=[pl.BlockSpec((n, tk), lambda i: (0, i))],
        out_specs=[pl.BlockSpec((tk, o1), lambda i: (i, 0)), pl.BlockSpec((tk, o3 - o2), lambda i: (i, 0)),
                   pl.BlockSpec((tk, LANES), lambda i: (i, 0)), pl.BlockSpec((n - o3, tk), lambda i: (0, i))],
        out_shape=[jax.ShapeDtypeStruct((d, o1), BF16), jax.ShapeDtypeStruct((d, o3 - o2), BF16),
                   jax.ShapeDtypeStruct((d, LANES), BF16), jax.ShapeDtypeStruct((n - o3, d), BF16)],
        compiler_params=_cparams(("arbitrary",)),
        name="split_w_in",
    )(w_in_t)


def _mla_heads(lat, kpe_raw, gq_ref, gkv_ref, wuq_ref, wuk_ref, wvt_ref, c, sa, sb, q_ref, k_ref, vt_ref):
    qn = (_rms(lat[:, :MLA_Q_RANK]) * gq_ref[...]).astype(BF16)
    kvn = (_rms(lat[:, MLA_Q_RANK:]) * gkv_ref[...]).astype(BF16)
    q = _dot(qn, wuq_ref[...])
    kn = _dot(kvn, wuk_ref[...])
    half = MLA_ROPE // 2
    qs = (MLA_NOPE + MLA_ROPE) ** -0.5 * LOG2E
    kpe = _rope_lanes(kpe_raw, c, sa, sb, half).astype(BF16)
    for h in range(MLA_HEADS):
        o = h * MLA_QK_PAD
        q_ref[:, o:o + LANES] = (q[:, o:o + LANES] * qs).astype(BF16)
        q_ref[:, o + LANES:o + 2 * LANES] = (
            _rope_lanes(q[:, o + LANES:o + 2 * LANES], c, sa, sb, half) * qs).astype(BF16)
        k_ref[:, o:o + LANES] = kn[:, h * MLA_NOPE:(h + 1) * MLA_NOPE].astype(BF16)
        k_ref[:, o + LANES:o + 2 * LANES] = kpe
    _store_vt_ext(vt_ref, _dot_nt(wvt_ref[...], kvn), MLA_HEADS)


def _inproj_kernel(x_ref, g_ref, sc_ref, sh_ref, wlat_ref, wqk_ref, wkpe_ref, wdvt_ref,
                   dc_ref, dsa_ref, dsb_ref, gq_ref, gkv_ref, wuq_ref, wuk_ref, wmvt_ref,
                   mc_ref, msa_ref, msb_ref,
                   dq_ref, dk_ref, dvt_ref, q_ref, k_ref, vt_ref):
    x = x_ref[...]
    h = _rms(x) * g_ref[...] * (1.0 + sc_ref[...]) + sh_ref[...]
    hb = h.astype(BF16)
    reps = DIFF_W // LANES
    c = _tile_lanes(dc_ref[...], reps)
    sa = _tile_lanes(dsa_ref[...], reps)
    sb = _tile_lanes(dsb_ref[...], reps)
    half = DIFF_ROT // 2
    q = _dot(hb, wqk_ref[:, 0:DIFF_W])
    dq_ref[...] = (_rope_lanes(q, c, sa, sb, half) * (DIFF_HEAD_DIM ** -0.5 * LOG2E)).astype(BF16)
    k = _dot(hb, wqk_ref[:, DIFF_W:])
    dk_ref[...] = _rope_lanes(k, c, sa, sb, half).astype(BF16)
    _store_vt_ext(dvt_ref, _dot_nt(wdvt_ref[...], hb), DIFF_HEADS)
    _mla_heads(_dot(hb, wlat_ref[...]), _dot(hb, wkpe_ref[...]), gq_ref, gkv_ref, wuq_ref, wuk_ref,
               wmvt_ref, mc_ref[...], msa_ref[...], msb_ref[...], q_ref, k_ref, vt_ref)


def _inproj(x, g, sc, sh, w_lat, w_qk, w_kpe, wdv_t, dtabs, gq, gkv, wuq_r, wuk_r, wmv_t, mtabs):
    s, d = x.shape
    tm = min(TM_PROJ, s)
    row = lambda n: pl.BlockSpec((tm, n), lambda i: (i, 0))
    col = lambda n: pl.BlockSpec((n, tm), lambda i: (0, i))
    const = lambda a: _const_spec(a.shape)
    hq = MLA_HEADS * MLA_QK_PAD
    return pl.pallas_call(
        _inproj_kernel,
        grid=(s // tm,),
        in_specs=[row(d), const(g), const(sc), const(sh), const(w_lat), const(w_qk), const(w_kpe),
                  const(wdv_t), row(LANES), row(LANES), row(LANES),
                  const(gq), const(gkv), const(wuq_r), const(wuk_r), const(wmv_t),
                  row(LANES), row(LANES), row(LANES)],
        out_specs=[row(DIFF_W), row(DIFF_W), col(DIFF_HEADS * V_EXT), row(hq), row(hq), col(MLA_HEADS * V_EXT)],
        out_shape=[jax.ShapeDtypeStruct((s, DIFF_W), BF16),
                   jax.ShapeDtypeStruct((s, DIFF_W), BF16),
                   jax.ShapeDtypeStruct((DIFF_HEADS * V_EXT, s), BF16),
                   jax.ShapeDtypeStruct((s, hq), BF16),
                   jax.ShapeDtypeStruct((s, hq), BF16),
                   jax.ShapeDtypeStruct((MLA_HEADS * V_EXT, s), BF16)],
        compiler_params=_cparams(("arbitrary",)),
        name="in_proj",
    )(x, g, sc, sh, w_lat, w_qk, w_kpe, wdv_t, *dtabs, gq, gkv, wuq_r, wuk_r, wmv_t, *mtabs)


def _pair_tables(nq, kv_per_q):
    n = [kv_per_q * (i + 1) for i in range(nq)]
    qi = np.concatenate([np.full(c, i) for i, c in enumerate(n)]).astype(np.int32)
    kj = np.concatenate([np.arange(c) for c in n]).astype(np.int32)
    return jnp.asarray(qi), jnp.asarray(kj)


def _ones_rows(n):
    rows = V_EXT - MLA_V
    return (lax.broadcasted_iota(I32, (rows, n), 0) == 0).astype(BF16)


def _store_vt_ext(vt_ref, vt, heads):
    ones = _ones_rows(vt.shape[1])
    for h in range(heads):
        vt_ref[h * V_EXT:h * V_EXT + MLA_V, :] = vt[h * MLA_V:(h + 1) * MLA_V, :].astype(BF16)
        vt_ref[h * V_EXT + MLA_V:(h + 1) * V_EXT, :] = ones


def _attn_step(q_refs, k_ref, vt_ref, s_bufs, m_refs, acc_refs, rel):
    w = UNIT_W
    nk = k_ref.shape[0]
    per_kv = nk // w
    specs = []
    for u in range(q_refs[0].shape[0] // w):
        d = None if rel is None else u - rel * per_kv
        if d is None or d >= per_kv:
            specs.append((u, nk, None))
        elif d >= 0:
            specs.append((u, (d + 1) * w, d * w))
    units = [(si, sp) for sp in specs for si in range(len(q_refs))]

    def scores(n):
        si, (u, rows, off) = units[n]
        s = _dot_nt(k_ref[0:rows, :], q_refs[si][u * w:(u + 1) * w, :])
        if off is not None:
            kc = lax.broadcasted_iota(I32, (rows, w), 0) // CHUNK
            qc = (lax.broadcasted_iota(I32, (rows, w), 1) + off) // CHUNK
            s = jnp.where(kc <= qc, s, -jnp.inf)
        s_bufs[n % 2][0:rows, :] = s
        return jnp.max(s, axis=0, keepdims=True)

    cmax_next = scores(0)
    for n, (si, (u, rows, _)) in enumerate(units):
        cmax = cmax_next
        if n + 1 < len(units):
            cmax_next = scores(n + 1)
        m_ref, acc_ref = m_refs[si][u], acc_refs[si][u]
        m_prev = m_ref[...]
        m_new = jnp.maximum(m_prev, cmax)
        m_ref[...] = m_new
        acc_ref[...] = jnp.exp2(m_prev - m_new) * acc_ref[...]
        p = jnp.exp2(s_bufs[n % 2][0:rows, :] - m_new).astype(BF16)
        acc_ref[...] += _dot(vt_ref[:, 0:rows], p)


def _init_stats(ms, accs):
    for m, a in zip(ms, accs):
        m[...] = jnp.full(m.shape, -jnp.inf, F32)
        a[...] = jnp.zeros(a.shape, F32)


def _normalized(acc_ref):
    acc = acc_ref[...]
    return acc[0:MLA_V, :] / acc[MLA_V:MLA_V + 1, :]


def _attn_phases(qi_ref, kj_ref, kv_per_q, init, step, finish):
    t = pl.program_id(1)
    rel = kj_ref[t] - kv_per_q * qi_ref[t]

    @pl.when(kj_ref[t] == 0)
    def _():
        init()

    @pl.when(rel < 0)
    def _():
        step(None)

    for r in range(kv_per_q):
        @pl.when(rel == r)
        def _(r=r):
            step(r)
            if r == kv_per_q - 1:
                finish()


def _unit_scratch(n_streams, bq, bk):
    nu = bq // UNIT_W
    return ([pltpu.VMEM((bk, UNIT_W), F32)] * 2
            + [pltpu.VMEM((1, UNIT_W), F32)] * (nu * n_streams)
            + [pltpu.VMEM((V_EXT, UNIT_W), F32)] * (nu * n_streams))


def _split_units(scr, n_streams):
    nu = len(scr) // (2 * n_streams)
    ms = [list(scr[s * nu:(s + 1) * nu]) for s in range(n_streams)]
    accs = [list(scr[(n_streams + s) * nu:(n_streams + s + 1) * nu]) for s in range(n_streams)]
    return ms, accs


def _mla_attn_kernel(qi_ref, kj_ref, q_ref, k_ref, vt_ref, o_ref, *scr):
    s_bufs, (ms, accs) = scr[:2], _split_units(scr[2:], 1)

    def finish():
        for u, a in enumerate(accs[0]):
            o_ref[u * UNIT_W:(u + 1) * UNIT_W, :] = _normalized(a).T.astype(o_ref.dtype)

    _attn_phases(qi_ref, kj_ref, q_ref.shape[0] // k_ref.shape[0],
                 lambda: _init_stats(ms[0], accs[0]),
                 lambda rel: _attn_step([q_ref], k_ref, vt_ref, s_bufs, ms, accs, rel),
                 finish)


def _mla_attn(qc, kc, vt):
    s = qc.shape[0]
    bq, bk = min(MLA_BQ, s), MLA_BK
    assert s % bq == 0
    qi, kj = _pair_tables(s // bq, bq // bk)
    gs = pltpu.PrefetchScalarGridSpec(
        num_scalar_prefetch=2,
        grid=(MLA_HEADS, qi.shape[0]),
        in_specs=[pl.BlockSpec((bq, MLA_QK_PAD), lambda h, t, qi, kj: (qi[t], h)),
                  pl.BlockSpec((bk, MLA_QK_PAD), lambda h, t, qi, kj: (kj[t], h)),
                  pl.BlockSpec((V_EXT, bk), lambda h, t, qi, kj: (h, kj[t]))],
        out_specs=pl.BlockSpec((bq, MLA_V), lambda h, t, qi, kj: (qi[t], h)),
        scratch_shapes=_unit_scratch(1, bq, bk),
    )
    return pl.pallas_call(
        _mla_attn_kernel,
        grid_spec=gs,
        out_shape=jax.ShapeDtypeStruct((s, MLA_HEADS * MLA_V), BF16),
        compiler_params=_cparams(("arbitrary", "arbitrary")),
        name="mla_attn",
    )(qi, kj, qc, kc, vt)


def _diff_attn_kernel(qi_ref, kj_ref, q_ref, k_ref, vt_ref, lq1_ref, lk1_ref, lq2_ref, lk2_ref,
                      g_ref, o_ref, q0_scr, q1_scr, *scr):
    s_bufs, (ms, accs) = scr[:2], _split_units(scr[2:], 2)

    def init():
        q = q_ref[...]
        lane = lax.broadcasted_iota(I32, q.shape, 1)
        zero = jnp.zeros_like(q)
        q0_scr[...] = jnp.where(lane < DIFF_HEAD_DIM, q, zero)
        q1_scr[...] = jnp.where(lane >= DIFF_HEAD_DIM, q, zero)
        _init_stats(ms[0] + ms[1], accs[0] + accs[1])

    def finish():
        lam = (jnp.exp(jnp.sum(lq1_ref[...] * lk1_ref[...], axis=1, keepdims=True))
               - jnp.exp(jnp.sum(lq2_ref[...] * lk2_ref[...], axis=1, keepdims=True))
               + LAMBDA_INIT)
        for u, (a0, a1) in enumerate(zip(accs[0], accs[1])):
            o = _normalized(a0) - lam * _normalized(a1)
            o = o * lax.rsqrt(jnp.mean(o * o, axis=0, keepdims=True) + EPS)
            o_ref[u * UNIT_W:(u + 1) * UNIT_W, :] = (
                o * g_ref[...] * (1.0 - LAMBDA_INIT)).T.astype(o_ref.dtype)

    _attn_phases(qi_ref, kj_ref, q_ref.shape[0] // k_ref.shape[0], init,
                 lambda rel: _attn_step([q0_scr, q1_scr], k_ref, vt_ref, s_bufs, ms, accs, rel),
                 finish)


def _diff_attn(dq, dk, dvt, lq1, lk1, lq2, lk2, g_col):
    s = dq.shape[0]
    bq, bk = min(DIFF_BQ, s), DIFF_BK
    assert s % bq == 0
    qi, kj = _pair_tables(s // bq, bq // bk)
    hw = 2 * DIFF_HEAD_DIM
    small = lambda a: pl.BlockSpec(a.shape, lambda h, t, qi, kj: (0, 0))
    gs = pltpu.PrefetchScalarGridSpec(
        num_scalar_prefetch=2,
        grid=(DIFF_HEADS, qi.shape[0]),
        in_specs=[pl.BlockSpec((bq, hw), lambda h, t, qi, kj: (qi[t], h)),
                  pl.BlockSpec((bk, hw), lambda h, t, qi, kj: (kj[t], h)),
                  pl.BlockSpec((V_EXT, bk), lambda h, t, qi, kj: (h, kj[t])),
                  small(lq1), small(lk1), small(lq2), small(lk2), small(g_col)],
        out_specs=pl.BlockSpec((bq, hw), lambda h, t, qi, kj: (qi[t], h)),
        scratch_shapes=[pltpu.VMEM((bq, hw), BF16), pltpu.VMEM((bq, hw), BF16)] + _unit_scratch(2, bq, bk),
    )
    return pl.pallas_call(
        _diff_attn_kernel,
        grid_spec=gs,
        out_shape=jax.ShapeDtypeStruct((s, DIFF_W), BF16),
        compiler_params=_cparams(("arbitrary", "arbitrary")),
        name="diff_attn",
    )(qi, kj, dq, dk, dvt, lq1, lk1, lq2, lk2, g_col)


def _pack_words(y):
    lo = pltpu.bitcast(y[:, :HALF].astype(BF16).astype(F32), U32) >> 16
    hi = pltpu.bitcast(y[:, HALF:].astype(BF16).astype(F32), U32) & jnp.uint32(0xFFFF0000)
    return lo | hi


def _store_packed(ref, words, rows):
    for s in range(ROW_TILES):
        ref[pl.ds(s, rows, stride=ROW_TILES), :] = words[:, s * LANES:(s + 1) * LANES]


def _load_packed(ref, rows):
    return jnp.concatenate(
        [ref[pl.ds(s, rows, stride=ROW_TILES), :] for s in range(ROW_TILES)], axis=1)


def _unpack_words(w):
    lo = pltpu.bitcast(w << 16, F32)
    hi = pltpu.bitcast(w & jnp.uint32(0xFFFF0000), F32)
    return lo, hi


def _outproj_kernel(om_ref, od_ref, x_ref, w_ref, gt_ref, gpost_ref, gpre_ref, sc_ref, sh_ref,
                    wr_ref, x1_ref, h2_ref, h2p_ref, lg_ref):
    n_sub = x_ref.shape[0] // TM_PROJ
    sub = lambda r, j: r.at[pl.ds(j * TM_PROJ, TM_PROJ)]
    w_hi, w_lo = _split_bf16(wr_ref[...])
    ne = w_hi.shape[0]
    w_hl = jnp.concatenate([w_hi, w_lo], axis=0)
    ys = [_dot(jnp.concatenate([sub(om_ref, j)[...], sub(od_ref, j)[...]], axis=1), w_ref[...])
          for j in range(n_sub)]
    for j, y in enumerate(ys):
        x1 = sub(x_ref, j)[...] + gt_ref[...] * (_rms(y) * gpost_ref[...])
        sub(x1_ref, j)[...] = x1
        h2 = _rms(x1) * gpre_ref[...] * (1.0 + sc_ref[...]) + sh_ref[...]
        h_hi, h_lo = _split_bf16(h2)
        sub(h2_ref, j)[...] = h_hi
        _store_packed(h2p_ref.at[pl.ds(j * TM_PROJ * ROW_TILES, TM_PROJ * ROW_TILES)], _pack_words(h2), TM_PROJ)
        a = _dot_nt(w_hl, h_hi)
        lg_ref[:, j * TM_PROJ:(j + 1) * TM_PROJ] = a[0:ne] + a[ne:] + _dot_nt(w_hi, h_lo)


def _outproj(om, od, x, w_out, gt, gpost, gpre, sc, sh, wr_t):
    s, d = x.shape
    tm = min(TM_OUT, s)
    row = lambda n: pl.BlockSpec((tm, n), lambda i: (i, 0))
    vec = _const_spec((1, d))
    return pl.pallas_call(
        _outproj_kernel,
        grid=(s // tm,),
        in_specs=[row(om.shape[1]), row(od.shape[1]), row(d), _const_spec(w_out.shape),
                  vec, vec, vec, vec, vec, _const_spec(wr_t.shape)],
        out_specs=[row(d), row(d),
                   pl.BlockSpec((tm * ROW_TILES, LANES), lambda i: (i, 0)),
                   pl.BlockSpec((N_EXPERTS, tm), lambda i: (0, i))],
        out_shape=[jax.ShapeDtypeStruct((s, d), F32),
                   jax.ShapeDtypeStruct((s, d), BF16),
                   jax.ShapeDtypeStruct((s * ROW_TILES, LANES), U32),
                   jax.ShapeDtypeStruct((N_EXPERTS, s), F32)],
        compiler_params=_cparams(("arbitrary",)),
        name="out_proj",
    )(om, od, x, w_out, gt, gpost, gpre, sc, sh, wr_t)


def _route_kernel(lg_ref, b_ref, eidx_ref, w_ref, rank_ref, cnt_ref, carry):
    step = pl.program_id(0)

    @pl.when(step == 0)
    def _():
        carry[...] = jnp.zeros(carry.shape, F32)

    tr = lg_ref.shape[1]
    ninf = jnp.float32(-jnp.inf)
    ie = lax.broadcasted_iota(I32, (GROUP_SIZE, tr), 0)
    rmax = lambda a: jnp.max(a, axis=0, keepdims=True)
    rmin = lambda a: jnp.min(a, axis=0, keepdims=True)
    rsum = lambda a: jnp.sum(a, axis=0, keepdims=True)

    sc, bi, gscore = [], [], []
    for g in range(N_GROUPS):
        lg = lg_ref[g * GROUP_SIZE:(g + 1) * GROUP_SIZE, :]
        s = jax.nn.sigmoid(lg)
        b = s + b_ref[g * GROUP_SIZE:(g + 1) * GROUP_SIZE, :]
        m1 = rmax(b)
        i1 = rmin(jnp.where(b == m1, ie, GROUP_SIZE))
        m2 = rmax(jnp.where(ie == i1, ninf, b))
        sc.append(s)
        bi.append(b)
        gscore.append(m1 + m2)

    gsel = [jnp.zeros((1, tr), I32) for _ in range(N_GROUPS)]
    for _ in range(TOPK_GROUPS):
        gm = functools.reduce(jnp.maximum, gscore)
        gi = functools.reduce(
            jnp.minimum, [jnp.where(gscore[g] == gm, g, N_GROUPS) for g in range(N_GROUPS)])
        for g in range(N_GROUPS):
            hit = gi == g
            gsel[g] = jnp.where(hit, 1, gsel[g])
            gscore[g] = jnp.where(hit, ninf, gscore[g])

    masked = [jnp.where(jnp.broadcast_to(gsel[g], (GROUP_SIZE, tr)) > 0, bi[g], ninf)
              for g in range(N_GROUPS)]
    sel = [jnp.zeros((GROUP_SIZE, tr), F32) for _ in range(N_GROUPS)]
    idxs, svals = [], []
    for _ in range(TOP_K):
        m = functools.reduce(jnp.maximum, [rmax(x) for x in masked])
        idx = functools.reduce(
            jnp.minimum,
            [rmin(jnp.where(masked[g] == m, ie + g * GROUP_SIZE, N_EXPERTS)) for g in range(N_GROUPS)])
        sv = jnp.zeros((1, tr), F32)
        for g in range(N_GROUPS):
            hit = (ie + g * GROUP_SIZE) == idx
            sv = sv + rsum(jnp.where(hit, sc[g], 0.0))
            masked[g] = jnp.where(hit, ninf, masked[g])
            sel[g] = jnp.where(hit, 1.0, sel[g])
        idxs.append(idx)
        svals.append(sv)
    tot = functools.reduce(lambda a, b: a + b, svals)

    sel2 = jnp.concatenate(sel, axis=0)
    before = (lax.broadcasted_iota(I32, (tr, tr), 0) < lax.broadcasted_iota(I32, (tr, tr), 1))
    rank2 = _dot(sel2.astype(BF16), before.astype(BF16)) + carry[:, 0:1]
    for k in range(TOP_K):
        rk = jnp.zeros((1, tr), F32)
        for g in range(N_GROUPS):
            hit = (ie + g * GROUP_SIZE) == idxs[k]
            rk = rk + rsum(jnp.where(hit, rank2[g * GROUP_SIZE:(g + 1) * GROUP_SIZE, :], 0.0))
        eidx_ref[k:k + 1, :] = idxs[k]
        w_ref[k:k + 1, :] = svals[k] / tot * ROUTED_SCALE
        rank_ref[k:k + 1, :] = rk.astype(I32)
    carry[...] = carry[...] + jnp.sum(sel2, axis=1, keepdims=True)
    cnt_ref[...] = carry[...]


def _route(lg_t, b_col):
    e, t = lg_t.shape
    tr = min(TR, t)
    tok = lambda: pl.BlockSpec((TOP_K, tr), lambda i: (0, i))
    return pl.pallas_call(
        _route_kernel,
        grid=(t // tr,),
        in_specs=[pl.BlockSpec((e, tr), lambda i: (0, i)), pl.BlockSpec((e, 1), lambda i: (0, 0))],
        out_specs=[tok(), tok(), tok(), pl.BlockSpec((e, LANES), lambda i: (0, 0))],
        out_shape=[jax.ShapeDtypeStruct((TOP_K, t), I32),
                   jax.ShapeDtypeStruct((TOP_K, t), F32),
                   jax.ShapeDtypeStruct((TOP_K, t), I32),
                   jax.ShapeDtypeStruct((e, LANES), F32)],
        scratch_shapes=[pltpu.VMEM((e, LANES), F32)],
        compiler_params=_cparams(("arbitrary",)),
        name="route",
    )(lg_t, b_col)


def _pos_kernel(pstart_ref, eidx_ref, rank_ref, pos_ref):
    e = eidx_ref[...]
    pos = rank_ref[...]
    for x in range(N_EXPERTS):
        pos = pos + jnp.where(e == x, pstart_ref[x], 0)
    pos_ref[...] = pos


def _positions(pstart, eidx, rank):
    k, t = eidx.shape
    tr = min(TR, t)
    gs = pltpu.PrefetchScalarGridSpec(
        num_scalar_prefetch=1,
        grid=(t // tr,),
        in_specs=[pl.BlockSpec((k, tr), lambda i, p: (0, i)), pl.BlockSpec((k, tr), lambda i, p: (0, i))],
        out_specs=pl.BlockSpec((k, tr), lambda i, p: (0, i)),
    )
    return pl.pallas_call(
        _pos_kernel, grid_spec=gs,
        out_shape=jax.ShapeDtypeStruct((k, t), I32),
        compiler_params=_cparams(("arbitrary",)),
        name="positions",
    )(pstart, eidx, rank)


def _row_copy(src, src_row, dst, dst_row, sem):
    return pltpu.make_async_copy(src.at[pl.ds(src_row * ROW_TILES, ROW_TILES)],
                                 dst.at[pl.ds(dst_row * ROW_TILES, ROW_TILES)], sem)


FILL_CHUNKS = (128, 64, 32, 16, 8, 4, 2, 1)


def _zero_fill(fs_ref, fl_ref, tail_ref, zbuf, xs_ref, sem, wait):
    def chunk(row, n):
        return pltpu.make_async_copy(zbuf.at[pl.ds(0, n * ROW_TILES)],
                                     xs_ref.at[pl.ds(row * ROW_TILES, n * ROW_TILES)], sem)

    def go(cp):
        if wait:
            cp.wait()
        else:
            cp.start()

    def per_expert(e, c):
        pad = fl_ref[e]
        row = fs_ref[e]
        for n in FILL_CHUNKS:
            @pl.when((pad & n) != 0)
            def _(row=row, n=n):
                go(chunk(row, n))
            row = row + (pad & n)
        return c

    lax.fori_loop(0, N_EXPERTS, per_expert, 0)

    def per_tail(b, c):
        for part in range(EXP_BLK // FILL_CHUNKS[0]):
            go(chunk(b * EXP_BLK + part * FILL_CHUNKS[0], FILL_CHUNKS[0]))
        return c

    lax.fori_loop(tail_ref[0], tail_ref[1], per_tail, 0)


def _dispatch_kernel(fs_ref, fl_ref, tail_ref, pos_ref, h_ref, xs_ref, zbuf, sem, fill_sem):
    td = pos_ref.shape[1]
    step = pl.program_id(0)

    @pl.when(step == 0)
    def _():
        zbuf[...] = jnp.zeros(zbuf.shape, zbuf.dtype)
        _zero_fill(fs_ref, fl_ref, tail_ref, zbuf, xs_ref, fill_sem, wait=False)

    def issue(t, c):
        for k in range(TOP_K):
            _row_copy(h_ref, t, xs_ref, pos_ref[k, t], sem).start(priority=k % 2)
        return c

    lax.fori_loop(0, td, issue, 0, unroll=4)
    for k in range(TOP_K):
        pltpu.make_async_copy(h_ref, xs_ref.at[pl.ds(0, td * ROW_TILES)], sem).wait()

    @pl.when(step == pl.num_programs(0) - 1)
    def _():
        _zero_fill(fs_ref, fl_ref, tail_ref, zbuf, xs_ref, fill_sem, wait=True)


def _dispatch(fill_start, fill_len, tail, pos, h2p, m_pad):
    k, t = pos.shape
    td = min(TD, t)
    gs = pltpu.PrefetchScalarGridSpec(
        num_scalar_prefetch=3,
        grid=(t // td,),
        in_specs=[pl.BlockSpec((k, td), lambda i, *_: (0, i), memory_space=pltpu.SMEM),
                  pl.BlockSpec((td * ROW_TILES, LANES), lambda i, *_: (i, 0))],
        out_specs=pl.BlockSpec(memory_space=pl.ANY),
        scratch_shapes=[pltpu.VMEM((FILL_CHUNKS[0] * ROW_TILES, LANES), U32),
                        pltpu.SemaphoreType.DMA, pltpu.SemaphoreType.DMA],
    )
    return pl.pallas_call(
        _dispatch_kernel,
        grid_spec=gs,
        out_shape=jax.ShapeDtypeStruct((m_pad * ROW_TILES, LANES), U32),
        compiler_params=_cparams(("arbitrary",)),
        name="dispatch",
    )(fill_start, fill_len, tail, pos, h2p)


def _weight_copies(w_hbm, w_buf, sem, e, s):
    return [pltpu.make_async_copy(h.at[e], b.at[s], sem.at[s]) for h, b in zip(w_hbm, w_buf)]


def _experts_kernel(be_ref, nu_ref, fresh_ref, nxt_ref, slot_ref, x_ref, wg_hbm, wu_hbm, wd_hbm, y_ref,
                    wg_buf, wu_buf, wd_buf, wg_s, wu_s, wd_s, sem):
    step = pl.program_id(0)
    b0 = step * BLKS_PER_STEP
    blk_rows = x_ref.shape[0] // BLKS_PER_STEP
    w_hbm = (wg_hbm, wu_hbm, wd_hbm)
    w_buf = (wg_buf, wu_buf, wd_buf)

    def maybe_weights(b):
        @pl.when(fresh_ref[b] == 1)
        def _():
            s = slot_ref[b]

            @pl.when(b == 0)
            def _():
                for c in _weight_copies(w_hbm, w_buf, sem, be_ref[0], 0):
                    c.start()

            for c in _weight_copies(w_hbm, w_buf, sem, be_ref[b], s):
                c.wait()

            @pl.when(nxt_ref[b] >= 0)
            def _():
                for c in _weight_copies(w_hbm, w_buf, sem, nxt_ref[b], 1 - s):
                    c.start()

            wg_s[...] = wg_buf[s].astype(BF16)
            wu_s[...] = wu_buf[s].astype(BF16)
            wd_s[...] = wd_buf[s].astype(BF16)

    def swiglu(first, n):
        x_blk = x_ref.at[pl.ds(first * blk_rows, n * blk_rows)]
        y_blk = y_ref.at[pl.ds(first * blk_rows, n * blk_rows)]
        rows = n * blk_rows // ROW_TILES
        xa, xb = _unpack_words(_load_packed(x_blk, rows))
        xa = xa.astype(BF16)
        xb = xb.astype(BF16)
        g = _dot(xa, wg_s[0:HALF, :]) + _dot(xb, wg_s[HALF:, :])
        u = _dot(xa, wu_s[0:HALF, :]) + _dot(xb, wu_s[HALF:, :])
        a = (g * jax.nn.sigmoid(g) * u).astype(BF16)
        _store_packed(y_blk, _pack_words(_dot(a, wd_s[...])), rows)

    def zero(first, n):
        y_ref[pl.ds(first * blk_rows, n * blk_rows), :] = jnp.zeros((n * blk_rows, LANES), y_ref.dtype)

    n_used = jnp.clip(nu_ref[0] - b0, 0, BLKS_PER_STEP)
    same = jnp.logical_and(n_used == BLKS_PER_STEP, be_ref[b0] == be_ref[b0 + 1])

    @pl.when(same)
    def _():
        maybe_weights(b0)
        swiglu(0, BLKS_PER_STEP)

    @pl.when(jnp.logical_and(jnp.logical_not(same), n_used > 0))
    def _():
        maybe_weights(b0)
        swiglu(0, 1)

        @pl.when(n_used > 1)
        def _():
            maybe_weights(b0 + 1)
            swiglu(1, 1)

        @pl.when(n_used == 1)
        def _():
            zero(1, 1)

    @pl.when(n_used == 0)
    def _():
        zero(0, BLKS_PER_STEP)


def _experts(blk_e, nused, fresh, nxt_e, slot, xs, w_gate, w_up, w_down, blk):
    n_blk = blk_e.shape[0]
    assert n_blk % BLKS_PER_STEP == 0 and BLKS_PER_STEP == 2
    ne, d, de = w_gate.shape
    rows = BLKS_PER_STEP * blk * ROW_TILES
    xmap = lambda i, be, nu, fr, nx, sl: (jnp.minimum(i, (nu[0] - 1) // BLKS_PER_STEP), 0)
    hbm = pl.BlockSpec(memory_space=pl.ANY)
    gs = pltpu.PrefetchScalarGridSpec(
        num_scalar_prefetch=5,
        grid=(n_blk // BLKS_PER_STEP,),
        in_specs=[pl.BlockSpec((rows, LANES), xmap), hbm, hbm, hbm],
        out_specs=pl.BlockSpec((rows, LANES), lambda i, *_: (i, 0)),
        scratch_shapes=[pltpu.VMEM((2, d, de), F32), pltpu.VMEM((2, d, de), F32), pltpu.VMEM((2, de, d), F32),
                        pltpu.VMEM((d, de), BF16), pltpu.VMEM((d, de), BF16), pltpu.VMEM((de, d), BF16),
                        pltpu.SemaphoreType.DMA((2,))],
    )
    return pl.pallas_call(
        _experts_kernel, grid_spec=gs,
        out_shape=jax.ShapeDtypeStruct(xs.shape, U32),
        compiler_params=_cparams(("arbitrary",)),
        name="experts",
    )(blk_e, nused, fresh, nxt_e, slot, xs, w_gate, w_up, w_down)


def _combine_kernel(pos_ref, wt_ref, ys_ref, h_ref, x1_ref, wsg_ref, wsu_ref, wsd_ref,
                    gt_ref, gpost_ref, o_ref, gbuf, sem):
    td = pos_ref.shape[1]

    def issue(t, c):
        for k in range(TOP_K):
            _row_copy(ys_ref, pos_ref[k, t], gbuf.at[k], t, sem).start(priority=k % 2)
        return c

    lax.fori_loop(0, td, issue, 0, unroll=4)

    hb = h_ref[...]
    g = _dot(hb, wsg_ref[...])
    u = _dot(hb, wsu_ref[...])
    y = _dot((g * jax.nn.sigmoid(g) * u).astype(BF16), wsd_ref[...])

    wpad = jnp.concatenate([wt_ref[...], jnp.zeros((LANES - TOP_K, td), F32)], axis=0)
    wcol = jnp.concatenate([wpad[:, j * LANES:(j + 1) * LANES].T for j in range(td // LANES)], axis=0)

    for k in range(TOP_K):
        pltpu.make_async_copy(ys_ref.at[pl.ds(0, td * ROW_TILES)], gbuf.at[k], sem).wait()
    lo = jnp.zeros((td, HALF), F32)
    hi = jnp.zeros((td, HALF), F32)
    for k in range(TOP_K):
        a, b = _unpack_words(_load_packed(gbuf.at[k], td))
        wk = wcol[:, k:k + 1]
        lo = lo + wk * a
        hi = hi + wk * b
    y = y + jnp.concatenate([lo, hi], axis=1)
    o_ref[...] = x1_ref[...] + gt_ref[...] * (_rms(y) * gpost_ref[...])


def _combine(pos, wts, ys, h2, x1, wsg, wsu, wsd, gt, gpost):
    k, t = pos.shape
    d = x1.shape[1]
    td = min(TD, t)
    assert td % LANES == 0
    row = lambda n: pl.BlockSpec((td, n), lambda i: (i, 0))
    vec = _const_spec((1, d))
    return pl.pallas_call(
        _combine_kernel,
        grid=(t // td,),
        in_specs=[pl.BlockSpec((k, td), lambda i: (0, i), memory_space=pltpu.SMEM),
                  pl.BlockSpec((k, td), lambda i: (0, i)),
                  pl.BlockSpec(memory_space=pl.ANY),
                  row(d), row(d), _const_spec(wsg.shape), _const_spec(wsu.shape), _const_spec(wsd.shape),
                  vec, vec],
        out_specs=row(d),
        out_shape=jax.ShapeDtypeStruct((t, d), F32),
        scratch_shapes=[pltpu.VMEM((k, td * ROW_TILES, LANES), U32), pltpu.SemaphoreType.DMA],
        compiler_params=_cparams(("arbitrary",)),
        name="combine",
    )(pos, wts, ys, h2, x1, wsg, wsu, wsd, gt, gpost)


def _rope_tables(seq, dim):
    pos = jnp.arange(seq, dtype=F32)
    inv = ROPE_THETA ** (-jnp.arange(0, dim, 2, dtype=F32) / dim)
    ang = pos[:, None] * inv[None, :]
    return jnp.cos(ang), jnp.sin(ang)


def _lane_tables(seq, dim, period):
    cos, sin = _rope_tables(seq, dim)
    half = dim // 2
    cos = jnp.tile(cos, (1, LANES // half))
    sin = jnp.tile(sin, (1, LANES // half))
    r = jnp.arange(LANES) % period
    c = jnp.where(r < dim, cos, 1.0)
    sa = jnp.where(r < half, -sin, 0.0)
    sb = jnp.where((r >= half) & (r < dim), sin, 0.0)
    return c, sa, sb


def _moe(h2, h2p, x1, lg_t, b_router, w_gate, w_up, w_down, wsg, wsu, wsd, gt_f, g_post_ffn):
    t = h2.shape[0]
    blk = EXP_BLK
    m_pad = t * TOP_K + N_EXPERTS * blk
    eidx, wts, rank, cnt = _route(lg_t, b_router.reshape(N_EXPERTS, 1))
    counts = cnt[:, 0].astype(I32)
    padded = (counts + blk - 1) // blk * blk
    pend = jnp.cumsum(padded)
    pstart = pend - padded
    n_blk = m_pad // blk
    blk_start = jnp.arange(n_blk, dtype=I32) * blk
    blk_e = jnp.minimum(jnp.sum(pend[None, :] <= blk_start[:, None], axis=1), N_EXPERTS - 1).astype(I32)
    nused = (pend[-1:] // blk).astype(I32)
    pos = _positions(pstart.astype(I32), eidx, rank)
    tail = jnp.concatenate([nused, jnp.full((1,), n_blk, I32)])
    xs = _dispatch((pstart + counts).astype(I32), (padded - counts).astype(I32), tail, pos, h2p, m_pad)
    idx = jnp.arange(n_blk, dtype=I32)
    fresh = (idx < nused[0]) & (blk_e != jnp.concatenate([jnp.full((1,), -1, I32), blk_e[:-1]]))
    slot = ((jnp.cumsum(fresh.astype(I32)) - 1) % 2).astype(I32)
    first_at_or_after = lax.cummin(jnp.where(fresh, idx, n_blk)[::-1])[::-1]
    nxt_idx = jnp.concatenate([first_at_or_after[1:], jnp.full((1,), n_blk, I32)])
    nxt_e = jnp.where(nxt_idx < n_blk, blk_e[jnp.minimum(nxt_idx, n_blk - 1)], -1).astype(I32)
    ys = _experts(blk_e, nused, fresh.astype(I32), nxt_e, slot, xs, w_gate, w_up, w_down, blk)
    return _combine(pos, wts, ys, h2, x1, wsg, wsu, wsd, gt_f, g_post_ffn)


def _layer(x, c, w_ada, b_ada, g_pre_mix, w_in, g_q_lat, w_uq, g_kv_lat, w_ukv,
           lq1, lk1, lq2, lk2, g_diff_sub, w_out, g_post_mix, g_pre_ffn, w_router, b_router,
           w_gate, w_up, w_down, ws_gate, ws_up, ws_down, g_post_ffn):
    s, d = x.shape
    row = lambda a: a.reshape(1, -1)

    mod = _ada(c, w_ada, row(b_ada))
    sh_a, sc_a, gt_a, sh_f, sc_f, gt_f = [mod[:, i * d:(i + 1) * d] for i in range(6)]

    o1 = MLA_Q_RANK + MLA_KV_RANK
    o2 = o1 + MLA_ROPE
    o3 = o2 + 2 * DIFF_W
    w_lat, w_qk, w_kpe, wdv_t = _split_w_in(w_in.T, (o1, o2, o3))
    wq = w_uq.reshape(MLA_Q_RANK, MLA_HEADS, MLA_NOPE + MLA_ROPE)
    wq = jnp.pad(wq, ((0, 0), (0, 0), (0, MLA_QK_PAD - MLA_NOPE - MLA_ROPE)))
    wuq_r = wq.reshape(MLA_Q_RANK, MLA_HEADS * MLA_QK_PAD).astype(BF16)
    wkv = w_ukv.reshape(MLA_KV_RANK, MLA_HEADS, MLA_NOPE + MLA_V)
    wuk_r = wkv[:, :, :MLA_NOPE].reshape(MLA_KV_RANK, -1).astype(BF16)
    wmv_t = wkv[:, :, MLA_NOPE:].reshape(MLA_KV_RANK, -1).T.astype(BF16)
    dq, dk, dvt, qc, kc, vt = _inproj(
        x, row(g_pre_mix), sc_a, sh_a, w_lat, w_qk, w_kpe, wdv_t,
        _lane_tables(s, DIFF_ROT, DIFF_HEAD_DIM), row(g_q_lat), row(g_kv_lat), wuq_r, wuk_r, wmv_t,
        _lane_tables(s, MLA_ROPE, MLA_ROPE))

    o_mla = _mla_attn(qc, kc, vt)
    o_diff = _diff_attn(dq, dk, dvt, row(lq1), row(lk1), row(lq2), row(lk2), g_diff_sub.reshape(-1, 1))

    x1, h2, h2p, lg_t = _outproj(o_mla, o_diff, x, w_out.astype(BF16), gt_a, row(g_post_mix),
                                 row(g_pre_ffn), sc_f, sh_f, w_router.T)
    return _moe(h2, h2p, x1, lg_t, b_router, w_gate, w_up, w_down,
                ws_gate.astype(BF16), ws_up.astype(BF16), ws_down.astype(BF16), gt_f, row(g_post_ffn))


def kernel(x, c, w_ada, b_ada, g_pre_mix, w_in, g_q_lat, w_uq, g_kv_lat, w_ukv, lambda_q1, lambda_k1, lambda_q2, lambda_k2, g_diff_sub, w_out, g_post_mix, g_pre_ffn, w_router, b_router, w_gate, w_up, w_down, ws_gate, ws_up, ws_down, g_post_ffn):
    assert x.shape[0] == 1 and w_ada.shape[0] == 1
    out = _layer(x[0], c, w_ada[0], b_ada[0], g_pre_mix[0], w_in[0], g_q_lat[0], w_uq[0],
                 g_kv_lat[0], w_ukv[0], lambda_q1[0], lambda_k1[0], lambda_q2[0], lambda_k2[0],
                 g_diff_sub[0], w_out[0], g_post_mix[0], g_pre_ffn[0], w_router[0], b_router[0],
                 w_gate[0], w_up[0], w_down[0], ws_gate[0], ws_up[0], ws_down[0], g_post_ffn[0])
    return out[None]
```
